```python
import jax, jax.numpy as jnp
from jax import lax
import numpy as np

D_MODEL = 1024
BATCH = 2
SEQ = 8192
DEPTH = 4

MLA_HEADS = 8
MLA_Q_LORA = 256
MLA_KV_LORA = 128
MLA_NOPE = 64
MLA_ROPE = 32
MLA_V = 64
SWA_HEADS = 8
SWA_KV_HEADS = 2
SWA_HEAD_DIM = 64
SWA_WINDOW = 128
SB_HEADS = 8
SB_HEAD_DIM = 64
D_FF = 4 * D_MODEL
BLOCK = 128
ROPE_THETA = 10000.0
EPS = 1e-6
N_BRANCHES = 3
MIX_A = MLA_HEADS * MLA_V
MIX_B = SWA_HEADS * SWA_HEAD_DIM
MIX_C = SB_HEADS * SB_HEAD_DIM

SPLIT_SIZES = (
    MLA_Q_LORA, MLA_KV_LORA, MLA_ROPE,
    SWA_HEADS * SWA_HEAD_DIM, SWA_KV_HEADS * SWA_HEAD_DIM, SWA_KV_HEADS * SWA_HEAD_DIM,
    SB_HEADS * SB_HEAD_DIM, SB_HEADS * SB_HEAD_DIM, SB_HEADS * SB_HEAD_DIM,
    N_BRANCHES * D_MODEL,
)
IN_WIDTH = sum(SPLIT_SIZES)
SPLIT_POINTS = [int(v) for v in np.cumsum(SPLIT_SIZES)[:-1]]

kernel_name = "hybrid_mla_swa_sinks_stickbreak_gated"


def rms_norm(x, g):
    xf = x.astype(jnp.float32)
    y = xf * lax.rsqrt(jnp.mean(xf * xf, axis=-1, keepdims=True) + EPS)
    return (y * g.astype(jnp.float32)).astype(x.dtype)


def rope(x, positions):
    d = x.shape[-1]
    inv = 1.0 / (ROPE_THETA ** (jnp.arange(0, d, 2, dtype=jnp.float32) / d))
    ang = positions.astype(jnp.float32)[..., None] * inv
    cos = jnp.cos(ang)[:, :, None, :]
    sin = jnp.sin(ang)[:, :, None, :]
    xf = x.astype(jnp.float32)
    x1, x2 = xf[..., : d // 2], xf[..., d // 2:]
    return jnp.concatenate([x1 * cos - x2 * sin, x2 * cos + x1 * sin], axis=-1).astype(x.dtype)


def to_blocks(t):
    b, s = t.shape[:2]
    return jnp.moveaxis(t.reshape(b, s // BLOCK, BLOCK, *t.shape[2:]), 1, 0)


def from_blocks(t):
    t = jnp.moveaxis(t, 0, 1)
    return t.reshape(t.shape[0], t.shape[1] * t.shape[2], *t.shape[3:])


def mla_branch(c_q, c_kv, k_rope, positions, g_q_lat, g_kv_lat, w_uq, w_ukv):
    B, S, _ = c_q.shape
    q = (rms_norm(c_q, g_q_lat) @ w_uq).reshape(B, S, MLA_HEADS, MLA_NOPE + MLA_ROPE)
    q_nope = q[..., :MLA_NOPE]
    q_pe = rope(q[..., MLA_NOPE:], positions)
    kv = (rms_norm(c_kv, g_kv_lat) @ w_ukv).reshape(B, S, MLA_HEADS, MLA_NOPE + MLA_V)
    k_nope, v = kv[..., :MLA_NOPE], kv[..., MLA_NOPE:]
    k_pe = rope(k_rope[:, :, None, :], positions)[:, :, 0]
    scale = (MLA_NOPE + MLA_ROPE) ** -0.5
    key_pos = jnp.arange(S)

    def block(args):
        qn, qp, i = args
        s = (jnp.einsum('bqhd,bkhd->bhqk', qn, k_nope, preferred_element_type=jnp.float32)
             + jnp.einsum('bqhd,bkd->bhqk', qp, k_pe, preferred_element_type=jnp.float32)) * scale
        q_pos = i * BLOCK + jnp.arange(BLOCK)
        s = jnp.where(key_pos[None, :] <= q_pos[:, None], s, -jnp.inf)
        p = jax.nn.softmax(s, axis=-1).astype(v.dtype)
        return jnp.einsum('bhqk,bkhd->bqhd', p, v)

    out = lax.map(block, (to_blocks(q_nope), to_blocks(q_pe), jnp.arange(S // BLOCK)))
    return from_blocks(out).reshape(B, S, MIX_A)


def swa_branch(q, k, v, positions, sinks):
    B, S, _ = q.shape
    G = SWA_HEADS // SWA_KV_HEADS
    n = S // BLOCK
    q = rope(q.reshape(B, S, SWA_HEADS, SWA_HEAD_DIM), positions)
    k = rope(k.reshape(B, S, SWA_KV_HEADS, SWA_HEAD_DIM), positions)
    v = v.reshape(B, S, SWA_KV_HEADS, SWA_HEAD_DIM)
    qb = q.reshape(B, n, BLOCK, SWA_KV_HEADS, G, SWA_HEAD_DIM)

    def band(t):
        tb = t.reshape(B, n, BLOCK, *t.shape[2:])
        prev = jnp.pad(tb, ((0, 0), (1, 0), (0, 0), (0, 0), (0, 0)))[:, :-1]
        return jnp.concatenate([prev, tb], axis=2)

    kb, vb = band(k), band(v)
    s = jnp.einsum('bnqkgd,bnskd->bnkgqs', qb, kb, preferred_element_type=jnp.float32) * SWA_HEAD_DIM ** -0.5
    qi = jnp.arange(BLOCK)[:, None] + BLOCK
    si = jnp.arange(2 * BLOCK)[None, :]
    diff = qi - si
    valid = (diff >= 0) & (diff < SWA_WINDOW)
    blk = jnp.arange(n)
    valid = valid[None] & ((blk[:, None, None] > 0) | (si[None] >= BLOCK))
    s = jnp.where(valid[None, :, None, None], s, -jnp.inf)
    sink = sinks.astype(jnp.float32).reshape(SWA_KV_HEADS, G)[None, None, :, :, None, None]
    m = jnp.maximum(jnp.max(s, axis=-1, keepdims=True), sink)
    p = jnp.exp(s - m)
    p = (p / (jnp.sum(p, axis=-1, keepdims=True) + jnp.exp(sink - m))).astype(v.dtype)
    o = jnp.einsum('bnkgqs,bnskd->bnqkgd', p, vb)
    return o.reshape(B, S, MIX_B)


def stick_breaking_branch(q, k, v):
    B, S, _ = q.shape
    q = q.reshape(B, S, SB_HEADS, SB_HEAD_DIM)
    k = k.reshape(B, S, SB_HEADS, SB_HEAD_DIM)
    v = v.reshape(B, S, SB_HEADS, SB_HEAD_DIM)
    scale = SB_HEAD_DIM ** -0.5
    key_pos = jnp.arange(S)

    def block(args):
        qb, i = args
        z = jnp.einsum('bqhd,bkhd->bhqk', qb, k, preferred_element_type=jnp.float32) * scale
        q_pos = i * BLOCK + jnp.arange(BLOCK)
        before = key_pos[None, :] < q_pos[:, None]
        log_1m_beta = jnp.where(before, jax.nn.log_sigmoid(-z), 0.0)
        tail = lax.cumsum(log_1m_beta, axis=3, reverse=True) - log_1m_beta
        a = jnp.where(before, jnp.exp(jax.nn.log_sigmoid(z) + tail), 0.0).astype(v.dtype)
        return jnp.einsum('bhqk,bkhd->bqhd', a, v)

    out = lax.map(block, (to_blocks(q), jnp.arange(S // BLOCK)))
    return from_blocks(out).reshape(B, S, MIX_C)


def hybrid_layer(x, positions, g_mix_pre, w_in, b_gate, g_q_lat, g_kv_lat, w_uq, w_ukv, swa_sinks,
                 w_o_mla, w_o_swa, w_o_sb, w_out, g_mix_post, g_mlp_pre, w_up, w_down, g_mlp_post):
    B, S, D = x.shape
    h = rms_norm(x, g_mix_pre)
    proj = h @ w_in
    (c_q, c_kv, k_rope, q_swa, k_swa, v_swa, q_sb, k_sb, v_sb, gate_logits) = jnp.split(proj, SPLIT_POINTS, axis=-1)
    o_a = mla_branch(c_q, c_kv, k_rope, positions, g_q_lat, g_kv_lat, w_uq, w_ukv) @ w_o_mla
    o_b = swa_branch(q_swa, k_swa, v_swa, positions, swa_sinks) @ w_o_swa
    o_c = stick_breaking_branch(q_sb, k_sb, v_sb) @ w_o_sb
    gates = jax.nn.sigmoid((gate_logits + b_gate).astype(jnp.float32)).astype(x.dtype)
    gates = gates.reshape(B, S, N_BRANCHES, D)
    mixed = gates[:, :, 0] * o_a + gates[:, :, 1] * o_b + gates[:, :, 2] * o_c
    x = x + rms_norm(mixed @ w_out, g_mix_post)
    h = rms_norm(x, g_mlp_pre)
    u = jnp.square(jax.nn.relu(h @ w_up))
    return x + rms_norm(u @ w_down, g_mlp_post)


def setup_inputs(seed: int = 0) -> dict:
    key = jax.random.key(seed)
    ks = jax.random.split(key, 24)
    f32 = jnp.float32

    def dense(k, fan_in, fan_out):
        return jax.random.normal(k, (DEPTH, fan_in, fan_out), f32) * fan_in ** -0.5

    def gain(k, n):
        return 1.0 + 0.05 * jax.random.normal(k, (DEPTH, n), f32)

    x = jax.random.normal(ks[0], (BATCH, SEQ, D_MODEL), f32)
    start = jax.random.randint(ks[1], (BATCH, 1), 0, 1024, dtype=jnp.int32)
    positions = start + jnp.arange(SEQ, dtype=jnp.int32)[None, :]
    return {
        "x": x,
        "positions": positions,
        "g_mix_pre": gain(ks[2], D_MODEL),
        "w_in": dense(ks[3], D_MODEL, IN_WIDTH),
        "b_gate": 0.02 * jax.random.normal(ks[4], (DEPTH, N_BRANCHES * D_MODEL), f32),
        "g_q_lat": gain(ks[5], MLA_Q_LORA),
        "g_kv_lat": gain(ks[6], MLA_KV_LORA),
        "w_uq": dense(ks[7], MLA_Q_LORA, MLA_HEADS * (MLA_NOPE + MLA_ROPE)),
        "w_ukv": dense(ks[8], MLA_KV_LORA, MLA_HEADS * (MLA_NOPE + MLA_V)),
        "swa_sinks": 0.5 * jax.random.normal(ks[9], (DEPTH, SWA_HEADS), f32),
        "w_o_mla": dense(ks[10], MIX_A, D_MODEL),
        "w_o_swa": dense(ks[11], MIX_B, D_MODEL),
        "w_o_sb": dense(ks[12], MIX_C, D_MODEL),
        "w_out": dense(ks[13], D_MODEL, D_MODEL),
        "g_mix_post": gain(ks[14], D_MODEL),
        "g_mlp_pre": gain(ks[15], D_MODEL),
        "w_up": dense(ks[16], D_MODEL, D_FF),
        "w_down": dense(ks[17], D_FF, D_MODEL),
        "g_mlp_post": gain(ks[18], D_MODEL),
    }


def reference(x, positions, g_mix_pre, w_in, b_gate, g_q_lat, g_kv_lat, w_uq, w_ukv, swa_sinks,
              w_o_mla, w_o_swa, w_o_sb, w_out, g_mix_post, g_mlp_pre, w_up, w_down, g_mlp_post):
    for l in range(DEPTH):
        x = hybrid_layer(x, positions, g_mix_pre[l], w_in[l], b_gate[l], g_q_lat[l], g_kv_lat[l],
                         w_uq[l], w_ukv[l], swa_sinks[l], w_o_mla[l], w_o_swa[l], w_o_sb[l], w_out[l],
                         g_mix_post[l], g_mlp_pre[l], w_up[l], w_down[l], g_mlp_post[l])
    return x
```

```python
import functools

import jax
import jax.numpy as jnp
from jax import lax
from jax.experimental import pallas as pl
from jax.experimental.pallas import tpu as pltpu

F32 = jnp.float32
BF16 = jnp.bfloat16

D_MODEL = 1024
DEPTH = 4
MLA_HEADS = 8
MLA_Q_LORA = 256
MLA_KV_LORA = 128
MLA_NOPE = 64
MLA_ROPE = 32
MLA_V = 64
SWA_HEADS = 8
SWA_KV_HEADS = 2
SWA_HEAD_DIM = 64
SWA_WINDOW = 128
SB_HEADS = 8
SB_HEAD_DIM = 64
D_FF = 4 * D_MODEL
ROPE_THETA = 10000.0
EPS = 1e-6
N_BRANCHES = 3

LANES = 128
HEAD_PAIRS = 4
MLA_HEAD_PAD = 128
NEG_BIG = -1e30
VMEM_LIMIT = 56 * 1024 * 1024

OFF_CQ = 0
OFF_CKV = OFF_CQ + MLA_Q_LORA
OFF_KPE = OFF_CKV + MLA_KV_LORA
OFF_KPE_ROT = OFF_KPE + LANES
OFF_QS = OFF_KPE_ROT + LANES
OFF_QS_ROT = OFF_QS + 512
OFF_KS = OFF_QS_ROT + 512
OFF_KS_ROT = OFF_KS + 256
OFF_VS = OFF_KS_ROT + 256
OFF_QB = OFF_VS + 256
OFF_KB = OFF_QB + 512
OFF_VB = OFF_KB + 512
OFF_GATE = OFF_VB + 512
W1_COLS = OFF_GATE + N_BRANCHES * D_MODEL

PREP_TM = 256
POST_TM = 256
MLA_TQ = 512
MLA_TK = 512
SB_TQ = 512
SB_TK = 256
SWA_TQ = 256
ROPE_TM = 2048


def _rms(x, g):
    return x * lax.rsqrt(jnp.mean(x * x, axis=-1, keepdims=True) + EPS) * g


def _dot(a, b):
    return jnp.dot(a, b, preferred_element_type=F32)


def _dot_nt(a, b):
    return lax.dot_general(a, b, (((1,), (1,)), ((), ())), preferred_element_type=F32)


def _const_spec(shape):
    return pl.BlockSpec(shape, lambda *_: (0,) * len(shape), pipeline_mode=pl.Buffered(1))


def _params(sem):
    return pltpu.CompilerParams(dimension_semantics=sem, vmem_limit_bytes=VMEM_LIMIT)


def _rope_table_kernel(pos_ref, inva_ref, maska_ref, invb_ref, ca_ref, sa_ref, cb_ref, sb_ref):
    pos = pos_ref[...].astype(F32)
    ang_a = pos * inva_ref[...]
    ca_ref[...] = jnp.cos(ang_a) * maska_ref[...]
    sa_ref[...] = jnp.sin(ang_a)
    ang_b = pos * invb_ref[...]
    cb_ref[...] = jnp.cos(ang_b)
    sb_ref[...] = jnp.sin(ang_b)


def _rope_tables(positions):
    t = positions.size
    pos = positions.reshape(t, 1)
    inv_a16 = 1.0 / (ROPE_THETA ** (jnp.arange(0, MLA_ROPE, 2, dtype=F32) / MLA_ROPE))
    inv_b32 = 1.0 / (ROPE_THETA ** (jnp.arange(0, SWA_HEAD_DIM, 2, dtype=F32) / SWA_HEAD_DIM))
    zeros = jnp.zeros
    inv_a = jnp.concatenate([zeros((MLA_NOPE,), F32), inv_a16, inv_a16, zeros((32,), F32)]).reshape(1, LANES)
    mask_a = jnp.concatenate([jnp.ones((MLA_NOPE + MLA_ROPE,), F32), zeros((32,), F32)]).reshape(1, LANES)
    inv_b = jnp.tile(inv_b32, 4).reshape(1, LANES)
    row = pl.BlockSpec((ROPE_TM, LANES), lambda i: (i, 0))
    vec = pl.BlockSpec((1, LANES), lambda i: (0, 0))
    out = jax.ShapeDtypeStruct((t, LANES), F32)
    return pl.pallas_call(
        _rope_table_kernel,
        grid=(t // ROPE_TM,),
        in_specs=[pl.BlockSpec((ROPE_TM, 1), lambda i: (i, 0)), vec, vec, vec],
        out_specs=[row, row, row, row],
        out_shape=[out, out, out, out],
        compiler_params=_params(("parallel",)),
        name="rope_tables",
    )(pos, inv_a, mask_a, inv_b)


def _prep_kernel(x_ref, g_ref, ca_ref, sa_ref, cb_ref, sb_ref, w1_ref, bg_ref, gq_ref, gkv_ref,
                 wq_ref, wkv_ref,
                 qm_ref, km_ref, vm_ref, qs_ref, ks_ref, vs_ref, qb_ref, kb_ref, vb_ref, gate_ref):
    h = _rms(x_ref[...], g_ref[...]).astype(BF16)

    def mm(lo, n):
        return _dot(h, w1_ref[:, lo:lo + n])

    ca, sa, cb, sb = ca_ref[...], sa_ref[...], cb_ref[...], sb_ref[...]

    cb4 = jnp.concatenate([cb] * 4, axis=1)
    sb4 = jnp.concatenate([sb] * 4, axis=1)
    qs_ref[...] = (mm(OFF_QS, 512) * cb4 + mm(OFF_QS_ROT, 512) * sb4).astype(BF16)
    ks_ref[...] = (mm(OFF_KS, 256) * cb4[:, :256] + mm(OFF_KS_ROT, 256) * sb4[:, :256]).astype(BF16)
    vs_ref[...] = mm(OFF_VS, 256).astype(BF16)

    qb_ref[...] = mm(OFF_QB, 512).astype(BF16)
    kb_ref[...] = mm(OFF_KB, 512).astype(BF16)
    vb_ref[...] = mm(OFF_VB, 512).astype(BF16)

    gate_ref[...] = jax.nn.sigmoid(mm(OFF_GATE, N_BRANCHES * D_MODEL) + bg_ref[...])

    scale = (MLA_NOPE + MLA_ROPE) ** -0.5
    cqn = _rms(mm(OFF_CQ, MLA_Q_LORA), gq_ref[...]).astype(BF16)
    ca8 = jnp.concatenate([ca * scale] * MLA_HEADS, axis=1)
    sa8 = jnp.concatenate([sa * scale] * MLA_HEADS, axis=1)
    nq = MLA_HEADS * MLA_HEAD_PAD
    qm_ref[...] = (_dot(cqn, wq_ref[:, :nq]) * ca8 + _dot(cqn, wq_ref[:, nq:]) * sa8).astype(BF16)
    kpe = mm(OFF_KPE, LANES) * ca + mm(OFF_KPE_ROT, LANES) * sa
    ckvn = _rms(mm(OFF_CKV, MLA_KV_LORA), gkv_ref[...]).astype(BF16)
    kpe8 = jnp.concatenate([kpe] * MLA_HEADS, axis=1)
    km_ref[...] = (_dot(ckvn, wkv_ref[:, :nq]) + kpe8).astype(BF16)
    vm_ref[...] = _dot(ckvn, wkv_ref[:, nq:]).astype(BF16)


def _prep(x, tables, g_pre, w1, b_gate, g_q, g_kv, wq, wkv):
    t = x.shape[0]
    tm = PREP_TM
    ca, sa, cb, sb = tables

    def row(n):
        return pl.BlockSpec((tm, n), lambda i: (i, 0))

    def out(n, dt=BF16):
        return jax.ShapeDtypeStruct((t, n), dt)

    nq = MLA_HEADS * MLA_HEAD_PAD
    return pl.pallas_call(
        _prep_kernel,
        grid=(t // tm,),
        in_specs=[row(D_MODEL), _const_spec((1, D_MODEL)), row(LANES), row(LANES), row(LANES), row(LANES),
                  _const_spec(w1.shape), _const_spec(b_gate.shape), _const_spec(g_q.shape),
                  _const_spec(g_kv.shape), _const_spec(wq.shape), _const_spec(wkv.shape)],
        out_specs=[row(nq), row(nq), row(512), row(512), row(256), row(256), row(512), row(512), row(512),
                   row(N_BRANCHES * D_MODEL)],
        out_shape=[out(nq), out(nq), out(512), out(512), out(256), out(256), out(512), out(512), out(512),
                   out(N_BRANCHES * D_MODEL, F32)],
        compiler_params=_params(("parallel",)),
        name="prep",
    )(x, g_pre, ca, sa, cb, sb, w1, b_gate, g_q, g_kv, wq, wkv)


def _mla_kernel(q_ref, k_ref, v_ref, o_ref, m_ref, l_ref, acc_ref):
    qi = pl.program_id(2)
    tq, tk = MLA_TQ, MLA_TK
    m_ref[...] = jnp.full(m_ref.shape, NEG_BIG, F32)
    l_ref[...] = jnp.zeros(l_ref.shape, F32)
    acc_ref[...] = jnp.zeros(acc_ref.shape, F32)

    def chunk(j, masked):
        start = pl.multiple_of(j * tk, tk)
        k = k_ref[pl.ds(start, tk), :]
        v = v_ref[pl.ds(start, tk), :]
        if masked:
            row = lax.broadcasted_iota(jnp.int32, (tq, tk), 0)
            col = lax.broadcasted_iota(jnp.int32, (tq, tk), 1)
            valid = col <= row
        for h in range(2):
            lanes = slice(h * MLA_HEAD_PAD, (h + 1) * MLA_HEAD_PAD)
            s = _dot_nt(q_ref[:, lanes], k[:, lanes])
            if masked:
                s = jnp.where(valid, s, NEG_BIG)
            m_old = m_ref[h]
            m_new = jnp.maximum(m_old, jnp.max(s, axis=-1, keepdims=True))
            alpha = jnp.exp(m_old - m_new)
            p = jnp.exp(s - m_new)
            l_ref[h] = alpha * l_ref[h] + jnp.sum(p, axis=-1, keepdims=True)
            acc_ref[h] = alpha * acc_ref[h] + _dot(p.astype(BF16), v)
            m_ref[h] = m_new

    def body(j, carry):
        chunk(j, False)
        return carry

    lax.fori_loop(0, qi, body, 0)
    chunk(qi, True)

    lane = lax.broadcasted_iota(jnp.int32, (tq, LANES), 1)
    o0 = acc_ref[0] / l_ref[0]
    o1 = acc_ref[1] / l_ref[1]
    o_ref[...] = jnp.where(lane < MLA_V, o0, o1).astype(o_ref.dtype)


def _mla_attention(qm, km, vm, batch, seq):
    t = qm.shape[0]
    tq = MLA_TQ
    nq = seq // tq
    return pl.pallas_call(
        _mla_kernel,
        grid=(batch, HEAD_PAIRS, nq),
        in_specs=[pl.BlockSpec((tq, 2 * MLA_HEAD_PAD), lambda b, p, i: (b * nq + i, p)),
                  pl.BlockSpec((seq, 2 * MLA_HEAD_PAD), lambda b, p, i: (b, p)),
                  pl.BlockSpec((seq, LANES), lambda b, p, i: (b, p))],
        out_specs=pl.BlockSpec((tq, LANES), lambda b, p, i: (b * nq + i, p)),
        out_shape=jax.ShapeDtypeStruct((t, HEAD_PAIRS * LANES), BF16),
        scratch_shapes=[pltpu.VMEM((2, tq, 1), F32), pltpu.VMEM((2, tq, 1), F32),
                        pltpu.VMEM((2, tq, LANES), F32)],
        compiler_params=_params(("parallel", "parallel", "parallel")),
        name="mla_attention",
    )(qm, km, vm)


def _sb_kernel(q_ref, k_ref, v_ref, tri_ref, o_ref, carry_ref, acc_ref):
    qi = pl.program_id(2)
    tq, tk = SB_TQ, SB_TK
    per_tile = tq // tk
    carry_ref[...] = jnp.zeros(carry_ref.shape, F32)
    acc_ref[...] = jnp.zeros(acc_ref.shape, F32)
    q2 = q_ref[...]
    lane = lax.broadcasted_iota(jnp.int32, q2.shape, 1)
    zero = jnp.zeros_like(q2)
    qh = [jnp.where(lane < SB_HEAD_DIM, q2, zero), jnp.where(lane < SB_HEAD_DIM, zero, q2)]
    tri = tri_ref[...]

    def chunk(j, masked):
        start = pl.multiple_of(j * tk, tk)
        k = k_ref[pl.ds(start, tk), :]
        v = v_ref[pl.ds(start, tk), :]
        if masked:
            row = lax.broadcasted_iota(jnp.int32, (tq, tk), 0) + qi * tq
            col = lax.broadcasted_iota(jnp.int32, (tq, tk), 1) + j * tk
            valid = col < row
        for h in range(2):
            z = _dot_nt(qh[h], k)
            sp = jnp.maximum(z, 0.0) + jnp.log1p(jnp.exp(-jnp.abs(z)))
            if masked:
                sp = jnp.where(valid, sp, 0.0)
            hi = sp.astype(BF16)
            lo = (sp - hi.astype(F32)).astype(BF16)
            incl = _dot(hi, tri) + _dot(lo, tri)
            a = jnp.exp(z - incl - carry_ref[h])
            if masked:
                a = jnp.where(valid, a, 0.0)
            acc_ref[h] += _dot(a.astype(BF16), v)
            carry_ref[h] += incl[:, 0:1]

    for d in range(per_tile):
        chunk(qi * per_tile + (per_tile - 1 - d), True)

    n_full = qi * per_tile

    def body(i, c):
        chunk(n_full - 1 - i, False)
        return c

    lax.fori_loop(0, n_full, body, 0)

    out_lane = lax.broadcasted_iota(jnp.int32, (tq, LANES), 1)
    o_ref[...] = jnp.where(out_lane < SB_HEAD_DIM, acc_ref[0], acc_ref[1]).astype(o_ref.dtype)


def _sb_attention(qb, kb, vb, tri, batch, seq):
    t = qb.shape[0]
    tq = SB_TQ
    nq = seq // tq
    return pl.pallas_call(
        _sb_kernel,
        grid=(batch, HEAD_PAIRS, nq),
        in_specs=[pl.BlockSpec((tq, LANES), lambda b, p, i: (b * nq + i, p)),
                  pl.BlockSpec((seq, LANES), lambda b, p, i: (b, p)),
                  pl.BlockSpec((seq, LANES), lambda b, p, i: (b, p)),
                  _const_spec(tri.shape)],
        out_specs=pl.BlockSpec((tq, LANES), lambda b, p, i: (b * nq + i, p)),
        out_shape=jax.ShapeDtypeStruct((t, HEAD_PAIRS * LANES), BF16),
        scratch_shapes=[pltpu.VMEM((2, tq, 1), F32), pltpu.VMEM((2, tq, LANES), F32)],
        compiler_params=_params(("parallel", "parallel", "parallel")),
        name="sb_attention",
    )(qb, kb, vb, tri)


def _swa_kernel(sink_ref, q_ref, k_ref, v_ref, kp_ref, vp_ref, o_ref, *, blocks_per_seq):
    i = pl.program_id(0)
    w = SWA_WINDOW
    not_first = (i % blocks_per_seq) != 0
    row = lax.broadcasted_iota(jnp.int32, (w, w), 0)
    col = lax.broadcasted_iota(jnp.int32, (w, w), 1)
    lane = lax.broadcasted_iota(jnp.int32, (w, LANES), 1)
    upper = lane >= SWA_HEAD_DIM
    valid_cur = col <= row
    valid_prev_inner = col > row
    valid_prev_first = valid_prev_inner & not_first
    for r in range(SWA_TQ // w):
        rows = slice(r * w, (r + 1) * w)
        prev_rows = slice((r - 1) * w, r * w)
        valid_prev = valid_prev_first if r == 0 else valid_prev_inner
        for p in range(HEAD_PAIRS):
            g = p // (HEAD_PAIRS // SWA_KV_HEADS)
            gl = slice(g * LANES, (g + 1) * LANES)
            q2 = q_ref[rows, p * LANES:(p + 1) * LANES]
            kc, vc = k_ref[rows, gl], v_ref[rows, gl]
            if r == 0:
                kp, vp = kp_ref[:, gl], vp_ref[:, gl]
            else:
                kp, vp = k_ref[prev_rows, gl], v_ref[prev_rows, gl]
            outs = []
            for hh in range(2):
                sink = sink_ref[2 * p + hh]
                zero = jnp.zeros_like(q2)
                qh = jnp.where(upper, q2, zero) if hh else jnp.where(upper, zero, q2)
                s_c = jnp.where(valid_cur, _dot_nt(qh, kc), NEG_BIG)
                s_p = jnp.where(valid_prev, _dot_nt(qh, kp), NEG_BIG)
                m = jnp.maximum(jnp.maximum(jnp.max(s_c, axis=-1, keepdims=True),
                                            jnp.max(s_p, axis=-1, keepdims=True)), sink)
                p_c = jnp.exp(s_c - m)
                p_p = jnp.exp(s_p - m)
                den = (jnp.sum(p_c, axis=-1, keepdims=True) + jnp.sum(p_p, axis=-1, keepdims=True)
                       + jnp.exp(sink - m))
                outs.append((_dot(p_c.astype(BF16), vc) + _dot(p_p.astype(BF16), vp)) / den)
            o_ref[rows, p * LANES:(p + 1) * LANES] = jnp.where(upper, outs[1], outs[0]).astype(o_ref.dtype)


def _swa_attention(sinks, qs, ks, vs, seq):
    t = qs.shape[0]
    tq = SWA_TQ
    per_tile = tq // SWA_WINDOW
    cur = lambda n: pl.BlockSpec((tq, n), lambda i: (i, 0))
    prev = lambda n: pl.BlockSpec((SWA_WINDOW, n), lambda i: (jnp.maximum(i * per_tile - 1, 0), 0))
    return pl.pallas_call(
        functools.partial(_swa_kernel, blocks_per_seq=seq // tq),
        grid=(t // tq,),
        in_specs=[pl.BlockSpec(memory_space=pltpu.SMEM), cur(512), cur(256), cur(256), prev(256), prev(256)],
        out_specs=cur(512),
        out_shape=jax.ShapeDtypeStruct((t, 512), BF16),
        compiler_params=_params(("parallel",)),
        name="swa_attention",
    )(sinks, qs, ks, vs, ks, vs)


def _post_kernel(x_ref, oa_ref, ob_ref, oc_ref, gate_ref, woa_ref, wob_ref, woc_ref, wout_ref,
                 gpost_ref, gpre_ref, wup_ref, wdown_ref, gmlp_ref, out_ref):
    d = D_MODEL
    mixed = (gate_ref[:, 0:d] * _dot(oa_ref[...], woa_ref[...])
             + gate_ref[:, d:2 * d] * _dot(ob_ref[...], wob_ref[...])
             + gate_ref[:, 2 * d:3 * d] * _dot(oc_ref[...], woc_ref[...]))
    x1 = x_ref[...] + _rms(_dot(mixed.astype(BF16), wout_ref[...]), gpost_ref[...])
    h = _rms(x1, gpre_ref[...]).astype(BF16)
    u = jnp.square(jnp.maximum(_dot(h, wup_ref[...]), 0.0)).astype(BF16)
    out_ref[...] = x1 + _rms(_dot(u, wdown_ref[...]), gmlp_ref[...])


def _post(x, oa, ob, oc, gates, woa, wob, woc, wout, g_post, g_pre, wup, wdown, g_mlp):
    t = x.shape[0]
    tm = POST_TM
    row = lambda n: pl.BlockSpec((tm, n), lambda i: (i, 0))
    consts = [woa, wob, woc, wout, g_post, g_pre, wup, wdown, g_mlp]
    return pl.pallas_call(
        _post_kernel,
        grid=(t // tm,),
        in_specs=[row(D_MODEL), row(512), row(512), row(512), row(N_BRANCHES * D_MODEL)]
                 + [_const_spec(c.shape) for c in consts],
        out_specs=row(D_MODEL),
        out_shape=jax.ShapeDtypeStruct((t, D_MODEL), F32),
        compiler_params=_params(("parallel",)),
        name="post",
    )(x, oa, ob, oc, gates, *consts)


def _rot_cols(w, heads, dim):
    lead = w.shape[:-1]
    w = w.reshape(*lead, heads, dim)
    return jnp.concatenate([-w[..., dim // 2:], w[..., :dim // 2]], axis=-1).reshape(*lead, heads * dim)


def _dup_kv(w):
    lead = w.shape[:-1]
    w = w.reshape(*lead, SWA_KV_HEADS, 1, SWA_HEAD_DIM)
    return jnp.broadcast_to(w, (*lead, SWA_KV_HEADS, 2, SWA_HEAD_DIM)).reshape(*lead, 2 * LANES)


def _layout_weights(w_in, w_uq, w_ukv):
    dep = w_in.shape[0]
    sizes = (MLA_Q_LORA, MLA_KV_LORA, MLA_ROPE, 512, 128, 128, 512, 512, 512, N_BRANCHES * D_MODEL)
    pts = []
    acc = 0
    for s in sizes[:-1]:
        acc += s
        pts.append(acc)
    c_q, c_kv, k_rope, q_s, k_s, v_s, q_b, k_b, v_b, gates = jnp.split(w_in, pts, axis=-1)

    def kpe_pad(w):
        z = jnp.zeros((dep, D_MODEL, MLA_NOPE), F32)
        return jnp.concatenate([z, w, jnp.zeros((dep, D_MODEL, 32), F32)], axis=-1)

    swa_scale = SWA_HEAD_DIM ** -0.5
    sb_scale = SB_HEAD_DIM ** -0.5
    w1 = jnp.concatenate([
        c_q, c_kv, kpe_pad(k_rope), kpe_pad(_rot_cols(k_rope, 1, MLA_ROPE)),
        q_s * swa_scale, _rot_cols(q_s, SWA_HEADS, SWA_HEAD_DIM) * swa_scale,
        _dup_kv(k_s), _dup_kv(_rot_cols(k_s, SWA_KV_HEADS, SWA_HEAD_DIM)), _dup_kv(v_s),
        q_b * sb_scale, k_b, v_b, gates], axis=-1).astype(BF16)
    assert w1.shape[-1] == W1_COLS

    uq = w_uq.reshape(dep, MLA_Q_LORA, MLA_HEADS, MLA_NOPE + MLA_ROPE)
    nope, pe = uq[..., :MLA_NOPE], uq[..., MLA_NOPE:]
    pe_rot = jnp.concatenate([-pe[..., MLA_ROPE // 2:], pe[..., :MLA_ROPE // 2]], axis=-1)
    zq = jnp.zeros((dep, MLA_Q_LORA, MLA_HEADS, 32), F32)
    wq_main = jnp.concatenate([nope, pe, zq], axis=-1).reshape(dep, MLA_Q_LORA, -1)
    wq_rot = jnp.concatenate([jnp.zeros_like(nope), pe_rot, zq], axis=-1).reshape(dep, MLA_Q_LORA, -1)
    wq = jnp.concatenate([wq_main, wq_rot], axis=-1).astype(BF16)

    ukv = w_ukv.reshape(dep, MLA_KV_LORA, MLA_HEADS, MLA_NOPE + MLA_V)
    k_nope, v = ukv[..., :MLA_NOPE], ukv[..., MLA_NOPE:]
    wk = jnp.concatenate([k_nope, jnp.zeros_like(k_nope)], axis=-1).reshape(dep, MLA_KV_LORA, -1)
    wkv = jnp.concatenate([wk, v.reshape(dep, MLA_KV_LORA, -1)], axis=-1).astype(BF16)
    return w1, wq, wkv


def kernel(x, positions, g_mix_pre, w_in, b_gate, g_q_lat, g_kv_lat, w_uq, w_ukv, swa_sinks, w_o_mla, w_o_swa, w_o_sb, w_out, g_mix_post, g_mlp_pre, w_up, w_down, g_mlp_post):
    batch, seq, d = x.shape
    t = batch * seq
    tables = _rope_tables(positions)
    w1, wq, wkv = _layout_weights(w_in, w_uq, w_ukv)
    woa, wob, woc, wout = (w.astype(BF16) for w in (w_o_mla, w_o_swa, w_o_sb, w_out))
    wup, wdown = w_up.astype(BF16), w_down.astype(BF16)
    tri = (lax.broadcasted_iota(jnp.int32, (SB_TK, SB_TK), 0)
           >= lax.broadcasted_iota(jnp.int32, (SB_TK, SB_TK), 1)).astype(BF16)
    vec = lambda g, l: g[l].reshape(1, -1)

    xt = x.reshape(t, d)
    for l in range(DEPTH):
        qm, km, vm, qs, ks, vs, qb, kb, vb, gates = _prep(
            xt, tables, vec(g_mix_pre, l), w1[l], vec(b_gate, l), vec(g_q_lat, l), vec(g_kv_lat, l),
            wq[l], wkv[l])
        oa = _mla_attention(qm, km, vm, batch, seq)
        ob = _swa_attention(swa_sinks[l], qs, ks, vs, seq)
        oc = _sb_attention(qb, kb, vb, tri, batch, seq)
        xt = _post(xt, oa, ob, oc, gates, woa[l], wob[l], woc[l], wout[l], vec(g_mix_post, l),
                   vec(g_mlp_pre, l), wup[l], wdown[l], vec(g_mlp_post, l))
    return xt.reshape(batch, seq, d)
```

```python
import functools

import jax
import jax.numpy as jnp
from jax import lax
from jax.experimental import pallas as pl
from jax.experimental.pallas import tpu as pltpu

F32 = jnp.float32
BF16 = jnp.bfloat16

D_MODEL = 1024
DEPTH = 4
MLA_HEADS = 8
MLA_Q_LORA = 256
MLA_KV_LORA = 128
MLA_NOPE = 64
MLA_ROPE = 32
MLA_V = 64
SWA_HEADS = 8
SWA_KV_HEADS = 2
SWA_HEAD_DIM = 64
SWA_WINDOW = 128
SB_HEADS = 8
SB_HEAD_DIM = 64
D_FF = 4 * D_MODEL
ROPE_THETA = 10000.0
EPS = 1e-6
N_BRANCHES = 3

LANES = 128
HEAD_PAIRS = 4
MLA_HEAD_PAD = 128
NEG_BIG = -1e30
LOG2E = 1.4426950408889634
VMEM_LIMIT = 56 * 1024 * 1024

OFF_CQ = 0
OFF_CKV = OFF_CQ + MLA_Q_LORA
OFF_KPE = OFF_CKV + MLA_KV_LORA
OFF_KPE_ROT = OFF_KPE + LANES
OFF_QS = OFF_KPE_ROT + LANES
OFF_QS_ROT = OFF_QS + 512
OFF_KS = OFF_QS_ROT + 512
OFF_KS_ROT = OFF_KS + 256
OFF_VS = OFF_KS_ROT + 256
OFF_QB = OFF_VS + 256
OFF_KB = OFF_QB + 512
OFF_VB = OFF_KB + 512
OFF_GATE = OFF_VB + 512
W1_COLS = OFF_GATE + N_BRANCHES * D_MODEL

PREP_TM = 256
POST_TM = 256
MLA_TQ = 512
MLA_TK = 512
SB_TQ = 512
SB_TK = 512
SB_BLK = 256
SWA_TQ = 256
ROPE_TM = 2048


def _rms(x, g):
    return x * lax.rsqrt(jnp.mean(x * x, axis=-1, keepdims=True) + EPS) * g


def _dot(a, b):
    return jnp.dot(a, b, preferred_element_type=F32)


def _dot_nt(a, b):
    return lax.dot_general(a, b, (((1,), (1,)), ((), ())), preferred_element_type=F32)


def _const_spec(shape):
    return pl.BlockSpec(shape, lambda *_: (0,) * len(shape), pipeline_mode=pl.Buffered(1))


def _params(sem):
    return pltpu.CompilerParams(dimension_semantics=sem, vmem_limit_bytes=VMEM_LIMIT)


def _rope_table_kernel(pos_ref, inva_ref, maska_ref, invb_ref, ca_ref, sa_ref, cb_ref, sb_ref):
    pos = pos_ref[...].astype(F32)
    ang_a = pos * inva_ref[...]
    ca_ref[...] = jnp.cos(ang_a) * maska_ref[...]
    sa_ref[...] = jnp.sin(ang_a)
    ang_b = pos * invb_ref[...]
    cb_ref[...] = jnp.cos(ang_b)
    sb_ref[...] = jnp.sin(ang_b)


def _rope_tables(positions):
    t = positions.size
    pos = positions.reshape(t, 1)
    inv_a16 = 1.0 / (ROPE_THETA ** (jnp.arange(0, MLA_ROPE, 2, dtype=F32) / MLA_ROPE))
    inv_b32 = 1.0 / (ROPE_THETA ** (jnp.arange(0, SWA_HEAD_DIM, 2, dtype=F32) / SWA_HEAD_DIM))
    zeros = jnp.zeros
    inv_a = jnp.concatenate([zeros((MLA_NOPE,), F32), inv_a16, inv_a16, zeros((32,), F32)]).reshape(1, LANES)
    mask_a = jnp.concatenate([jnp.ones((MLA_NOPE + MLA_ROPE,), F32), zeros((32,), F32)]).reshape(1, LANES)
    inv_b = jnp.tile(inv_b32, 4).reshape(1, LANES)
    row = pl.BlockSpec((ROPE_TM, LANES), lambda i: (i, 0))
    vec = pl.BlockSpec((1, LANES), lambda i: (0, 0))
    out = jax.ShapeDtypeStruct((t, LANES), F32)
    return pl.pallas_call(
        _rope_table_kernel,
        grid=(t // ROPE_TM,),
        in_specs=[pl.BlockSpec((ROPE_TM, 1), lambda i: (i, 0)), vec, vec, vec],
        out_specs=[row, row, row, row],
        out_shape=[out, out, out, out],
        compiler_params=_params(("parallel",)),
        name="rope_tables",
    )(pos, inv_a, mask_a, inv_b)


def _prep_kernel(x_ref, g_ref, ca_ref, sa_ref, cb_ref, sb_ref, w1_ref, bg_ref, gq_ref, gkv_ref,
                 wq_ref, wkv_ref,
                 qm_ref, km_ref, vm_ref, qs_ref, ks_ref, vs_ref, qb_ref, kb_ref, vb_ref, gate_ref):
    h = _rms(x_ref[...], g_ref[...]).astype(BF16)

    def mm(lo, n):
        return _dot(h, w1_ref[:, lo:lo + n])

    ca, sa, cb, sb = ca_ref[...], sa_ref[...], cb_ref[...], sb_ref[...]

    cb4 = jnp.concatenate([cb] * 4, axis=1)
    sb4 = jnp.concatenate([sb] * 4, axis=1)
    qs_ref[...] = (mm(OFF_QS, 512) * cb4 + mm(OFF_QS_ROT, 512) * sb4).astype(BF16)
    ks_ref[...] = (mm(OFF_KS, 256) * cb4[:, :256] + mm(OFF_KS_ROT, 256) * sb4[:, :256]).astype(BF16)
    vs_ref[...] = mm(OFF_VS, 256).astype(BF16)

    qb_ref[...] = (mm(OFF_QB, 512) * LOG2E).astype(BF16)
    kb_ref[...] = mm(OFF_KB, 512).astype(BF16)
    vb_ref[...] = mm(OFF_VB, 512).astype(BF16)

    gate_ref[...] = jax.nn.sigmoid(mm(OFF_GATE, N_BRANCHES * D_MODEL) + bg_ref[...])

    scale = (MLA_NOPE + MLA_ROPE) ** -0.5 * LOG2E
    cqn = _rms(mm(OFF_CQ, MLA_Q_LORA), gq_ref[...]).astype(BF16)
    ca8 = jnp.concatenate([ca * scale] * MLA_HEADS, axis=1)
    sa8 = jnp.concatenate([sa * scale] * MLA_HEADS, axis=1)
    nq = MLA_HEADS * MLA_HEAD_PAD
    qm_ref[...] = (_dot(cqn, wq_ref[:, :nq]) * ca8 + _dot(cqn, wq_ref[:, nq:]) * sa8).astype(BF16)
    kpe = mm(OFF_KPE, LANES) * ca + mm(OFF_KPE_ROT, LANES) * sa
    ckvn = _rms(mm(OFF_CKV, MLA_KV_LORA), gkv_ref[...]).astype(BF16)
    kpe8 = jnp.concatenate([kpe] * MLA_HEADS, axis=1)
    km_ref[...] = (_dot(ckvn, wkv_ref[:, :nq]) + kpe8).astype(BF16)
    lane = lax.broadcasted_iota(jnp.int32, (1, nq), 1) % (2 * LANES)
    ones = ((lane >= MLA_V) & (lane < 2 * LANES - MLA_V)).astype(F32)
    vm_ref[...] = (_dot(ckvn, wkv_ref[:, nq:]) + ones).astype(BF16)


def _prep(x, tables, g_pre, w1, b_gate, g_q, g_kv, wq, wkv):
    t = x.shape[0]
    tm = PREP_TM
    ca, sa, cb, sb = tables

    def row(n):
        return pl.BlockSpec((tm, n), lambda i: (i, 0))

    def out(n, dt=BF16):
        return jax.ShapeDtypeStruct((t, n), dt)

    nq = MLA_HEADS * MLA_HEAD_PAD
    return pl.pallas_call(
        _prep_kernel,
        grid=(t // tm,),
        in_specs=[row(D_MODEL), _const_spec((1, D_MODEL)), row(LANES), row(LANES), row(LANES), row(LANES),
                  _const_spec(w1.shape), _const_spec(b_gate.shape), _const_spec(g_q.shape),
                  _const_spec(g_kv.shape), _const_spec(wq.shape), _const_spec(wkv.shape)],
        out_specs=[row(nq), row(nq), row(nq), row(512), row(256), row(256), row(512), row(512), row(512),
                   row(N_BRANCHES * D_MODEL)],
        out_shape=[out(nq), out(nq), out(nq), out(512), out(256), out(256), out(512), out(512), out(512),
                   out(N_BRANCHES * D_MODEL, F32)],
        compiler_params=_params(("parallel",)),
        name="prep",
    )(x, g_pre, ca, sa, cb, sb, w1, b_gate, g_q, g_kv, wq, wkv)


def _mla_kernel(q_ref, k_ref, v_ref, o_ref, m_ref, acc_ref):
    qi = pl.program_id(2)
    tq, tk = MLA_TQ, MLA_TK
    m_ref[...] = jnp.full(m_ref.shape, NEG_BIG, F32)
    acc_ref[...] = jnp.zeros(acc_ref.shape, F32)

    def chunk(j, masked):
        start = pl.multiple_of(j * tk, tk)
        k = k_ref[pl.ds(start, tk), :]
        v = v_ref[pl.ds(start, tk), :]
        if masked:
            row = lax.broadcasted_iota(jnp.int32, (tq, tk), 0)
            col = lax.broadcasted_iota(jnp.int32, (tq, tk), 1)
            valid = col <= row
        for h in range(2):
            lanes = slice(h * MLA_HEAD_PAD, (h + 1) * MLA_HEAD_PAD)
            s = _dot_nt(q_ref[:, lanes], k[:, lanes])
            if masked:
                s = jnp.where(valid, s, NEG_BIG)
            m_old = m_ref[h]
            m_new = jnp.maximum(m_old, jnp.max(s, axis=-1, keepdims=True))
            alpha = jnp.exp2(m_old - m_new)
            p = jnp.exp2(s - jnp.concatenate([m_new] * (tk // LANES), axis=1))
            acc_ref[h] = alpha * acc_ref[h] + _dot(p.astype(BF16), v[:, lanes])
            m_ref[h] = m_new

    def body(j, carry):
        chunk(j, False)
        return carry

    lax.fori_loop(0, qi, body, 0)
    chunk(qi, True)

    first_half = lax.broadcasted_iota(jnp.int32, (tq, LANES), 1) < MLA_V
    a0, a1 = acc_ref[0], acc_ref[1]
    num = jnp.where(first_half, a0, a1)
    den = jnp.where(first_half, pltpu.roll(a0, MLA_V, axis=1), pltpu.roll(a1, MLA_V, axis=1))
    o_ref[...] = (num / den).astype(o_ref.dtype)


def _mla_attention(qm, km, vm, batch, seq):
    t = qm.shape[0]
    tq = MLA_TQ
    nq = seq // tq
    return pl.pallas_call(
        _mla_kernel,
        grid=(batch, HEAD_PAIRS, nq),
        in_specs=[pl.BlockSpec((tq, 2 * MLA_HEAD_PAD), lambda b, p, i: (b * nq + i, p)),
                  pl.BlockSpec((seq, 2 * MLA_HEAD_PAD), lambda b, p, i: (b, p)),
                  pl.BlockSpec((seq, 2 * LANES), lambda b, p, i: (b, p))],
        out_specs=pl.BlockSpec((tq, LANES), lambda b, p, i: (b * nq + i, p)),
        out_shape=jax.ShapeDtypeStruct((t, HEAD_PAIRS * LANES), BF16),
        scratch_shapes=[pltpu.VMEM((2, tq, LANES), F32), pltpu.VMEM((2, tq, LANES), F32)],
        compiler_params=_params(("parallel", "parallel", "parallel")),
        name="mla_attention",
    )(qm, km, vm)


def _sb_kernel(q_ref, k_ref, v_ref, tri_ref, o_ref, qh_ref, carry_ref, acc_ref):
    qi = pl.program_id(2)
    tq, tk, blk = SB_TQ, SB_TK, SB_BLK
    carry_ref[...] = jnp.zeros(carry_ref.shape, F32)
    acc_ref[...] = jnp.zeros(acc_ref.shape, F32)
    q2 = q_ref[...]
    first_half = lax.broadcasted_iota(jnp.int32, q2.shape, 1) < SB_HEAD_DIM
    zero = jnp.zeros_like(q2)
    qh_ref[0] = jnp.where(first_half, q2, zero)
    qh_ref[1] = jnp.where(first_half, zero, q2)

    def chunk(j, masked):
        start = pl.multiple_of(j * tk, tk)
        k = k_ref[pl.ds(start, tk), :]
        v = v_ref[pl.ds(start, tk), :]
        if masked:
            row = lax.broadcasted_iota(jnp.int32, (tq, tk), 0)
            col = lax.broadcasted_iota(jnp.int32, (tq, tk), 1)
            valid = col < row
        for h in range(2):
            z = _dot_nt(qh_ref[h], k)
            sp = jnp.maximum(z, 0.0) + jnp.log2(1.0 + jnp.exp2(-jnp.abs(z)))
            if masked:
                sp = jnp.where(valid, sp, 0.0)
            hi = sp.astype(BF16)
            lo = (sp - hi.astype(F32)).astype(BF16)
            c = carry_ref[h]
            expo = [None] * (tk // blk)
            for b in reversed(range(tk // blk)):
                cols = slice(b * blk, (b + 1) * blk)
                incl = _dot(jnp.concatenate([hi[:, cols], lo[:, cols]], axis=1), tri_ref[...])
                expo[b] = z[:, cols] - incl - jnp.concatenate([c] * (blk // LANES), axis=1)
                c = c + jnp.broadcast_to(incl[:, 0:1], c.shape)
            carry_ref[h] = c
            a = jnp.exp2(jnp.concatenate(expo, axis=1))
            if masked:
                a = jnp.where(valid, a, 0.0)
            acc_ref[h] += _dot(a.astype(BF16), v)

    chunk(qi, True)

    def body(i, c):
        chunk(qi - 1 - i, False)
        return c

    lax.fori_loop(0, qi, body, 0)

    out_lane = lax.broadcasted_iota(jnp.int32, (tq, LANES), 1)
    o_ref[...] = jnp.where(out_lane < SB_HEAD_DIM, acc_ref[0], acc_ref[1]).astype(o_ref.dtype)


def _sb_attention(qb, kb, vb, tri, batch, seq):
    t = qb.shape[0]
    tq = SB_TQ
    nq = seq // tq
    return pl.pallas_call(
        _sb_kernel,
        grid=(batch, HEAD_PAIRS, nq),
        in_specs=[pl.BlockSpec((tq, LANES), lambda b, p, i: (b * nq + i, p)),
                  pl.BlockSpec((seq, LANES), lambda b, p, i: (b, p)),
                  pl.BlockSpec((seq, LANES), lambda b, p, i: (b, p)),
                  _const_spec(tri.shape)],
        out_specs=pl.BlockSpec((tq, LANES), lambda b, p, i: (b * nq + i, p)),
        out_shape=jax.ShapeDtypeStruct((t, HEAD_PAIRS * LANES), BF16),
        scratch_shapes=[pltpu.VMEM((2, tq, LANES), BF16), pltpu.VMEM((2, tq, LANES), F32),
                        pltpu.VMEM((2, tq, LANES), F32)],
        compiler_params=_params(("parallel", "parallel", "parallel")),
        name="sb_attention",
    )(qb, kb, vb, tri)


def _swa_kernel(sink_ref, q_ref, k_ref, v_ref, kp_ref, vp_ref, o_ref, *, blocks_per_seq):
    i = pl.program_id(0)
    w = SWA_WINDOW
    not_first = (i % blocks_per_seq) != 0
    row = lax.broadcasted_iota(jnp.int32, (w, w), 0)
    col = lax.broadcasted_iota(jnp.int32, (w, w), 1)
    lane = lax.broadcasted_iota(jnp.int32, (w, LANES), 1)
    upper = lane >= SWA_HEAD_DIM
    valid_cur = col <= row
    valid_prev_inner = col > row
    valid_prev_first = valid_prev_inner & not_first
    for r in range(SWA_TQ // w):
        rows = slice(r * w, (r + 1) * w)
        prev_rows = slice((r - 1) * w, r * w)
        valid_prev = valid_prev_first if r == 0 else valid_prev_inner
        for p in range(HEAD_PAIRS):
            g = p // (HEAD_PAIRS // SWA_KV_HEADS)
            gl = slice(g * LANES, (g + 1) * LANES)
            q2 = q_ref[rows, p * LANES:(p + 1) * LANES]
            kc, vc = k_ref[rows, gl], v_ref[rows, gl]
            if r == 0:
                kp, vp = kp_ref[:, gl], vp_ref[:, gl]
            else:
                kp, vp = k_ref[prev_rows, gl], v_ref[prev_rows, gl]
            outs = []
            for hh in range(2):
                sink = sink_ref[2 * p + hh]
                zero = jnp.zeros_like(q2)
                qh = jnp.where(upper, q2, zero) if hh else jnp.where(upper, zero, q2)
                s_c = jnp.where(valid_cur, _dot_nt(qh, kc), NEG_BIG)
                s_p = jnp.where(valid_prev, _dot_nt(qh, kp), NEG_BIG)
                m = jnp.maximum(jnp.maximum(jnp.max(s_c, axis=-1, keepdims=True),
                                            jnp.max(s_p, axis=-1, keepdims=True)), sink)
                p_c = jnp.exp(s_c - m)
                p_p = jnp.exp(s_p - m)
                den = (jnp.sum(p_c, axis=-1, keepdims=True) + jnp.sum(p_p, axis=-1, keepdims=True)
                       + jnp.exp(sink - m))
                outs.append((_dot(p_c.astype(BF16), vc) + _dot(p_p.astype(BF16), vp)) / den)
            o_ref[rows, p * LANES:(p + 1) * LANES] = jnp.where(upper, outs[1], outs[0]).astype(o_ref.dtype)


def _swa_attention(sinks, qs, ks, vs, seq):
    t = qs.shape[0]
    tq = SWA_TQ
    per_tile = tq // SWA_WINDOW
    cur = lambda n: pl.BlockSpec((tq, n), lambda i: (i, 0))
    prev = lambda n: pl.BlockSpec((SWA_WINDOW, n), lambda i: (jnp.maximum(i * per_tile - 1, 0), 0))
    return pl.pallas_call(
        functools.partial(_swa_kernel, blocks_per_seq=seq // tq),
        grid=(t // tq,),
        in_specs=[pl.BlockSpec(memory_space=pltpu.SMEM), cur(512), cur(256), cur(256), prev(256), prev(256)],
        out_specs=cur(512),
        out_shape=jax.ShapeDtypeStruct((t, 512), BF16),
        compiler_params=_params(("parallel",)),
        name="swa_attention",
    )(sinks, qs, ks, vs, ks, vs)


def _post_kernel(x_ref, oa_ref, ob_ref, oc_ref, gate_ref, woa_ref, wob_ref, woc_ref, wout_ref,
                 gpost_ref, gpre_ref, wup_ref, wdown_ref, gmlp_ref, out_ref):
    d = D_MODEL
    mixed = (gate_ref[:, 0:d] * _dot(oa_ref[...], woa_ref[...])
             + gate_ref[:, d:2 * d] * _dot(ob_ref[...], wob_ref[...])
             + gate_ref[:, 2 * d:3 * d] * _dot(oc_ref[...], woc_ref[...]))
    x1 = x_ref[...] + _rms(_dot(mixed.astype(BF16), wout_ref[...]), gpost_ref[...])
    h = _rms(x1, gpre_ref[...]).astype(BF16)
    u = jnp.square(jnp.maximum(_dot(h, wup_ref[...]), 0.0)).astype(BF16)
    out_ref[...] = x1 + _rms(_dot(u, wdown_ref[...]), gmlp_ref[...])


def _post(x, oa, ob, oc, gates, woa, wob, woc, wout, g_post, g_pre, wup, wdown, g_mlp):
    t = x.shape[0]
    tm = POST_TM
    row = lambda n: pl.BlockSpec((tm, n), lambda i: (i, 0))
    consts = [woa, wob, woc, wout, g_post, g_pre, wup, wdown, g_mlp]
    return pl.pallas_call(
        _post_kernel,
        grid=(t // tm,),
        in_specs=[row(D_MODEL), row(512), row(512), row(512), row(N_BRANCHES * D_MODEL)]
                 + [_const_spec(c.shape) for c in consts],
        out_specs=row(D_MODEL),
        out_shape=jax.ShapeDtypeStruct((t, D_MODEL), F32),
        compiler_params=_params(("parallel",)),
        name="post",
    )(x, oa, ob, oc, gates, *consts)


def _rot_cols(w, heads, dim):
    lead = w.shape[:-1]
    w = w.reshape(*lead, heads, dim)
    return jnp.concatenate([-w[..., dim // 2:], w[..., :dim // 2]], axis=-1).reshape(*lead, heads * dim)


def _dup_kv(w):
    lead = w.shape[:-1]
    w = w.reshape(*lead, SWA_KV_HEADS, 1, SWA_HEAD_DIM)
    return jnp.broadcast_to(w, (*lead, SWA_KV_HEADS, 2, SWA_HEAD_DIM)).reshape(*lead, 2 * LANES)


def _layout_weights(w_in, w_uq, w_ukv):
    dep = w_in.shape[0]
    sizes = (MLA_Q_LORA, MLA_KV_LORA, MLA_ROPE, 512, 128, 128, 512, 512, 512, N_BRANCHES * D_MODEL)
    pts = []
    acc = 0
    for s in sizes[:-1]:
        acc += s
        pts.append(acc)
    c_q, c_kv, k_rope, q_s, k_s, v_s, q_b, k_b, v_b, gates = jnp.split(w_in, pts, axis=-1)

    def kpe_pad(w):
        z = jnp.zeros((dep, D_MODEL, MLA_NOPE), F32)
        return jnp.concatenate([z, w, jnp.zeros((dep, D_MODEL, 32), F32)], axis=-1)

    swa_scale = SWA_HEAD_DIM ** -0.5
    sb_scale = SB_HEAD_DIM ** -0.5
    w1 = jnp.concatenate([
        c_q, c_kv, kpe_pad(k_rope), kpe_pad(_rot_cols(k_rope, 1, MLA_ROPE)),
        q_s * swa_scale, _rot_cols(q_s, SWA_HEADS, SWA_HEAD_DIM) * swa_scale,
        _dup_kv(k_s), _dup_kv(_rot_cols(k_s, SWA_KV_HEADS, SWA_HEAD_DIM)), _dup_kv(v_s),
        q_b * sb_scale, k_b, v_b, gates], axis=-1).astype(BF16)
    assert w1.shape[-1] == W1_COLS

    uq = w_uq.reshape(dep, MLA_Q_LORA, MLA_HEADS, MLA_NOPE + MLA_ROPE)
    nope, pe = uq[..., :MLA_NOPE], uq[..., MLA_NOPE:]
    pe_rot = jnp.concatenate([-pe[..., MLA_ROPE // 2:], pe[..., :MLA_ROPE // 2]], axis=-1)
    zq = jnp.zeros((dep, MLA_Q_LORA, MLA_HEADS, 32), F32)
    wq_main = jnp.concatenate([nope, pe, zq], axis=-1).reshape(dep, MLA_Q_LORA, -1)
    wq_rot = jnp.concatenate([jnp.zeros_like(nope), pe_rot, zq], axis=-1).reshape(dep, MLA_Q_LORA, -1)
    wq = jnp.concatenate([wq_main, wq_rot], axis=-1).astype(BF16)

    ukv = w_ukv.reshape(dep, MLA_KV_LORA, MLA_HEADS, MLA_NOPE + MLA_V)
    k_nope, v = ukv[..., :MLA_NOPE], ukv[..., MLA_NOPE:]
    wk = jnp.concatenate([k_nope, jnp.zeros_like(k_nope)], axis=-1).reshape(dep, MLA_KV_LORA, -1)
    vp = v.reshape(dep, MLA_KV_LORA, HEAD_PAIRS, 2, MLA_V)
    zv = jnp.zeros_like(vp[..., 0, :])
    wv = jnp.concatenate([vp[..., 0, :], zv, zv, vp[..., 1, :]], axis=-1).reshape(dep, MLA_KV_LORA, -1)
    wkv = jnp.concatenate([wk, wv], axis=-1).astype(BF16)
    return w1, wq, wkv


def kernel(x, positions, g_mix_pre, w_in, b_gate, g_q_lat, g_kv_lat, w_uq, w_ukv, swa_sinks, w_o_mla, w_o_swa, w_o_sb, w_out, g_mix_post, g_mlp_pre, w_up, w_down, g_mlp_post):
    batch, seq, d = x.shape
    t = batch * seq
    tables = _rope_tables(positions)
    w1, wq, wkv = _layout_weights(w_in, w_uq, w_ukv)
    woa, wob, woc, wout = (w.astype(BF16) for w in (w_o_mla, w_o_swa, w_o_sb, w_out))
    wup, wdown = w_up.astype(BF16), w_down.astype(BF16)
    tri = (lax.broadcasted_iota(jnp.int32, (SB_BLK, SB_BLK), 0)
           >= lax.broadcasted_iota(jnp.int32, (SB_BLK, SB_BLK), 1)).astype(BF16)
    tri = jnp.concatenate([tri, tri], axis=0)
    vec = lambda g, l: g[l].reshape(1, -1)

    xt = x.reshape(t, d)
    for l in range(DEPTH):
        qm, km, vm, qs, ks, vs, qb, kb, vb, gates = _prep(
            xt, tables, vec(g_mix_pre, l), w1[l], vec(b_gate, l), vec(g_q_lat, l), vec(g_kv_lat, l),
            wq[l], wkv[l])
        oa = _mla_attention(qm, km, vm, batch, seq)
        ob = _swa_attention(swa_sinks[l], qs, ks, vs, seq)
        oc = _sb_attention(qb, kb, vb, tri, batch, seq)
        xt = _post(xt, oa, ob, oc, gates, woa[l], wob[l], woc[l], wout[l], vec(g_mix_post, l),
                   vec(g_mlp_pre, l), wup[l], wdown[l], vec(g_mlp_post, l))
    return xt.reshape(batch, seq, d)
```

```python
import functools

import jax
import jax.numpy as jnp
from jax import lax
from jax.experimental import pallas as pl
from jax.experimental.pallas import tpu as pltpu

F32 = jnp.float32
BF16 = jnp.bfloat16

D_MODEL = 1024
DEPTH = 4
MLA_HEADS = 8
MLA_Q_LORA = 256
MLA_KV_LORA = 128
MLA_NOPE = 64
MLA_ROPE = 32
MLA_V = 64
SWA_HEADS = 8
SWA_KV_HEADS = 2
SWA_HEAD_DIM = 64
SWA_WINDOW = 128
SB_HEADS = 8
SB_HEAD_DIM = 64
D_FF = 4 * D_MODEL
ROPE_THETA = 10000.0
EPS = 1e-6
N_BRANCHES = 3

LANES = 128
HEAD_PAIRS = 4
MLA_HEAD_PAD = 128
NEG_BIG = -1e30
LOG2E = 1.4426950408889634
SP_CLAMP = 126.0
VMEM_LIMIT = 56 * 1024 * 1024

OFF_CQ = 0
OFF_CKV = OFF_CQ + MLA_Q_LORA
OFF_KPE = OFF_CKV + MLA_KV_LORA
OFF_KPE_ROT = OFF_KPE + LANES
OFF_QS = OFF_KPE_ROT + LANES
OFF_QS_ROT = OFF_QS + 512
OFF_KS = OFF_QS_ROT + 512
OFF_KS_ROT = OFF_KS + 256
OFF_VS = OFF_KS_ROT + 256
OFF_QB = OFF_VS + 256
OFF_KB = OFF_QB + 512
OFF_VB = OFF_KB + 512
OFF_GATE = OFF_VB + 512
W1_COLS = OFF_GATE + N_BRANCHES * D_MODEL

PREP_TM = 256
POST_TM = 256
MLA_TQ = 512
MLA_TK = 512
MLA_ROWS = 512
SB_TQ = 512
SB_TK = 512
SB_BLK = 256
SWA_TQ = 256
ROPE_TM = 2048


def _rms(x, g):
    return x * lax.rsqrt(jnp.mean(x * x, axis=-1, keepdims=True) + EPS) * g


def _dot(a, b):
    return jnp.dot(a, b, preferred_element_type=F32)


def _dot_nt(a, b):
    return lax.dot_general(a, b, (((1,), (1,)), ((), ())), preferred_element_type=F32)


def _const_spec(shape):
    return pl.BlockSpec(shape, lambda *_: (0,) * len(shape), pipeline_mode=pl.Buffered(1))


def _params(sem):
    return pltpu.CompilerParams(dimension_semantics=sem, vmem_limit_bytes=VMEM_LIMIT)


def _rope_table_kernel(pos_ref, inva_ref, maska_ref, invb_ref, ca_ref, sa_ref, cb_ref, sb_ref):
    pos = pos_ref[...].astype(F32)
    ang_a = pos * inva_ref[...]
    ca_ref[...] = jnp.cos(ang_a) * maska_ref[...]
    sa_ref[...] = jnp.sin(ang_a)
    ang_b = pos * invb_ref[...]
    cb_ref[...] = jnp.cos(ang_b)
    sb_ref[...] = jnp.sin(ang_b)


def _rope_tables(positions):
    t = positions.size
    pos = positions.reshape(t, 1)
    inv_a16 = 1.0 / (ROPE_THETA ** (jnp.arange(0, MLA_ROPE, 2, dtype=F32) / MLA_ROPE))
    inv_b32 = 1.0 / (ROPE_THETA ** (jnp.arange(0, SWA_HEAD_DIM, 2, dtype=F32) / SWA_HEAD_DIM))
    zeros = jnp.zeros
    inv_a = jnp.concatenate([zeros((MLA_NOPE,), F32), inv_a16, inv_a16, zeros((32,), F32)]).reshape(1, LANES)
    mask_a = jnp.concatenate([jnp.ones((MLA_NOPE + MLA_ROPE,), F32), zeros((32,), F32)]).reshape(1, LANES)
    inv_b = jnp.tile(inv_b32, 4).reshape(1, LANES)
    row = pl.BlockSpec((ROPE_TM, LANES), lambda i: (i, 0))
    vec = pl.BlockSpec((1, LANES), lambda i: (0, 0))
    out = jax.ShapeDtypeStruct((t, LANES), F32)
    return pl.pallas_call(
        _rope_table_kernel,
        grid=(t // ROPE_TM,),
        in_specs=[pl.BlockSpec((ROPE_TM, 1), lambda i: (i, 0)), vec, vec, vec],
        out_specs=[row, row, row, row],
        out_shape=[out, out, out, out],
        compiler_params=_params(("parallel",)),
        name="rope_tables",
    )(pos, inv_a, mask_a, inv_b)


def _prep_kernel(x_ref, g_ref, ca_ref, sa_ref, cb_ref, sb_ref, w1_ref, bg_ref, gq_ref, gkv_ref,
                 wq_ref, wkv_ref,
                 qm_ref, km_ref, vm_ref, qs_ref, ks_ref, vs_ref, qb_ref, kb_ref, vb_ref, gate_ref):
    h = _rms(x_ref[...], g_ref[...]).astype(BF16)

    def mm(lo, n):
        return _dot(h, w1_ref[:, lo:lo + n])

    ca, sa, cb, sb = ca_ref[...], sa_ref[...], cb_ref[...], sb_ref[...]

    cb4 = jnp.concatenate([cb] * 4, axis=1)
    sb4 = jnp.concatenate([sb] * 4, axis=1)
    qs_ref[...] = (mm(OFF_QS, 512) * cb4 + mm(OFF_QS_ROT, 512) * sb4).astype(BF16)
    ks_ref[...] = (mm(OFF_KS, 256) * cb4[:, :256] + mm(OFF_KS_ROT, 256) * sb4[:, :256]).astype(BF16)
    vs_ref[...] = mm(OFF_VS, 256).astype(BF16)

    qb_ref[...] = (mm(OFF_QB, 512) * LOG2E).astype(BF16)
    kb_ref[...] = mm(OFF_KB, 512).astype(BF16)
    vb_ref[...] = mm(OFF_VB, 512).astype(BF16)

    gate_ref[...] = jax.nn.sigmoid(mm(OFF_GATE, N_BRANCHES * D_MODEL) + bg_ref[...])

    scale = (MLA_NOPE + MLA_ROPE) ** -0.5 * LOG2E
    cqn = _rms(mm(OFF_CQ, MLA_Q_LORA), gq_ref[...]).astype(BF16)
    ca8 = jnp.concatenate([ca * scale] * MLA_HEADS, axis=1)
    sa8 = jnp.concatenate([sa * scale] * MLA_HEADS, axis=1)
    nq = MLA_HEADS * MLA_HEAD_PAD
    qm_ref[...] = (_dot(cqn, wq_ref[:, :nq]) * ca8 + _dot(cqn, wq_ref[:, nq:]) * sa8).astype(BF16)
    kpe = mm(OFF_KPE, LANES) * ca + mm(OFF_KPE_ROT, LANES) * sa
    ckvn = _rms(mm(OFF_CKV, MLA_KV_LORA), gkv_ref[...]).astype(BF16)
    kpe8 = jnp.concatenate([kpe] * MLA_HEADS, axis=1)
    km_ref[...] = (_dot(ckvn, wkv_ref[:, :nq]) + kpe8).astype(BF16)
    lane = lax.broadcasted_iota(jnp.int32, (1, nq), 1) % (2 * LANES)
    ones = ((lane >= MLA_V) & (lane < 2 * LANES - MLA_V)).astype(F32)
    vm_ref[...] = (_dot(ckvn, wkv_ref[:, nq:]) + ones).astype(BF16)


def _prep(x, tables, g_pre, w1, b_gate, g_q, g_kv, wq, wkv):
    t = x.shape[0]
    tm = PREP_TM
    ca, sa, cb, sb = tables

    def row(n):
        return pl.BlockSpec((tm, n), lambda i: (i, 0))

    def out(n, dt=BF16):
        return jax.ShapeDtypeStruct((t, n), dt)

    nq = MLA_HEADS * MLA_HEAD_PAD
    return pl.pallas_call(
        _prep_kernel,
        grid=(t // tm,),
        in_specs=[row(D_MODEL), _const_spec((1, D_MODEL)), row(LANES), row(LANES), row(LANES), row(LANES),
                  _const_spec(w1.shape), _const_spec(b_gate.shape), _const_spec(g_q.shape),
                  _const_spec(g_kv.shape), _const_spec(wq.shape), _const_spec(wkv.shape)],
        out_specs=[row(nq), row(nq), row(nq), row(512), row(256), row(256), row(512), row(512), row(512),
                   row(N_BRANCHES * D_MODEL)],
        out_shape=[out(nq), out(nq), out(nq), out(512), out(256), out(256), out(512), out(512), out(512),
                   out(N_BRANCHES * D_MODEL, F32)],
        compiler_params=_params(("parallel",)),
        name="prep",
    )(x, g_pre, ca, sa, cb, sb, w1, b_gate, g_q, g_kv, wq, wkv)


def _mla_kernel(q_ref, k_ref, v_ref, o_ref, m_ref, acc_ref):
    qi = pl.program_id(2)
    tq, tk = MLA_TQ, MLA_TK
    m_ref[...] = jnp.full(m_ref.shape, NEG_BIG, F32)
    acc_ref[...] = jnp.zeros(acc_ref.shape, F32)

    def head_lanes(h):
        return slice(h * MLA_HEAD_PAD, (h + 1) * MLA_HEAD_PAD)

    def scores(j):
        start = pl.multiple_of(j * tk, tk)
        k = k_ref[pl.ds(start, tk), :]
        return tuple(_dot_nt(q_ref[:, head_lanes(h)], k[:, head_lanes(h)]) for h in range(2))

    def softmax_pv(j, s_pair, masked):
        start = pl.multiple_of(j * tk, tk)
        v = v_ref[pl.ds(start, tk), :]
        for h, s in enumerate(s_pair):
            if masked:
                row = lax.broadcasted_iota(jnp.int32, (tq, tk), 0)
                col = lax.broadcasted_iota(jnp.int32, (tq, tk), 1)
                s = jnp.where(col <= row, s, NEG_BIG)
            m_old = m_ref[h]
            m_new = jnp.maximum(m_old, jnp.max(s, axis=-1, keepdims=True))
            alpha = jnp.exp2(m_old - m_new)
            p = jnp.exp2(s - jnp.concatenate([m_new] * (tk // LANES), axis=1))
            acc_ref[h] = alpha * acc_ref[h] + _dot(p.astype(BF16), v[:, head_lanes(h)])
            m_ref[h] = m_new

    def process(chunks):
        s_all = [scores(j) for j, _ in chunks]
        for (j, masked), s_pair in zip(chunks, s_all):
            softmax_pv(j, s_pair, masked)

    def body(i, carry):
        process([(2 * i, False), (2 * i + 1, False)])
        return carry

    lax.fori_loop(0, qi // 2, body, 0)

    @pl.when(qi % 2 == 1)
    def _():
        process([(qi - 1, False), (qi, True)])

    @pl.when(qi % 2 == 0)
    def _():
        process([(qi, True)])

    first_half = lax.broadcasted_iota(jnp.int32, (tq, LANES), 1) < MLA_V
    a0, a1 = acc_ref[0], acc_ref[1]
    num = jnp.where(first_half, a0, a1)
    den = jnp.where(first_half, pltpu.roll(a0, MLA_V, axis=1), pltpu.roll(a1, MLA_V, axis=1))
    o_ref[...] = (num / den).astype(o_ref.dtype)


def _mla_attention(qm, km, vm, batch, seq):
    t = qm.shape[0]
    tq = MLA_TQ
    nq = seq // tq
    return pl.pallas_call(
        _mla_kernel,
        grid=(batch, HEAD_PAIRS, nq),
        in_specs=[pl.BlockSpec((tq, 2 * MLA_HEAD_PAD), lambda b, p, i: (b * nq + i, p)),
                  pl.BlockSpec((seq, 2 * MLA_HEAD_PAD), lambda b, p, i: (b, p)),
                  pl.BlockSpec((seq, 2 * LANES), lambda b, p, i: (b, p))],
        out_specs=pl.BlockSpec((tq, LANES), lambda b, p, i: (b * nq + i, p)),
        out_shape=jax.ShapeDtypeStruct((t, HEAD_PAIRS * LANES), BF16),
        scratch_shapes=[pltpu.VMEM((2, tq, LANES), F32), pltpu.VMEM((2, tq, LANES), F32)],
        compiler_params=_params(("parallel", "parallel", "parallel")),
        name="mla_attention",
    )(qm, km, vm)


def _sb_kernel(q_ref, k_ref, v_ref, tri_ref, o_ref, qh_ref, carry_ref, acc_ref):
    qi = pl.program_id(2)
    tq, tk, blk = SB_TQ, SB_TK, SB_BLK
    carry_ref[...] = jnp.zeros(carry_ref.shape, F32)
    acc_ref[...] = jnp.zeros(acc_ref.shape, F32)
    q2 = q_ref[...]
    first_half = lax.broadcasted_iota(jnp.int32, q2.shape, 1) < SB_HEAD_DIM
    zero = jnp.zeros_like(q2)
    qh_ref[0] = jnp.where(first_half, q2, zero)
    qh_ref[1] = jnp.where(first_half, zero, q2)

    def logits(j):
        start = pl.multiple_of(j * tk, tk)
        k = k_ref[pl.ds(start, tk), :]
        return tuple(_dot_nt(qh_ref[h], k) for h in range(2))

    def weights_pv(j, z_pair, masked):
        start = pl.multiple_of(j * tk, tk)
        v = v_ref[pl.ds(start, tk), :]
        if masked:
            row = lax.broadcasted_iota(jnp.int32, (tq, tk), 0)
            col = lax.broadcasted_iota(jnp.int32, (tq, tk), 1)
            valid = col < row
        for h, z in enumerate(z_pair):
            sp = jnp.maximum(z, jnp.log2(1.0 + jnp.exp2(jnp.minimum(z, SP_CLAMP))))
            own = z - sp
            if masked:
                sp = jnp.where(valid, sp, 0.0)
            sp16 = sp.astype(BF16)
            c = carry_ref[h]
            expo = [None] * (tk // blk)
            for b in reversed(range(tk // blk)):
                cols = slice(b * blk, (b + 1) * blk)
                later = _dot(sp16[:, cols], tri_ref[...])
                expo[b] = own[:, cols] - later - jnp.concatenate([c] * (blk // LANES), axis=1)
                c = c + jnp.sum(sp[:, cols], axis=-1, keepdims=True)
            carry_ref[h] = c
            a = jnp.exp2(jnp.concatenate(expo, axis=1))
            if masked:
                a = jnp.where(valid, a, 0.0)
            acc_ref[h] += _dot(a.astype(BF16), v)

    def process(chunks):
        z_all = [logits(j) for j, _ in chunks]
        for (j, masked), z_pair in zip(chunks, z_all):
            weights_pv(j, z_pair, masked)

    @pl.when(qi % 2 == 1)
    def _():
        process([(qi, True), (qi - 1, False)])

    @pl.when(qi % 2 == 0)
    def _():
        process([(qi, True)])

    n_left = qi - qi % 2

    def body(i, c):
        process([(n_left - 1 - 2 * i, False), (n_left - 2 - 2 * i, False)])
        return c

    lax.fori_loop(0, n_left // 2, body, 0)

    out_lane = lax.broadcasted_iota(jnp.int32, (tq, LANES), 1)
    o_ref[...] = jnp.where(out_lane < SB_HEAD_DIM, acc_ref[0], acc_ref[1]).astype(o_ref.dtype)


def _sb_attention(qb, kb, vb, tri, batch, seq):
    t = qb.shape[0]
    tq = SB_TQ
    nq = seq // tq
    return pl.pallas_call(
        _sb_kernel,
        grid=(batch, HEAD_PAIRS, nq),
        in_specs=[pl.BlockSpec((tq, LANES), lambda b, p, i: (b * nq + i, p)),
                  pl.BlockSpec((seq, LANES), lambda b, p, i: (b, p)),
                  pl.BlockSpec((seq, LANES), lambda b, p, i: (b, p)),
                  _const_spec(tri.shape)],
        out_specs=pl.BlockSpec((tq, LANES), lambda b, p, i: (b * nq + i, p)),
        out_shape=jax.ShapeDtypeStruct((t, HEAD_PAIRS * LANES), BF16),
        scratch_shapes=[pltpu.VMEM((2, tq, LANES), BF16), pltpu.VMEM((2, tq, LANES), F32),
                        pltpu.VMEM((2, tq, LANES), F32)],
        compiler_params=_params(("parallel", "parallel", "parallel")),
        name="sb_attention",
    )(qb, kb, vb, tri)


def _swa_kernel(sink_ref, q_ref, k_ref, v_ref, kp_ref, vp_ref, o_ref, *, blocks_per_seq):
    i = pl.program_id(0)
    w = SWA_WINDOW
    not_first = (i % blocks_per_seq) != 0
    row = lax.broadcasted_iota(jnp.int32, (w, w), 0)
    col = lax.broadcasted_iota(jnp.int32, (w, w), 1)
    lane = lax.broadcasted_iota(jnp.int32, (w, LANES), 1)
    upper = lane >= SWA_HEAD_DIM
    valid_cur = col <= row
    valid_prev_inner = col > row
    valid_prev_first = valid_prev_inner & not_first
    for r in range(SWA_TQ // w):
        rows = slice(r * w, (r + 1) * w)
        prev_rows = slice((r - 1) * w, r * w)
        valid_prev = valid_prev_first if r == 0 else valid_prev_inner
        for p in range(HEAD_PAIRS):
            g = p // (HEAD_PAIRS // SWA_KV_HEADS)
            gl = slice(g * LANES, (g + 1) * LANES)
            q2 = q_ref[rows, p * LANES:(p + 1) * LANES]
            kc, vc = k_ref[rows, gl], v_ref[rows, gl]
            if r == 0:
                kp, vp = kp_ref[:, gl], vp_ref[:, gl]
            else:
                kp, vp = k_ref[prev_rows, gl], v_ref[prev_rows, gl]
            outs = []
            for hh in range(2):
                sink = sink_ref[2 * p + hh]
                zero = jnp.zeros_like(q2)
                qh = jnp.where(upper, q2, zero) if hh else jnp.where(upper, zero, q2)
                s_c = jnp.where(valid_cur, _dot_nt(qh, kc), NEG_BIG)
                s_p = jnp.where(valid_prev, _dot_nt(qh, kp), NEG_BIG)
                m = jnp.maximum(jnp.maximum(jnp.max(s_c, axis=-1, keepdims=True),
                                            jnp.max(s_p, axis=-1, keepdims=True)), sink)
                p_c = jnp.exp(s_c - m)
                p_p = jnp.exp(s_p - m)
                den = (jnp.sum(p_c, axis=-1, keepdims=True) + jnp.sum(p_p, axis=-1, keepdims=True)
                       + jnp.exp(sink - m))
                outs.append((_dot(p_c.astype(BF16), vc) + _dot(p_p.astype(BF16), vp)) / den)
            o_ref[rows, p * LANES:(p + 1) * LANES] = jnp.where(upper, outs[1], outs[0]).astype(o_ref.dtype)


def _swa_attention(sinks, qs, ks, vs, seq):
    t = qs.shape[0]
    tq = SWA_TQ
    per_tile = tq // SWA_WINDOW
    cur = lambda n: pl.BlockSpec((tq, n), lambda i: (i, 0))
    prev = lambda n: pl.BlockSpec((SWA_WINDOW, n), lambda i: (jnp.maximum(i * per_tile - 1, 0), 0))
    return pl.pallas_call(
        functools.partial(_swa_kernel, blocks_per_seq=seq // tq),
        grid=(t // tq,),
        in_specs=[pl.BlockSpec(memory_space=pltpu.SMEM), cur(512), cur(256), cur(256), prev(256), prev(256)],
        out_specs=cur(512),
        out_shape=jax.ShapeDtypeStruct((t, 512), BF16),
        compiler_params=_params(("parallel",)),
        name="swa_attention",
    )(sinks, qs, ks, vs, ks, vs)


def _post_kernel(x_ref, oa_ref, ob_ref, oc_ref, gate_ref, woa_ref, wob_ref, woc_ref, wout_ref,
                 gpost_ref, gpre_ref, wup_ref, wdown_ref, gmlp_ref, out_ref):
    d = D_MODEL
    mixed = (gate_ref[:, 0:d] * _dot(oa_ref[...], woa_ref[...])
             + gate_ref[:, d:2 * d] * _dot(ob_ref[...], wob_ref[...])
             + gate_ref[:, 2 * d:3 * d] * _dot(oc_ref[...], woc_ref[...]))
    x1 = x_ref[...] + _rms(_dot(mixed.astype(BF16), wout_ref[...]), gpost_ref[...])
    h = _rms(x1, gpre_ref[...]).astype(BF16)
    u = jnp.square(jnp.maximum(_dot(h, wup_ref[...]), 0.0)).astype(BF16)
    out_ref[...] = x1 + _rms(_dot(u, wdown_ref[...]), gmlp_ref[...])


def _post(x, oa, ob, oc, gates, woa, wob, woc, wout, g_post, g_pre, wup, wdown, g_mlp):
    t = x.shape[0]
    tm = POST_TM
    row = lambda n: pl.BlockSpec((tm, n), lambda i: (i, 0))
    consts = [woa, wob, woc, wout, g_post, g_pre, wup, wdown, g_mlp]
    return pl.pallas_call(
        _post_kernel,
        grid=(t // tm,),
        in_specs=[row(D_MODEL), row(512), row(512), row(512), row(N_BRANCHES * D_MODEL)]
                 + [_const_spec(c.shape) for c in consts],
        out_specs=row(D_MODEL),
        out_shape=jax.ShapeDtypeStruct((t, D_MODEL), F32),
        compiler_params=_params(("parallel",)),
        name="post",
    )(x, oa, ob, oc, gates, *consts)


def _rot_cols(w, heads, dim):
    lead = w.shape[:-1]
    w = w.reshape(*lead, heads, dim)
    return jnp.concatenate([-w[..., dim // 2:], w[..., :dim // 2]], axis=-1).reshape(*lead, heads * dim)


def _dup_kv(w):
    lead = w.shape[:-1]
    w = w.reshape(*lead, SWA_KV_HEADS, 1, SWA_HEAD_DIM)
    return jnp.broadcast_to(w, (*lead, SWA_KV_HEADS, 2, SWA_HEAD_DIM)).reshape(*lead, 2 * LANES)


def _layout_weights(w_in, w_uq, w_ukv):
    dep = w_in.shape[0]
    sizes = (MLA_Q_LORA, MLA_KV_LORA, MLA_ROPE, 512, 128, 128, 512, 512, 512, N_BRANCHES * D_MODEL)
    pts = []
    acc = 0
    for s in sizes[:-1]:
        acc += s
        pts.append(acc)
    c_q, c_kv, k_rope, q_s, k_s, v_s, q_b, k_b, v_b, gates = jnp.split(w_in, pts, axis=-1)

    def kpe_pad(w):
        z = jnp.zeros((dep, D_MODEL, MLA_NOPE), F32)
        return jnp.concatenate([z, w, jnp.zeros((dep, D_MODEL, 32), F32)], axis=-1)

    swa_scale = SWA_HEAD_DIM ** -0.5
    sb_scale = SB_HEAD_DIM ** -0.5
    w1 = jnp.concatenate([
        c_q, c_kv, kpe_pad(k_rope), kpe_pad(_rot_cols(k_rope, 1, MLA_ROPE)),
        q_s * swa_scale, _rot_cols(q_s, SWA_HEADS, SWA_HEAD_DIM) * swa_scale,
        _dup_kv(k_s), _dup_kv(_rot_cols(k_s, SWA_KV_HEADS, SWA_HEAD_DIM)), _dup_kv(v_s),
        q_b * sb_scale, k_b, v_b, gates], axis=-1).astype(BF16)
    assert w1.shape[-1] == W1_COLS

    uq = w_uq.reshape(dep, MLA_Q_LORA, MLA_HEADS, MLA_NOPE + MLA_ROPE)
    nope, pe = uq[..., :MLA_NOPE], uq[..., MLA_NOPE:]
    pe_rot = jnp.concatenate([-pe[..., MLA_ROPE // 2:], pe[..., :MLA_ROPE // 2]], axis=-1)
    zq = jnp.zeros((dep, MLA_Q_LORA, MLA_HEADS, 32), F32)
    wq_main = jnp.concatenate([nope, pe, zq], axis=-1).reshape(dep, MLA_Q_LORA, -1)
    wq_rot = jnp.concatenate([jnp.zeros_like(nope), pe_rot, zq], axis=-1).reshape(dep, MLA_Q_LORA, -1)
    wq = jnp.concatenate([wq_main, wq_rot], axis=-1).astype(BF16)

    ukv = w_ukv.reshape(dep, MLA_KV_LORA, MLA_HEADS, MLA_NOPE + MLA_V)
    k_nope, v = ukv[..., :MLA_NOPE], ukv[..., MLA_NOPE:]
    wk = jnp.concatenate([k_nope, jnp.zeros_like(k_nope)], axis=-1).reshape(dep, MLA_KV_LORA, -1)
    vp = v.reshape(dep, MLA_KV_LORA, HEAD_PAIRS, 2, MLA_V)
    zv = jnp.zeros_like(vp[..., 0, :])
    wv = jnp.concatenate([vp[..., 0, :], zv, zv, vp[..., 1, :]], axis=-1).reshape(dep, MLA_KV_LORA, -1)
    wkv = jnp.concatenate([wk, wv], axis=-1).astype(BF16)
    return w1, wq, wkv


def kernel(x, positions, g_mix_pre, w_in, b_gate, g_q_lat, g_kv_lat, w_uq, w_ukv, swa_sinks, w_o_mla, w_o_swa, w_o_sb, w_out, g_mix_post, g_mlp_pre, w_up, w_down, g_mlp_post):
    batch, seq, d = x.shape
    t = batch * seq
    tables = _rope_tables(positions)
    w1, wq, wkv = _layout_weights(w_in, w_uq, w_ukv)
    woa, wob, woc, wout = (w.astype(BF16) for w in (w_o_mla, w_o_swa, w_o_sb, w_out))
    wup, wdown = w_up.astype(BF16), w_down.astype(BF16)
    tri = (lax.broadcasted_iota(jnp.int32, (SB_BLK, SB_BLK), 0)
           > lax.broadcasted_iota(jnp.int32, (SB_BLK, SB_BLK), 1)).astype(BF16)
    vec = lambda g, l: g[l].reshape(1, -1)

    xt = x.reshape(t, d)
    for l in range(DEPTH):
        qm, km, vm, qs, ks, vs, qb, kb, vb, gates = _prep(
            xt, tables, vec(g_mix_pre, l), w1[l], vec(b_gate, l), vec(g_q_lat, l), vec(g_kv_lat, l),
            wq[l], wkv[l])
        oa = _mla_attention(qm, km, vm, batch, seq)
        ob = _swa_attention(swa_sinks[l], qs, ks, vs, seq)
        oc = _sb_attention(qb, kb, vb, tri, batch, seq)
        xt = _post(xt, oa, ob, oc, gates, woa[l], wob[l], woc[l], wout[l], vec(g_mix_post, l),
                   vec(g_mlp_pre, l), wup[l], wdown[l], vec(g_mlp_post, l))
    return xt.reshape(batch, seq, d)
```

```python
import functools

import jax
import jax.numpy as jnp
from jax import lax
from jax.experimental import pallas as pl
from jax.experimental.pallas import tpu as pltpu

F32 = jnp.float32
BF16 = jnp.bfloat16

D_MODEL = 1024
DEPTH = 4
MLA_HEADS = 8
MLA_Q_LORA = 256
MLA_KV_LORA = 128
MLA_NOPE = 64
MLA_ROPE = 32
MLA_V = 64
SWA_HEADS = 8
SWA_KV_HEADS = 2
SWA_HEAD_DIM = 64
SWA_WINDOW = 128
SB_HEADS = 8
SB_HEAD_DIM = 64
D_FF = 4 * D_MODEL
ROPE_THETA = 10000.0
EPS = 1e-6
N_BRANCHES = 3

LANES = 128
HEAD_PAIRS = 4
MLA_HEAD_PAD = 128
NEG_BIG = -1e30
LOG2E = 1.4426950408889634
SB_SKIP = 160.0
SP_CLAMP = 126.0
VMEM_LIMIT = 56 * 1024 * 1024

OFF_CQ = 0
OFF_CKV = OFF_CQ + MLA_Q_LORA
OFF_KPE = OFF_CKV + MLA_KV_LORA
OFF_KPE_ROT = OFF_KPE + LANES
OFF_QS = OFF_KPE_ROT + LANES
OFF_QS_ROT = OFF_QS + 512
OFF_KS = OFF_QS_ROT + 512
OFF_KS_ROT = OFF_KS + 256
OFF_VS = OFF_KS_ROT + 256
OFF_QB = OFF_VS + 256
OFF_KB = OFF_QB + 512
OFF_VB = OFF_KB + 512
OFF_GATE = OFF_VB + 512
W1_COLS = OFF_GATE + N_BRANCHES * D_MODEL

PREP_TM = 256
POST_TM = 256
MLA_TQ = 512
MLA_TK = 512
MLA_ROWS = 512
SB_TQ = 512
SB_TK = 512
SB_BLK = 256
SWA_TQ = 256
ROPE_TM = 2048


def _rms(x, g):
    return x * lax.rsqrt(jnp.mean(x * x, axis=-1, keepdims=True) + EPS) * g


def _dot(a, b):
    return jnp.dot(a, b, preferred_element_type=F32)


def _dot_nt(a, b):
    return lax.dot_general(a, b, (((1,), (1,)), ((), ())), preferred_element_type=F32)


def _const_spec(shape):
    return pl.BlockSpec(shape, lambda *_: (0,) * len(shape), pipeline_mode=pl.Buffered(1))


def _params(sem):
    return pltpu.CompilerParams(dimension_semantics=sem, vmem_limit_bytes=VMEM_LIMIT)


def _rope_table_kernel(pos_ref, inva_ref, maska_ref, invb_ref, ca_ref, sa_ref, cb_ref, sb_ref):
    pos = pos_ref[...].astype(F32)
    ang_a = pos * inva_ref[...]
    ca_ref[...] = jnp.cos(ang_a) * maska_ref[...]
    sa_ref[...] = jnp.sin(ang_a)
    ang_b = pos * invb_ref[...]
    cb_ref[...] = jnp.cos(ang_b)
    sb_ref[...] = jnp.sin(ang_b)


def _rope_tables(positions):
    t = positions.size
    pos = positions.reshape(t, 1)
    inv_a16 = 1.0 / (ROPE_THETA ** (jnp.arange(0, MLA_ROPE, 2, dtype=F32) / MLA_ROPE))
    inv_b32 = 1.0 / (ROPE_THETA ** (jnp.arange(0, SWA_HEAD_DIM, 2, dtype=F32) / SWA_HEAD_DIM))
    zeros = jnp.zeros
    inv_a = jnp.concatenate([zeros((MLA_NOPE,), F32), inv_a16, inv_a16, zeros((32,), F32)]).reshape(1, LANES)
    mask_a = jnp.concatenate([jnp.ones((MLA_NOPE + MLA_ROPE,), F32), zeros((32,), F32)]).reshape(1, LANES)
    inv_b = jnp.tile(inv_b32, 4).reshape(1, LANES)
    row = pl.BlockSpec((ROPE_TM, LANES), lambda i: (i, 0))
    vec = pl.BlockSpec((1, LANES), lambda i: (0, 0))
    out = jax.ShapeDtypeStruct((t, LANES), F32)
    return pl.pallas_call(
        _rope_table_kernel,
        grid=(t // ROPE_TM,),
        in_specs=[pl.BlockSpec((ROPE_TM, 1), lambda i: (i, 0)), vec, vec, vec],
        out_specs=[row, row, row, row],
        out_shape=[out, out, out, out],
        compiler_params=_params(("parallel",)),
        name="rope_tables",
    )(pos, inv_a, mask_a, inv_b)


def _prep_kernel(x_ref, g_ref, ca_ref, sa_ref, cb_ref, sb_ref, w1_ref, bg_ref, gq_ref, gkv_ref,
                 wq_ref, wkv_ref,
                 qm_ref, km_ref, vm_ref, qs_ref, ks_ref, vs_ref, qb_ref, kb_ref, vb_ref, gate_ref):
    h = _rms(x_ref[...], g_ref[...]).astype(BF16)

    def mm(lo, n):
        return _dot(h, w1_ref[:, lo:lo + n])

    ca, sa, cb, sb = ca_ref[...], sa_ref[...], cb_ref[...], sb_ref[...]

    cb4 = jnp.concatenate([cb] * 4, axis=1)
    sb4 = jnp.concatenate([sb] * 4, axis=1)
    qs_ref[...] = (mm(OFF_QS, 512) * cb4 + mm(OFF_QS_ROT, 512) * sb4).astype(BF16)
    ks_ref[...] = (mm(OFF_KS, 256) * cb4[:, :256] + mm(OFF_KS_ROT, 256) * sb4[:, :256]).astype(BF16)
    vs_ref[...] = mm(OFF_VS, 256).astype(BF16)

    qb_ref[...] = (mm(OFF_QB, 512) * LOG2E).astype(BF16)
    kb_ref[...] = mm(OFF_KB, 512).astype(BF16)
    vb_ref[...] = mm(OFF_VB, 512).astype(BF16)

    gate_ref[...] = jax.nn.sigmoid(mm(OFF_GATE, N_BRANCHES * D_MODEL) + bg_ref[...])

    scale = (MLA_NOPE + MLA_ROPE) ** -0.5 * LOG2E
    cqn = _rms(mm(OFF_CQ, MLA_Q_LORA), gq_ref[...]).astype(BF16)
    ca8 = jnp.concatenate([ca * scale] * MLA_HEADS, axis=1)
    sa8 = jnp.concatenate([sa * scale] * MLA_HEADS, axis=1)
    nq = MLA_HEADS * MLA_HEAD_PAD
    qm_ref[...] = (_dot(cqn, wq_ref[:, :nq]) * ca8 + _dot(cqn, wq_ref[:, nq:]) * sa8).astype(BF16)
    kpe = mm(OFF_KPE, LANES) * ca + mm(OFF_KPE_ROT, LANES) * sa
    ckvn = _rms(mm(OFF_CKV, MLA_KV_LORA), gkv_ref[...]).astype(BF16)
    kpe8 = jnp.concatenate([kpe] * MLA_HEADS, axis=1)
    km_ref[...] = (_dot(ckvn, wkv_ref[:, :nq]) + kpe8).astype(BF16)
    lane = lax.broadcasted_iota(jnp.int32, (1, nq), 1) % (2 * LANES)
    ones = ((lane >= MLA_V) & (lane < 2 * LANES - MLA_V)).astype(F32)
    vm_ref[...] = (_dot(ckvn, wkv_ref[:, nq:]) + ones).astype(BF16)


def _prep(x, tables, g_pre, w1, b_gate, g_q, g_kv, wq, wkv):
    t = x.shape[0]
    tm = PREP_TM
    ca, sa, cb, sb = tables

    def row(n):
        return pl.BlockSpec((tm, n), lambda i: (i, 0))

    def out(n, dt=BF16):
        return jax.ShapeDtypeStruct((t, n), dt)

    nq = MLA_HEADS * MLA_HEAD_PAD
    return pl.pallas_call(
        _prep_kernel,
        grid=(t // tm,),
        in_specs=[row(D_MODEL), _const_spec((1, D_MODEL)), row(LANES), row(LANES), row(LANES), row(LANES),
                  _const_spec(w1.shape), _const_spec(b_gate.shape), _const_spec(g_q.shape),
                  _const_spec(g_kv.shape), _const_spec(wq.shape), _const_spec(wkv.shape)],
        out_specs=[row(nq), row(nq), row(nq), row(512), row(256), row(256), row(512), row(512), row(512),
                   row(N_BRANCHES * D_MODEL)],
        out_shape=[out(nq), out(nq), out(nq), out(512), out(256), out(256), out(512), out(512), out(512),
                   out(N_BRANCHES * D_MODEL, F32)],
        compiler_params=_params(("parallel",)),
        name="prep",
    )(x, g_pre, ca, sa, cb, sb, w1, b_gate, g_q, g_kv, wq, wkv)


def _mla_kernel(q_ref, k_ref, v_ref, o_ref, m_ref, acc_ref):
    qi = pl.program_id(2)
    tq, tk = MLA_TQ, MLA_TK
    m_ref[...] = jnp.full(m_ref.shape, NEG_BIG, F32)
    acc_ref[...] = jnp.zeros(acc_ref.shape, F32)

    def head_lanes(h):
        return slice(h * MLA_HEAD_PAD, (h + 1) * MLA_HEAD_PAD)

    def scores(j):
        start = pl.multiple_of(j * tk, tk)
        k = k_ref[pl.ds(start, tk), :]
        return tuple(_dot_nt(q_ref[:, head_lanes(h)], k[:, head_lanes(h)]) for h in range(2))

    def softmax_pv(j, s_pair, masked):
        start = pl.multiple_of(j * tk, tk)
        v = v_ref[pl.ds(start, tk), :]
        for h, s in enumerate(s_pair):
            if masked:
                row = lax.broadcasted_iota(jnp.int32, (tq, tk), 0)
                col = lax.broadcasted_iota(jnp.int32, (tq, tk), 1)
                s = jnp.where(col <= row, s, NEG_BIG)
            m_old = m_ref[h]
            m_new = jnp.maximum(m_old, jnp.max(s, axis=-1, keepdims=True))
            alpha = jnp.exp2(m_old - m_new)
            p = jnp.exp2(s - jnp.concatenate([m_new] * (tk // LANES), axis=1))
            acc_ref[h] = alpha * acc_ref[h] + _dot(p.astype(BF16), v[:, head_lanes(h)])
            m_ref[h] = m_new

    def process(chunks):
        s_all = [scores(j) for j, _ in chunks]
        for (j, masked), s_pair in zip(chunks, s_all):
            softmax_pv(j, s_pair, masked)

    def body(i, carry):
        process([(2 * i, False), (2 * i + 1, False)])
        return carry

    lax.fori_loop(0, qi // 2, body, 0)

    @pl.when(qi % 2 == 1)
    def _():
        process([(qi - 1, False), (qi, True)])

    @pl.when(qi % 2 == 0)
    def _():
        process([(qi, True)])

    first_half = lax.broadcasted_iota(jnp.int32, (tq, LANES), 1) < MLA_V
    a0, a1 = acc_ref[0], acc_ref[1]
    num = jnp.where(first_half, a0, a1)
    den = jnp.where(first_half, pltpu.roll(a0, MLA_V, axis=1), pltpu.roll(a1, MLA_V, axis=1))
    o_ref[...] = (num / den).astype(o_ref.dtype)


def _mla_attention(qm, km, vm, batch, seq):
    t = qm.shape[0]
    tq = MLA_TQ
    nq = seq // tq
    return pl.pallas_call(
        _mla_kernel,
        grid=(batch, HEAD_PAIRS, nq),
        in_specs=[pl.BlockSpec((tq, 2 * MLA_HEAD_PAD), lambda b, p, i: (b * nq + i, p)),
                  pl.BlockSpec((seq, 2 * MLA_HEAD_PAD), lambda b, p, i: (b, p)),
                  pl.BlockSpec((seq, 2 * LANES), lambda b, p, i: (b, p))],
        out_specs=pl.BlockSpec((tq, LANES), lambda b, p, i: (b * nq + i, p)),
        out_shape=jax.ShapeDtypeStruct((t, HEAD_PAIRS * LANES), BF16),
        scratch_shapes=[pltpu.VMEM((2, tq, LANES), F32), pltpu.VMEM((2, tq, LANES), F32)],
        compiler_params=_params(("parallel", "parallel", "parallel")),
        name="mla_attention",
    )(qm, km, vm)


def _sb_kernel(q_ref, k_ref, v_ref, tri_ref, o_ref, qh_ref, carry_ref, acc_ref):
    qi = pl.program_id(2)
    tq, tk, blk = SB_TQ, SB_TK, SB_BLK
    carry_ref[...] = jnp.zeros(carry_ref.shape, F32)
    acc_ref[...] = jnp.zeros(acc_ref.shape, F32)
    q2 = q_ref[...]
    first_half = lax.broadcasted_iota(jnp.int32, q2.shape, 1) < SB_HEAD_DIM
    zero = jnp.zeros_like(q2)
    qh_ref[0] = jnp.where(first_half, q2, zero)
    qh_ref[1] = jnp.where(first_half, zero, q2)

    def logits(j):
        start = pl.multiple_of(j * tk, tk)
        k = k_ref[pl.ds(start, tk), :]
        return tuple(_dot_nt(qh_ref[h], k) for h in range(2))

    def weights_pv(j, z_pair, masked):
        start = pl.multiple_of(j * tk, tk)
        v = v_ref[pl.ds(start, tk), :]
        if masked:
            row = lax.broadcasted_iota(jnp.int32, (tq, tk), 0)
            col = lax.broadcasted_iota(jnp.int32, (tq, tk), 1)
            valid = col < row
        for h, z in enumerate(z_pair):
            sp = jnp.maximum(z, jnp.log2(1.0 + jnp.exp2(jnp.minimum(z, SP_CLAMP))))
            own = z - sp
            if masked:
                sp = jnp.where(valid, sp, 0.0)
            sp16 = sp.astype(BF16)
            c = carry_ref[h]
            expo = [None] * (tk // blk)
            for b in reversed(range(tk // blk)):
                cols = slice(b * blk, (b + 1) * blk)
                later = _dot(sp16[:, cols], tri_ref[...])
                expo[b] = own[:, cols] - later - jnp.concatenate([c] * (blk // LANES), axis=1)
                c = c + jnp.sum(sp[:, cols], axis=-1, keepdims=True)
            carry_ref[h] = c
            a = jnp.exp2(jnp.concatenate(expo, axis=1))
            if masked:
                a = jnp.where(valid, a, 0.0)
            acc_ref[h] += _dot(a.astype(BF16), v)

    def process(chunks):
        z_all = [logits(j) for j, _ in chunks]
        for (j, masked), z_pair in zip(chunks, z_all):
            weights_pv(j, z_pair, masked)

    def min_carry():
        return jnp.min(jnp.minimum(carry_ref[0], carry_ref[1]))

    def more(state):
        j, cmin = state
        return jnp.logical_and(j >= 0, cmin < SB_SKIP)

    def step(state):
        j, _ = state
        process([(j, False)])
        return j - 1, min_carry()

    process([(qi, True)])
    lax.while_loop(more, step, (qi - 1, min_carry()))

    out_lane = lax.broadcasted_iota(jnp.int32, (tq, LANES), 1)
    o_ref[...] = jnp.where(out_lane < SB_HEAD_DIM, acc_ref[0], acc_ref[1]).astype(o_ref.dtype)


def _sb_attention(qb, kb, vb, tri, batch, seq):
    t = qb.shape[0]
    tq = SB_TQ
    nq = seq // tq
    return pl.pallas_call(
        _sb_kernel,
        grid=(batch, HEAD_PAIRS, nq),
        in_specs=[pl.BlockSpec((tq, LANES), lambda b, p, i: (b * nq + i, p)),
                  pl.BlockSpec((seq, LANES), lambda b, p, i: (b, p)),
                  pl.BlockSpec((seq, LANES), lambda b, p, i: (b, p)),
                  _const_spec(tri.shape)],
        out_specs=pl.BlockSpec((tq, LANES), lambda b, p, i: (b * nq + i, p)),
        out_shape=jax.ShapeDtypeStruct((t, HEAD_PAIRS * LANES), BF16),
        scratch_shapes=[pltpu.VMEM((2, tq, LANES), BF16), pltpu.VMEM((2, tq, LANES), F32),
                        pltpu.VMEM((2, tq, LANES), F32)],
        compiler_params=_params(("parallel", "parallel", "parallel")),
        name="sb_attention",
    )(qb, kb, vb, tri)


def _swa_kernel(sink_ref, q_ref, k_ref, v_ref, kp_ref, vp_ref, o_ref, *, blocks_per_seq):
    i = pl.program_id(0)
    w = SWA_WINDOW
    not_first = (i % blocks_per_seq) != 0
    row = lax.broadcasted_iota(jnp.int32, (w, w), 0)
    col = lax.broadcasted_iota(jnp.int32, (w, w), 1)
    lane = lax.broadcasted_iota(jnp.int32, (w, LANES), 1)
    upper = lane >= SWA_HEAD_DIM
    valid_cur = col <= row
    valid_prev_inner = col > row
    valid_prev_first = valid_prev_inner & not_first
    for r in range(SWA_TQ // w):
        rows = slice(r * w, (r + 1) * w)
        prev_rows = slice((r - 1) * w, r * w)
        valid_prev = valid_prev_first if r == 0 else valid_prev_inner
        for p in range(HEAD_PAIRS):
            g = p // (HEAD_PAIRS // SWA_KV_HEADS)
            gl = slice(g * LANES, (g + 1) * LANES)
            q2 = q_ref[rows, p * LANES:(p + 1) * LANES]
            kc, vc = k_ref[rows, gl], v_ref[rows, gl]
            if r == 0:
                kp, vp = kp_ref[:, gl], vp_ref[:, gl]
            else:
                kp, vp = k_ref[prev_rows, gl], v_ref[prev_rows, gl]
            outs = []
            for hh in range(2):
                sink = sink_ref[2 * p + hh]
                zero = jnp.zeros_like(q2)
                qh = jnp.where(upper, q2, zero) if hh else jnp.where(upper, zero, q2)
                s_c = jnp.where(valid_cur, _dot_nt(qh, kc), NEG_BIG)
                s_p = jnp.where(valid_prev, _dot_nt(qh, kp), NEG_BIG)
                m = jnp.maximum(jnp.maximum(jnp.max(s_c, axis=-1, keepdims=True),
                                            jnp.max(s_p, axis=-1, keepdims=True)), sink)
                p_c = jnp.exp(s_c - m)
                p_p = jnp.exp(s_p - m)
                den = (jnp.sum(p_c, axis=-1, keepdims=True) + jnp.sum(p_p, axis=-1, keepdims=True)
                       + jnp.exp(sink - m))
                outs.append((_dot(p_c.astype(BF16), vc) + _dot(p_p.astype(BF16), vp)) / den)
            o_ref[rows, p * LANES:(p + 1) * LANES] = jnp.where(upper, outs[1], outs[0]).astype(o_ref.dtype)


def _swa_attention(sinks, qs, ks, vs, seq):
    t = qs.shape[0]
    tq = SWA_TQ
    per_tile = tq // SWA_WINDOW
    cur = lambda n: pl.BlockSpec((tq, n), lambda i: (i, 0))
    prev = lambda n: pl.BlockSpec((SWA_WINDOW, n), lambda i: (jnp.maximum(i * per_tile - 1, 0), 0))
    return pl.pallas_call(
        functools.partial(_swa_kernel, blocks_per_seq=seq // tq),
        grid=(t // tq,),
        in_specs=[pl.BlockSpec(memory_space=pltpu.SMEM), cur(512), cur(256), cur(256), prev(256), prev(256)],
        out_specs=cur(512),
        out_shape=jax.ShapeDtypeStruct((t, 512), BF16),
        compiler_params=_params(("parallel",)),
        name="swa_attention",
    )(sinks, qs, ks, vs, ks, vs)


def _post_kernel(x_ref, oa_ref, ob_ref, oc_ref, gate_ref, woa_ref, wob_ref, woc_ref, wout_ref,
                 gpost_ref, gpre_ref, wup_ref, wdown_ref, gmlp_ref, out_ref):
    d = D_MODEL
    mixed = (gate_ref[:, 0:d] * _dot(oa_ref[...], woa_ref[...])
             + gate_ref[:, d:2 * d] * _dot(ob_ref[...], wob_ref[...])
             + gate_ref[:, 2 * d:3 * d] * _dot(oc_ref[...], woc_ref[...]))
    x1 = x_ref[...] + _rms(_dot(mixed.astype(BF16), wout_ref[...]), gpost_ref[...])
    h = _rms(x1, gpre_ref[...]).astype(BF16)
    u = jnp.square(jnp.maximum(_dot(h, wup_ref[...]), 0.0)).astype(BF16)
    out_ref[...] = x1 + _rms(_dot(u, wdown_ref[...]), gmlp_ref[...])


def _post(x, oa, ob, oc, gates, woa, wob, woc, wout, g_post, g_pre, wup, wdown, g_mlp):
    t = x.shape[0]
    tm = POST_TM
    row = lambda n: pl.BlockSpec((tm, n), lambda i: (i, 0))
    consts = [woa, wob, woc, wout, g_post, g_pre, wup, wdown, g_mlp]
    return pl.pallas_call(
        _post_kernel,
        grid=(t // tm,),
        in_specs=[row(D_MODEL), row(512), row(512), row(512), row(N_BRANCHES * D_MODEL)]
                 + [_const_spec(c.shape) for c in consts],
        out_specs=row(D_MODEL),
        out_shape=jax.ShapeDtypeStruct((t, D_MODEL), F32),
        compiler_params=_params(("parallel",)),
        name="post",
    )(x, oa, ob, oc, gates, *consts)


def _rot_cols(w, heads, dim):
    lead = w.shape[:-1]
    w = w.reshape(*lead, heads, dim)
    return jnp.concatenate([-w[..., dim // 2:], w[..., :dim // 2]], axis=-1).reshape(*lead, heads * dim)


def _dup_kv(w):
    lead = w.shape[:-1]
    w = w.reshape(*lead, SWA_KV_HEADS, 1, SWA_HEAD_DIM)
    return jnp.broadcast_to(w, (*lead, SWA_KV_HEADS, 2, SWA_HEAD_DIM)).reshape(*lead, 2 * LANES)


def _layout_weights(w_in, w_uq, w_ukv):
    dep = w_in.shape[0]
    sizes = (MLA_Q_LORA, MLA_KV_LORA, MLA_ROPE, 512, 128, 128, 512, 512, 512, N_BRANCHES * D_MODEL)
    pts = []
    acc = 0
    for s in sizes[:-1]:
        acc += s
        pts.append(acc)
    c_q, c_kv, k_rope, q_s, k_s, v_s, q_b, k_b, v_b, gates = jnp.split(w_in, pts, axis=-1)

    def kpe_pad(w):
        z = jnp.zeros((dep, D_MODEL, MLA_NOPE), F32)
        return jnp.concatenate([z, w, jnp.zeros((dep, D_MODEL, 32), F32)], axis=-1)

    swa_scale = SWA_HEAD_DIM ** -0.5
    sb_scale = SB_HEAD_DIM ** -0.5
    w1 = jnp.concatenate([
        c_q, c_kv, kpe_pad(k_rope), kpe_pad(_rot_cols(k_rope, 1, MLA_ROPE)),
        q_s * swa_scale, _rot_cols(q_s, SWA_HEADS, SWA_HEAD_DIM) * swa_scale,
        _dup_kv(k_s), _dup_kv(_rot_cols(k_s, SWA_KV_HEADS, SWA_HEAD_DIM)), _dup_kv(v_s),
        q_b * sb_scale, k_b, v_b, gates], axis=-1).astype(BF16)
    assert w1.shape[-1] == W1_COLS

    uq = w_uq.reshape(dep, MLA_Q_LORA, MLA_HEADS, MLA_NOPE + MLA_ROPE)
    nope, pe = uq[..., :MLA_NOPE], uq[..., MLA_NOPE:]
    pe_rot = jnp.concatenate([-pe[..., MLA_ROPE // 2:], pe[..., :MLA_ROPE // 2]], axis=-1)
    zq = jnp.zeros((dep, MLA_Q_LORA, MLA_HEADS, 32), F32)
    wq_main = jnp.concatenate([nope, pe, zq], axis=-1).reshape(dep, MLA_Q_LORA, -1)
    wq_rot = jnp.concatenate([jnp.zeros_like(nope), pe_rot, zq], axis=-1).reshape(dep, MLA_Q_LORA, -1)
    wq = jnp.concatenate([wq_main, wq_rot], axis=-1).astype(BF16)

    ukv = w_ukv.reshape(dep, MLA_KV_LORA, MLA_HEADS, MLA_NOPE + MLA_V)
    k_nope, v = ukv[..., :MLA_NOPE], ukv[..., MLA_NOPE:]
    wk = jnp.concatenate([k_nope, jnp.zeros_like(k_nope)], axis=-1).reshape(dep, MLA_KV_LORA, -1)
    vp = v.reshape(dep, MLA_KV_LORA, HEAD_PAIRS, 2, MLA_V)
    zv = jnp.zeros_like(vp[..., 0, :])
    wv = jnp.concatenate([vp[..., 0, :], zv, zv, vp[..., 1, :]], axis=-1).reshape(dep, MLA_KV_LORA, -1)
    wkv = jnp.concatenate([wk, wv], axis=-1).astype(BF16)
    return w1, wq, wkv


def kernel(x, positions, g_mix_pre, w_in, b_gate, g_q_lat, g_kv_lat, w_uq, w_ukv, swa_sinks, w_o_mla, w_o_swa, w_o_sb, w_out, g_mix_post, g_mlp_pre, w_up, w_down, g_mlp_post):
    batch, seq, d = x.shape
    t = batch * seq
    tables = _rope_tables(positions)
    w1, wq, wkv = _layout_weights(w_in, w_uq, w_ukv)
    woa, wob, woc, wout = (w.astype(BF16) for w in (w_o_mla, w_o_swa, w_o_sb, w_out))
    wup, wdown = w_up.astype(BF16), w_down.astype(BF16)
    tri = (lax.broadcasted_iota(jnp.int32, (SB_BLK, SB_BLK), 0)
           > lax.broadcasted_iota(jnp.int32, (SB_BLK, SB_BLK), 1)).astype(BF16)
    vec = lambda g, l: g[l].reshape(1, -1)

    xt = x.reshape(t, d)
    for l in range(DEPTH):
        qm, km, vm, qs, ks, vs, qb, kb, vb, gates = _prep(
            xt, tables, vec(g_mix_pre, l), w1[l], vec(b_gate, l), vec(g_q_lat, l), vec(g_kv_lat, l),
            wq[l], wkv[l])
        oa = _mla_attention(qm, km, vm, batch, seq)
        ob = _swa_attention(swa_sinks[l], qs, ks, vs, seq)
        oc = _sb_attention(qb, kb, vb, tri, batch, seq)
        xt = _post(xt, oa, ob, oc, gates, woa[l], wob[l], woc[l], wout[l], vec(g_mix_post, l),
                   vec(g_mlp_pre, l), wup[l], wdown[l], vec(g_mlp_post, l))
    return xt.reshape(batch, seq, d)
```

```python
import functools

import jax
import jax.numpy as jnp
from jax import lax
from jax.experimental import pallas as pl
from jax.experimental.pallas import tpu as pltpu

F32 = jnp.float32
BF16 = jnp.bfloat16

D_MODEL = 1024
DEPTH = 4
MLA_HEADS = 8
MLA_Q_LORA = 256
MLA_KV_LORA = 128
MLA_NOPE = 64
MLA_ROPE = 32
MLA_V = 64
SWA_HEADS = 8
SWA_KV_HEADS = 2
SWA_HEAD_DIM = 64
SWA_WINDOW = 128
SB_HEADS = 8
SB_HEAD_DIM = 64
D_FF = 4 * D_MODEL
ROPE_THETA = 10000.0
EPS = 1e-6
N_BRANCHES = 3

LANES = 128
HEAD_PAIRS = 4
MLA_HEAD_PAD = 128
NEG_BIG = -1e30
LOG2E = 1.4426950408889634
SB_SKIP = 160.0
SP_CLAMP = 126.0
VMEM_LIMIT = 56 * 1024 * 1024

OFF_CQ = 0
OFF_CKV = OFF_CQ + MLA_Q_LORA
OFF_KPE = OFF_CKV + MLA_KV_LORA
OFF_KPE_ROT = OFF_KPE + LANES
OFF_QS = OFF_KPE_ROT + LANES
OFF_QS_ROT = OFF_QS + 512
OFF_KS = OFF_QS_ROT + 512
OFF_KS_ROT = OFF_KS + 256
OFF_VS = OFF_KS_ROT + 256
OFF_QB = OFF_VS + 512
OFF_KB = OFF_QB + 512
OFF_VB = OFF_KB + 512
OFF_GATE = OFF_VB + 512
W1_COLS = OFF_GATE + N_BRANCHES * D_MODEL

PREP_TM = 256
POST_TM = 256
MLA_TQ = 512
MLA_TK = 512
SB_TQ = 256
SB_TK = 256
SB_BLK = 256
SWA_TQ = 256
ROPE_TM = 2048


def _rms(x, g):
    return x * lax.rsqrt(jnp.mean(x * x, axis=-1, keepdims=True) + EPS) * g


def _dot(a, b):
    return jnp.dot(a, b, preferred_element_type=F32)


def _dot_nt(a, b):
    return lax.dot_general(a, b, (((1,), (1,)), ((), ())), preferred_element_type=F32)


def _const_spec(shape):
    return pl.BlockSpec(shape, lambda *_: (0,) * len(shape), pipeline_mode=pl.Buffered(1))


def _layer_spec(arr, layer):
    return pl.BlockSpec((None,) + arr.shape[1:], lambda *_: (layer, 0, 0), pipeline_mode=pl.Buffered(1))


def _params(sem):
    return pltpu.CompilerParams(dimension_semantics=sem, vmem_limit_bytes=VMEM_LIMIT)


def _pair_ones(width):
    lane = lax.broadcasted_iota(jnp.int32, (1, width), 1) % (2 * LANES)
    return ((lane >= 64) & (lane < 2 * LANES - 64)).astype(F32)


def _rope_table_kernel(pos_ref, inva_ref, maska_ref, invb_ref, ca_ref, sa_ref, cb_ref, sb_ref):
    pos = pos_ref[...].astype(F32)
    ang_a = pos * inva_ref[...]
    ca_ref[...] = jnp.cos(ang_a) * maska_ref[...]
    sa_ref[...] = jnp.sin(ang_a)
    ang_b = pos * invb_ref[...]
    cb_ref[...] = jnp.cos(ang_b)
    sb_ref[...] = jnp.sin(ang_b)


def _rope_tables(positions):
    t = positions.size
    pos = positions.reshape(t, 1)
    inv_a16 = 1.0 / (ROPE_THETA ** (jnp.arange(0, MLA_ROPE, 2, dtype=F32) / MLA_ROPE))
    inv_b32 = 1.0 / (ROPE_THETA ** (jnp.arange(0, SWA_HEAD_DIM, 2, dtype=F32) / SWA_HEAD_DIM))
    zeros = jnp.zeros
    inv_a = jnp.concatenate([zeros((MLA_NOPE,), F32), inv_a16, inv_a16, zeros((32,), F32)]).reshape(1, LANES)
    mask_a = jnp.concatenate([jnp.ones((MLA_NOPE + MLA_ROPE,), F32), zeros((32,), F32)]).reshape(1, LANES)
    inv_b = jnp.tile(inv_b32, 4).reshape(1, LANES)
    row = pl.BlockSpec((ROPE_TM, LANES), lambda i: (i, 0))
    vec = pl.BlockSpec((1, LANES), lambda i: (0, 0))
    out = jax.ShapeDtypeStruct((t, LANES), F32)
    return pl.pallas_call(
        _rope_table_kernel,
        grid=(t // ROPE_TM,),
        in_specs=[pl.BlockSpec((ROPE_TM, 1), lambda i: (i, 0)), vec, vec, vec],
        out_specs=[row, row, row, row],
        out_shape=[out, out, out, out],
        compiler_params=_params(("parallel",)),
        name="rope_tables",
    )(pos, inv_a, mask_a, inv_b)


def _prep_kernel(x_ref, g_ref, ca_ref, sa_ref, cb_ref, sb_ref, w1_ref, bg_ref, gq_ref, gkv_ref,
                 wq_ref, wkv_ref,
                 qm_ref, km_ref, vm_ref, qs_ref, ks_ref, vs_ref, qb_ref, kb_ref, vb_ref, gate_ref):
    h = _rms(x_ref[...], g_ref[...]).astype(BF16)

    def mm(lo, n):
        return _dot(h, w1_ref[:, lo:lo + n])

    ca, sa, cb, sb = ca_ref[...], sa_ref[...], cb_ref[...], sb_ref[...]

    cb4 = jnp.concatenate([cb * LOG2E] * 4, axis=1)
    sb4 = jnp.concatenate([sb * LOG2E] * 4, axis=1)
    qs_ref[...] = (mm(OFF_QS, 512) * cb4 + mm(OFF_QS_ROT, 512) * sb4).astype(BF16)
    cb2 = jnp.concatenate([cb] * 2, axis=1)
    sb2 = jnp.concatenate([sb] * 2, axis=1)
    ks_ref[...] = (mm(OFF_KS, 256) * cb2 + mm(OFF_KS_ROT, 256) * sb2).astype(BF16)
    vs_ref[...] = (mm(OFF_VS, 512) + _pair_ones(512)).astype(BF16)

    qb_ref[...] = (mm(OFF_QB, 512) * LOG2E).astype(BF16)
    kb_ref[...] = mm(OFF_KB, 512).astype(BF16)
    vb_ref[...] = mm(OFF_VB, 512).astype(BF16)

    gate_ref[...] = jax.nn.sigmoid(mm(OFF_GATE, N_BRANCHES * D_MODEL) + bg_ref[...])

    scale = (MLA_NOPE + MLA_ROPE) ** -0.5 * LOG2E
    cqn = _rms(mm(OFF_CQ, MLA_Q_LORA), gq_ref[...]).astype(BF16)
    ca8 = jnp.concatenate([ca * scale] * MLA_HEADS, axis=1)
    sa8 = jnp.concatenate([sa * scale] * MLA_HEADS, axis=1)
    nq = MLA_HEADS * MLA_HEAD_PAD
    qm_ref[...] = (_dot(cqn, wq_ref[:, :nq]) * ca8 + _dot(cqn, wq_ref[:, nq:]) * sa8).astype(BF16)
    kpe = mm(OFF_KPE, LANES) * ca + mm(OFF_KPE_ROT, LANES) * sa
    ckvn = _rms(mm(OFF_CKV, MLA_KV_LORA), gkv_ref[...]).astype(BF16)
    kpe8 = jnp.concatenate([kpe] * MLA_HEADS, axis=1)
    km_ref[...] = (_dot(ckvn, wkv_ref[:, :nq]) + kpe8).astype(BF16)
    vm_ref[...] = (_dot(ckvn, wkv_ref[:, nq:]) + _pair_ones(nq)).astype(BF16)


def _prep(x, tables, layer, g_pre, w1, b_gate, g_q, g_kv, wq, wkv):
    t = x.shape[0]
    tm = PREP_TM
    ca, sa, cb, sb = tables

    def row(n):
        return pl.BlockSpec((tm, n), lambda i: (i, 0))

    def out(n, dt=BF16):
        return jax.ShapeDtypeStruct((t, n), dt)

    nq = MLA_HEADS * MLA_HEAD_PAD
    consts = [w1, b_gate, g_q, g_kv, wq, wkv]
    return pl.pallas_call(
        _prep_kernel,
        grid=(t // tm,),
        in_specs=[row(D_MODEL), _layer_spec(g_pre, layer), row(LANES), row(LANES), row(LANES), row(LANES)]
                 + [_layer_spec(c, layer) for c in consts],
        out_specs=[row(nq), row(nq), row(nq), row(512), row(256), row(512), row(512), row(512), row(512),
                   row(N_BRANCHES * D_MODEL)],
        out_shape=[out(nq), out(nq), out(nq), out(512), out(256), out(512), out(512), out(512), out(512),
                   out(N_BRANCHES * D_MODEL, F32)],
        compiler_params=_params(("parallel",)),
        name="prep",
    )(x, g_pre, ca, sa, cb, sb, *consts)


def _mla_kernel(q_ref, k_ref, v_ref, o_ref, m_ref, acc_ref):
    qi = pl.program_id(2)
    tq, tk = MLA_TQ, MLA_TK
    m_ref[...] = jnp.full(m_ref.shape, NEG_BIG, F32)
    acc_ref[...] = jnp.zeros(acc_ref.shape, F32)

    def head_lanes(h):
        return slice(h * MLA_HEAD_PAD, (h + 1) * MLA_HEAD_PAD)

    def scores(j):
        start = pl.multiple_of(j * tk, tk)
        k = k_ref[pl.ds(start, tk), :]
        return tuple(_dot_nt(q_ref[:, head_lanes(h)], k[:, head_lanes(h)]) for h in range(2))

    def softmax_pv(j, s_pair, masked):
        start = pl.multiple_of(j * tk, tk)
        v = v_ref[pl.ds(start, tk), :]
        for h, s in enumerate(s_pair):
            if masked:
                row = lax.broadcasted_iota(jnp.int32, (tq, tk), 0)
                col = lax.broadcasted_iota(jnp.int32, (tq, tk), 1)
                s = jnp.where(col <= row, s, NEG_BIG)
            m_old = m_ref[h]
            m_new = jnp.maximum(m_old, jnp.max(s, axis=-1, keepdims=True))
            alpha = jnp.exp2(m_old - m_new)
            p = jnp.exp2(s - jnp.concatenate([m_new] * (tk // LANES), axis=1))
            acc_ref[h] = alpha * acc_ref[h] + _dot(p.astype(BF16), v[:, head_lanes(h)])
            m_ref[h] = m_new

    def process(chunks):
        s_all = [scores(j) for j, _ in chunks]
        for (j, masked), s_pair in zip(chunks, s_all):
            softmax_pv(j, s_pair, masked)

    def body(i, carry):
        process([(2 * i, False), (2 * i + 1, False)])
        return carry

    lax.fori_loop(0, qi // 2, body, 0)

    @pl.when(qi % 2 == 1)
    def _():
        process([(qi - 1, False), (qi, True)])

    @pl.when(qi % 2 == 0)
    def _():
        process([(qi, True)])

    first_half = lax.broadcasted_iota(jnp.int32, (tq, LANES), 1) < MLA_V
    a0, a1 = acc_ref[0], acc_ref[1]
    num = jnp.where(first_half, a0, a1)
    den = jnp.where(first_half, pltpu.roll(a0, MLA_V, axis=1), pltpu.roll(a1, MLA_V, axis=1))
    o_ref[...] = (num / den).astype(o_ref.dtype)


def _mla_attention(qm, km, vm, batch, seq):
    t = qm.shape[0]
    tq = MLA_TQ
    nq = seq // tq
    return pl.pallas_call(
        _mla_kernel,
        grid=(batch, HEAD_PAIRS, nq),
        in_specs=[pl.BlockSpec((tq, 2 * MLA_HEAD_PAD), lambda b, p, i: (b * nq + i, p)),
                  pl.BlockSpec((seq, 2 * MLA_HEAD_PAD), lambda b, p, i: (b, p)),
                  pl.BlockSpec((seq, 2 * LANES), lambda b, p, i: (b, p))],
        out_specs=pl.BlockSpec((tq, LANES), lambda b, p, i: (b * nq + i, p)),
        out_shape=jax.ShapeDtypeStruct((t, HEAD_PAIRS * LANES), BF16),
        scratch_shapes=[pltpu.VMEM((2, tq, LANES), F32), pltpu.VMEM((2, tq, LANES), F32)],
        compiler_params=_params(("parallel", "parallel", "parallel")),
        name="mla_attention",
    )(qm, km, vm)


def _sb_kernel(q_ref, k_ref, v_ref, tri_ref, o_ref, qh_ref, carry_ref, acc_ref):
    qi = pl.program_id(2)
    tq, tk, blk = SB_TQ, SB_TK, SB_BLK
    carry_ref[...] = jnp.zeros(carry_ref.shape, F32)
    acc_ref[...] = jnp.zeros(acc_ref.shape, F32)
    q2 = q_ref[...]
    first_half = lax.broadcasted_iota(jnp.int32, q2.shape, 1) < SB_HEAD_DIM
    zero = jnp.zeros_like(q2)
    qh_ref[0] = jnp.where(first_half, q2, zero)
    qh_ref[1] = jnp.where(first_half, zero, q2)

    def logits(j):
        start = pl.multiple_of(j * tk, tk)
        k = k_ref[pl.ds(start, tk), :]
        return tuple(_dot_nt(qh_ref[h], k) for h in range(2))

    def weights_pv(j, z_pair, masked):
        start = pl.multiple_of(j * tk, tk)
        v = v_ref[pl.ds(start, tk), :]
        if masked:
            row = lax.broadcasted_iota(jnp.int32, (tq, tk), 0)
            col = lax.broadcasted_iota(jnp.int32, (tq, tk), 1)
            valid = col < row
        for h, z in enumerate(z_pair):
            sp = jnp.maximum(z, jnp.log2(1.0 + jnp.exp2(jnp.minimum(z, SP_CLAMP))))
            own = z - sp
            if masked:
                sp = jnp.where(valid, sp, 0.0)
            sp16 = sp.astype(BF16)
            c = carry_ref[h]
            expo = [None] * (tk // blk)
            for b in reversed(range(tk // blk)):
                cols = slice(b * blk, (b + 1) * blk)
                later = _dot(sp16[:, cols], tri_ref[...])
                expo[b] = own[:, cols] - later - jnp.concatenate([c] * (blk // LANES), axis=1)
                c = c + jnp.sum(sp[:, cols], axis=-1, keepdims=True)
            carry_ref[h] = c
            a = jnp.exp2(jnp.concatenate(expo, axis=1))
            if masked:
                a = jnp.where(valid, a, 0.0)
            acc_ref[h] += _dot(a.astype(BF16), v)

    def process(chunks):
        z_all = [logits(j) for j, _ in chunks]
        for (j, masked), z_pair in zip(chunks, z_all):
            weights_pv(j, z_pair, masked)

    def min_carry():
        return jnp.min(jnp.minimum(carry_ref[0], carry_ref[1]))

    def more(state):
        j, cmin = state
        return jnp.logical_and(j >= 0, cmin < SB_SKIP)

    def step(state):
        j, _ = state
        process([(j, False)])
        return j - 1, min_carry()

    process([(qi, True)])
    lax.while_loop(more, step, (qi - 1, min_carry()))

    out_lane = lax.broadcasted_iota(jnp.int32, (tq, LANES), 1)
    o_ref[...] = jnp.where(out_lane < SB_HEAD_DIM, acc_ref[0], acc_ref[1]).astype(o_ref.dtype)


def _sb_attention(qb, kb, vb, tri, batch, seq):
    t = qb.shape[0]
    tq = SB_TQ
    nq = seq // tq
    return pl.pallas_call(
        _sb_kernel,
        grid=(batch, HEAD_PAIRS, nq),
        in_specs=[pl.BlockSpec((tq, LANES), lambda b, p, i: (b * nq + i, p)),
                  pl.BlockSpec((seq, LANES), lambda b, p, i: (b, p)),
                  pl.BlockSpec((seq, LANES), lambda b, p, i: (b, p)),
                  _const_spec(tri.shape)],
        out_specs=pl.BlockSpec((tq, LANES), lambda b, p, i: (b * nq + i, p)),
        out_shape=jax.ShapeDtypeStruct((t, HEAD_PAIRS * LANES), BF16),
        scratch_shapes=[pltpu.VMEM((2, tq, LANES), BF16), pltpu.VMEM((2, tq, LANES), F32),
                        pltpu.VMEM((2, tq, LANES), F32)],
        compiler_params=_params(("parallel", "parallel", "parallel")),
        name="sb_attention",
    )(qb, kb, vb, tri)


def _swa_kernel(sink_ref, q_ref, k_ref, v_ref, kp_ref, vp_ref, o_ref, *, layer, tiles_per_seq):
    i = pl.program_id(0)
    w, tq = SWA_WINDOW, SWA_TQ
    has_prev = (i % tiles_per_seq) != 0
    kcat = jnp.concatenate([kp_ref[...], k_ref[...]], axis=0)
    vcat = jnp.concatenate([vp_ref[...], v_ref[...]], axis=0)
    row = lax.broadcasted_iota(jnp.int32, (w, 2 * w), 0)
    col = lax.broadcasted_iota(jnp.int32, (w, 2 * w), 1)
    band = (col > row) & (col <= row + w)
    band_first = band & ((col >= w) | has_prev)
    first_half = lax.broadcasted_iota(jnp.int32, (w, LANES), 1) < SWA_HEAD_DIM
    for p in range(HEAD_PAIRS):
        g = p // (HEAD_PAIRS // SWA_KV_HEADS)
        q2 = q_ref[:, p * LANES:(p + 1) * LANES]
        kg = kcat[:, g * LANES:(g + 1) * LANES]
        half = lax.broadcasted_iota(jnp.int32, q2.shape, 1) < SWA_HEAD_DIM
        zero = jnp.zeros_like(q2)
        qh = [jnp.where(half, q2, zero), jnp.where(half, zero, q2)]
        s_full = [_dot_nt(qh[hh], kg) for hh in range(2)]
        for r in range(tq // w):
            keys = slice(r * w, (r + 2) * w)
            acc, esink = [], []
            for hh in range(2):
                sink = sink_ref[layer, 2 * p + hh] * LOG2E
                s = jnp.where(band_first if r == 0 else band, s_full[hh][r * w:(r + 1) * w, keys], NEG_BIG)
                m = jnp.maximum(jnp.broadcast_to(jnp.max(s, axis=-1, keepdims=True), (w, LANES)), sink)
                prob = jnp.exp2(s - jnp.concatenate([m, m], axis=1))
                vh = vcat[keys, (2 * g + hh) * LANES:(2 * g + hh + 1) * LANES]
                acc.append(_dot(prob.astype(BF16), vh))
                esink.append(jnp.exp2(sink - m))
            num = jnp.where(first_half, acc[0], acc[1])
            den = (jnp.where(first_half, pltpu.roll(acc[0], SWA_HEAD_DIM, axis=1),
                             pltpu.roll(acc[1], SWA_HEAD_DIM, axis=1))
                   + jnp.where(first_half, esink[0], esink[1]))
            o_ref[r * w:(r + 1) * w, p * LANES:(p + 1) * LANES] = (num / den).astype(o_ref.dtype)


def _swa_attention(sinks, layer, qs, ks, vs, seq):
    t = qs.shape[0]
    tq = SWA_TQ
    per_tile = tq // SWA_WINDOW
    cur = lambda n: pl.BlockSpec((tq, n), lambda i: (i, 0))
    prev = lambda n: pl.BlockSpec((SWA_WINDOW, n), lambda i: (jnp.maximum(i * per_tile - 1, 0), 0))
    return pl.pallas_call(
        functools.partial(_swa_kernel, layer=layer, tiles_per_seq=seq // tq),
        grid=(t // tq,),
        in_specs=[pl.BlockSpec(memory_space=pltpu.SMEM), cur(512), cur(256), cur(512), prev(256), prev(512)],
        out_specs=cur(512),
        out_shape=jax.ShapeDtypeStruct((t, 512), BF16),
        compiler_params=_params(("parallel",)),
        name="swa_attention",
    )(sinks, qs, ks, vs, ks, vs)


def _post_kernel(x_ref, oa_ref, ob_ref, oc_ref, gate_ref, woa_ref, wob_ref, woc_ref, wout_ref,
                 gpost_ref, gpre_ref, wup_ref, wdown_ref, gmlp_ref, out_ref):
    d = D_MODEL
    mixed = (gate_ref[:, 0:d] * _dot(oa_ref[...], woa_ref[...])
             + gate_ref[:, d:2 * d] * _dot(ob_ref[...], wob_ref[...])
             + gate_ref[:, 2 * d:3 * d] * _dot(oc_ref[...], woc_ref[...]))
    x1 = x_ref[...] + _rms(_dot(mixed.astype(BF16), wout_ref[...]), gpost_ref[...])
    h = _rms(x1, gpre_ref[...]).astype(BF16)
    u = jnp.square(jnp.maximum(_dot(h, wup_ref[...]), 0.0)).astype(BF16)
    out_ref[...] = x1 + _rms(_dot(u, wdown_ref[...]), gmlp_ref[...])


def _post(x, oa, ob, oc, gates, layer, woa, wob, woc, wout, g_post, g_pre, wup, wdown, g_mlp):
    t = x.shape[0]
    tm = POST_TM
    row = lambda n: pl.BlockSpec((tm, n), lambda i: (i, 0))
    consts = [woa, wob, woc, wout, g_post, g_pre, wup, wdown, g_mlp]
    return pl.pallas_call(
        _post_kernel,
        grid=(t // tm,),
        in_specs=[row(D_MODEL), row(512), row(512), row(512), row(N_BRANCHES * D_MODEL)]
                 + [_layer_spec(c, layer) for c in consts],
        out_specs=row(D_MODEL),
        out_shape=jax.ShapeDtypeStruct((t, D_MODEL), F32),
        compiler_params=_params(("parallel",)),
        name="post",
    )(x, oa, ob, oc, gates, *consts)


def _rot_cols(w, heads, dim):
    lead = w.shape[:-1]
    w = w.reshape(*lead, heads, dim)
    return jnp.concatenate([-w[..., dim // 2:], w[..., :dim // 2]], axis=-1).reshape(*lead, heads * dim)


def _dup_heads(w, heads, dim):
    lead = w.shape[:-1]
    w = w.reshape(*lead, heads, 1, dim)
    return jnp.broadcast_to(w, (*lead, heads, 2, dim)).reshape(*lead, heads * 2 * dim)


def _pair_pad(a, b):
    z = jnp.zeros_like(a)
    return jnp.concatenate([a, z, z, b], axis=-1).reshape(*a.shape[:-2], -1)


def _layout_weights(w_in, w_uq, w_ukv):
    dep = w_in.shape[0]
    w_in = w_in.astype(BF16)
    sizes = (MLA_Q_LORA, MLA_KV_LORA, MLA_ROPE, 512, 128, 128, 512, 512, 512, N_BRANCHES * D_MODEL)
    pts = []
    acc = 0
    for s in sizes[:-1]:
        acc += s
        pts.append(acc)
    c_q, c_kv, k_rope, q_s, k_s, v_s, q_b, k_b, v_b, gates = jnp.split(w_in, pts, axis=-1)

    def kpe_pad(w):
        z = jnp.zeros((dep, D_MODEL, MLA_NOPE), BF16)
        return jnp.concatenate([z, w, jnp.zeros((dep, D_MODEL, 32), BF16)], axis=-1)

    swa_scale = SWA_HEAD_DIM ** -0.5
    sb_scale = SB_HEAD_DIM ** -0.5
    v_sh = v_s.reshape(dep, D_MODEL, SWA_KV_HEADS, SWA_HEAD_DIM)
    w1 = jnp.concatenate([
        c_q, c_kv, kpe_pad(k_rope), kpe_pad(_rot_cols(k_rope, 1, MLA_ROPE)),
        q_s * swa_scale, _rot_cols(q_s, SWA_HEADS, SWA_HEAD_DIM) * swa_scale,
        _dup_heads(k_s, SWA_KV_HEADS, SWA_HEAD_DIM),
        _dup_heads(_rot_cols(k_s, SWA_KV_HEADS, SWA_HEAD_DIM), SWA_KV_HEADS, SWA_HEAD_DIM),
        _pair_pad(v_sh, v_sh),
        q_b * sb_scale, k_b, v_b, gates], axis=-1)
    assert w1.shape[-1] == W1_COLS and w1.dtype == BF16

    uq = w_uq.astype(BF16).reshape(dep, MLA_Q_LORA, MLA_HEADS, MLA_NOPE + MLA_ROPE)
    nope, pe = uq[..., :MLA_NOPE], uq[..., MLA_NOPE:]
    pe_rot = jnp.concatenate([-pe[..., MLA_ROPE // 2:], pe[..., :MLA_ROPE // 2]], axis=-1)
    zq = jnp.zeros((dep, MLA_Q_LORA, MLA_HEADS, 32), BF16)
    wq_main = jnp.concatenate([nope, pe, zq], axis=-1).reshape(dep, MLA_Q_LORA, -1)
    wq_rot = jnp.concatenate([jnp.zeros_like(nope), pe_rot, zq], axis=-1).reshape(dep, MLA_Q_LORA, -1)
    wq = jnp.concatenate([wq_main, wq_rot], axis=-1)

    ukv = w_ukv.astype(BF16).reshape(dep, MLA_KV_LORA, MLA_HEADS, MLA_NOPE + MLA_V)
    k_nope, v = ukv[..., :MLA_NOPE], ukv[..., MLA_NOPE:]
    wk = jnp.concatenate([k_nope, jnp.zeros_like(k_nope)], axis=-1).reshape(dep, MLA_KV_LORA, -1)
    vp = v.reshape(dep, MLA_KV_LORA, HEAD_PAIRS, 2, MLA_V)
    wkv = jnp.concatenate([wk, _pair_pad(vp[..., 0, :], vp[..., 1, :])], axis=-1)
    return w1, wq, wkv


def kernel(x, positions, g_mix_pre, w_in, b_gate, g_q_lat, g_kv_lat, w_uq, w_ukv, swa_sinks, w_o_mla, w_o_swa, w_o_sb, w_out, g_mix_post, g_mlp_pre, w_up, w_down, g_mlp_post):
    batch, seq, d = x.shape
    t = batch * seq
    tables = _rope_tables(positions)
    w1, wq, wkv = _layout_weights(w_in, w_uq, w_ukv)
    woa, wob, woc, wout, wup, wdown = (w.astype(BF16) for w in (w_o_mla, w_o_swa, w_o_sb, w_out, w_up, w_down))
    tri = (lax.broadcasted_iota(jnp.int32, (SB_BLK, SB_BLK), 0)
           > lax.broadcasted_iota(jnp.int32, (SB_BLK, SB_BLK), 1)).astype(BF16)
    rows = lambda g: g.reshape(DEPTH, 1, -1)
    g_pre, bg, g_q, g_kv = rows(g_mix_pre), rows(b_gate), rows(g_q_lat), rows(g_kv_lat)
    g_post, g_mlp_in, g_mlp_out = rows(g_mix_post), rows(g_mlp_pre), rows(g_mlp_post)

    xt = x.reshape(t, d)
    for l in range(DEPTH):
        qm, km, vm, qs, ks, vs, qb, kb, vb, gates = _prep(xt, tables, l, g_pre, w1, bg, g_q, g_kv, wq, wkv)
        oa = _mla_attention(qm, km, vm, batch, seq)
        ob = _swa_attention(swa_sinks, l, qs, ks, vs, seq)
        oc = _sb_attention(qb, kb, vb, tri, batch, seq)
        xt = _post(xt, oa, ob, oc, gates, l, woa, wob, woc, wout, g_post, g_mlp_in, wup, wdown, g_mlp_out)
    return xt.reshape(batch, seq, d)
```

```python
import functools

import jax
import jax.numpy as jnp
from jax import lax
from jax.experimental import pallas as pl
from jax.experimental.pallas import tpu as pltpu

F32 = jnp.float32
BF16 = jnp.bfloat16

D_MODEL = 1024
DEPTH = 4
MLA_HEADS = 8
MLA_Q_LORA = 256
MLA_KV_LORA = 128
MLA_NOPE = 64
MLA_ROPE = 32
MLA_V = 64
SWA_HEADS = 8
SWA_KV_HEADS = 2
SWA_HEAD_DIM = 64
SWA_WINDOW = 128
SB_HEADS = 8
SB_HEAD_DIM = 64
D_FF = 4 * D_MODEL
ROPE_THETA = 10000.0
EPS = 1e-6
N_BRANCHES = 3

LANES = 128
HEAD_PAIRS = 4
MLA_HEAD_PAD = 128
NEG_BIG = -1e30
LOG2E = 1.4426950408889634
SB_SKIP = 160.0
SP_CLAMP = 126.0
VMEM_LIMIT = 56 * 1024 * 1024

OFF_CQ = 0
OFF_CKV = OFF_CQ + MLA_Q_LORA
OFF_KPE = OFF_CKV + MLA_KV_LORA
OFF_KPE_ROT = OFF_KPE + LANES
OFF_QS = OFF_KPE_ROT + LANES
OFF_QS_ROT = OFF_QS + 512
OFF_KS = OFF_QS_ROT + 512
OFF_KS_ROT = OFF_KS + 256
OFF_VS = OFF_KS_ROT + 256
OFF_QB = OFF_VS + 512
OFF_KB = OFF_QB + 512
OFF_VB = OFF_KB + 512
OFF_GATE = OFF_VB + 512
W1_COLS = OFF_GATE + N_BRANCHES * D_MODEL

PREP_TM = 256
POST_TM = 256
MLA_TQ = 512
MLA_TK = 512
SB_TQ = 256
SB_TK = 256
SB_BLK = 256
SB_PAIRS = 4
SWA_TQ = 256
ROPE_TM = 2048


def _rms(x, g):
    return x * lax.rsqrt(jnp.mean(x * x, axis=-1, keepdims=True) + EPS) * g


def _dot(a, b):
    return jnp.dot(a, b, preferred_element_type=F32)


def _dot_nt(a, b):
    return lax.dot_general(a, b, (((1,), (1,)), ((), ())), preferred_element_type=F32)


def _const_spec(shape):
    return pl.BlockSpec(shape, lambda *_: (0,) * len(shape), pipeline_mode=pl.Buffered(1))


def _layer_spec(arr, layer):
    return pl.BlockSpec((None,) + arr.shape[1:], lambda *_: (layer, 0, 0), pipeline_mode=pl.Buffered(1))


def _params(sem):
    return pltpu.CompilerParams(dimension_semantics=sem, vmem_limit_bytes=VMEM_LIMIT)


def _pair_ones(width):
    lane = lax.broadcasted_iota(jnp.int32, (1, width), 1) % (2 * LANES)
    return ((lane >= 64) & (lane < 2 * LANES - 64)).astype(F32)


def _rope_table_kernel(pos_ref, inva_ref, maska_ref, invb_ref, ca_ref, sa_ref, cb_ref, sb_ref):
    pos = pos_ref[...].astype(F32)
    ang_a = pos * inva_ref[...]
    ca_ref[...] = jnp.cos(ang_a) * maska_ref[...]
    sa_ref[...] = jnp.sin(ang_a)
    ang_b = pos * invb_ref[...]
    cb_ref[...] = jnp.cos(ang_b)
    sb_ref[...] = jnp.sin(ang_b)


def _rope_tables(positions):
    t = positions.size
    pos = positions.reshape(t, 1)
    inv_a16 = 1.0 / (ROPE_THETA ** (jnp.arange(0, MLA_ROPE, 2, dtype=F32) / MLA_ROPE))
    inv_b32 = 1.0 / (ROPE_THETA ** (jnp.arange(0, SWA_HEAD_DIM, 2, dtype=F32) / SWA_HEAD_DIM))
    zeros = jnp.zeros
    inv_a = jnp.concatenate([zeros((MLA_NOPE,), F32), inv_a16, inv_a16, zeros((32,), F32)]).reshape(1, LANES)
    mask_a = jnp.concatenate([jnp.ones((MLA_NOPE + MLA_ROPE,), F32), zeros((32,), F32)]).reshape(1, LANES)
    inv_b = jnp.tile(inv_b32, 4).reshape(1, LANES)
    row = pl.BlockSpec((ROPE_TM, LANES), lambda i: (i, 0))
    vec = pl.BlockSpec((1, LANES), lambda i: (0, 0))
    out = jax.ShapeDtypeStruct((t, LANES), F32)
    return pl.pallas_call(
        _rope_table_kernel,
        grid=(t // ROPE_TM,),
        in_specs=[pl.BlockSpec((ROPE_TM, 1), lambda i: (i, 0)), vec, vec, vec],
        out_specs=[row, row, row, row],
        out_shape=[out, out, out, out],
        compiler_params=_params(("parallel",)),
        name="rope_tables",
    )(pos, inv_a, mask_a, inv_b)


def _prep_kernel(x_ref, g_ref, ca_ref, sa_ref, cb_ref, sb_ref, w1_ref, bg_ref, gq_ref, gkv_ref,
                 wq_ref, wkv_ref,
                 qm_ref, km_ref, vm_ref, qs_ref, ks_ref, vs_ref, qb_ref, kb_ref, vb_ref, gate_ref):
    h = _rms(x_ref[...], g_ref[...]).astype(BF16)

    def mm(lo, n):
        return _dot(h, w1_ref[:, lo:lo + n])

    ca, sa, cb, sb = ca_ref[...], sa_ref[...], cb_ref[...], sb_ref[...]

    cb4 = jnp.concatenate([cb * LOG2E] * 4, axis=1)
    sb4 = jnp.concatenate([sb * LOG2E] * 4, axis=1)
    qs_ref[...] = (mm(OFF_QS, 512) * cb4 + mm(OFF_QS_ROT, 512) * sb4).astype(BF16)
    cb2 = jnp.concatenate([cb] * 2, axis=1)
    sb2 = jnp.concatenate([sb] * 2, axis=1)
    ks_ref[...] = (mm(OFF_KS, 256) * cb2 + mm(OFF_KS_ROT, 256) * sb2).astype(BF16)
    vs_ref[...] = (mm(OFF_VS, 512) + _pair_ones(512)).astype(BF16)

    qb_ref[...] = (mm(OFF_QB, 512) * LOG2E).astype(BF16)
    kb_ref[...] = mm(OFF_KB, 512).astype(BF16)
    vb_ref[...] = mm(OFF_VB, 512).astype(BF16)

    gate_ref[...] = jax.nn.sigmoid(mm(OFF_GATE, N_BRANCHES * D_MODEL) + bg_ref[...])

    scale = (MLA_NOPE + MLA_ROPE) ** -0.5 * LOG2E
    cqn = _rms(mm(OFF_CQ, MLA_Q_LORA), gq_ref[...]).astype(BF16)
    ca8 = jnp.concatenate([ca * scale] * MLA_HEADS, axis=1)
    sa8 = jnp.concatenate([sa * scale] * MLA_HEADS, axis=1)
    nq = MLA_HEADS * MLA_HEAD_PAD
    qm_ref[...] = (_dot(cqn, wq_ref[:, :nq]) * ca8 + _dot(cqn, wq_ref[:, nq:]) * sa8).astype(BF16)
    kpe = mm(OFF_KPE, LANES) * ca + mm(OFF_KPE_ROT, LANES) * sa
    ckvn = _rms(mm(OFF_CKV, MLA_KV_LORA), gkv_ref[...]).astype(BF16)
    kpe8 = jnp.concatenate([kpe] * MLA_HEADS, axis=1)
    km_ref[...] = (_dot(ckvn, wkv_ref[:, :nq]) + kpe8).astype(BF16)
    vm_ref[...] = (_dot(ckvn, wkv_ref[:, nq:]) + _pair_ones(nq)).astype(BF16)


def _prep(x, tables, layer, g_pre, w1, b_gate, g_q, g_kv, wq, wkv):
    t = x.shape[0]
    tm = PREP_TM
    ca, sa, cb, sb = tables

    def row(n):
        return pl.BlockSpec((tm, n), lambda i: (i, 0))

    def out(n, dt=BF16):
        return jax.ShapeDtypeStruct((t, n), dt)

    nq = MLA_HEADS * MLA_HEAD_PAD
    consts = [w1, b_gate, g_q, g_kv, wq, wkv]
    return pl.pallas_call(
        _prep_kernel,
        grid=(t // tm,),
        in_specs=[row(D_MODEL), _layer_spec(g_pre, layer), row(LANES), row(LANES), row(LANES), row(LANES)]
                 + [_layer_spec(c, layer) for c in consts],
        out_specs=[row(nq), row(nq), row(nq), row(512), row(256), row(512), row(512), row(512), row(512),
                   row(N_BRANCHES * D_MODEL)],
        out_shape=[out(nq), out(nq), out(nq), out(512), out(256), out(512), out(512), out(512), out(512),
                   out(N_BRANCHES * D_MODEL, F32)],
        compiler_params=_params(("parallel",)),
        name="prep",
    )(x, g_pre, ca, sa, cb, sb, *consts)


def _mla_kernel(q_ref, k_ref, v_ref, o_ref, m_ref, acc_ref):
    qi = pl.program_id(2)
    tq, tk = MLA_TQ, MLA_TK
    m_ref[...] = jnp.full(m_ref.shape, NEG_BIG, F32)
    acc_ref[...] = jnp.zeros(acc_ref.shape, F32)

    def head_lanes(h):
        return slice(h * MLA_HEAD_PAD, (h + 1) * MLA_HEAD_PAD)

    def scores(j):
        start = pl.multiple_of(j * tk, tk)
        k = k_ref[pl.ds(start, tk), :]
        return tuple(_dot_nt(q_ref[:, head_lanes(h)], k[:, head_lanes(h)]) for h in range(2))

    def softmax_pv(j, s_pair, masked):
        start = pl.multiple_of(j * tk, tk)
        v = v_ref[pl.ds(start, tk), :]
        for h, s in enumerate(s_pair):
            if masked:
                row = lax.broadcasted_iota(jnp.int32, (tq, tk), 0)
                col = lax.broadcasted_iota(jnp.int32, (tq, tk), 1)
                s = jnp.where(col <= row, s, NEG_BIG)
            m_old = m_ref[h]
            m_new = jnp.maximum(m_old, jnp.max(s, axis=-1, keepdims=True))
            alpha = jnp.exp2(m_old - m_new)
            p = jnp.exp2(s - jnp.concatenate([m_new] * (tk // LANES), axis=1))
            acc_ref[h] = alpha * acc_ref[h] + _dot(p.astype(BF16), v[:, head_lanes(h)])
            m_ref[h] = m_new

    def process(chunks):
        s_all = [scores(j) for j, _ in chunks]
        for (j, masked), s_pair in zip(chunks, s_all):
            softmax_pv(j, s_pair, masked)

    def body(i, carry):
        process([(2 * i, False), (2 * i + 1, False)])
        return carry

    lax.fori_loop(0, qi // 2, body, 0)

    @pl.when(qi % 2 == 1)
    def _():
        process([(qi - 1, False), (qi, True)])

    @pl.when(qi % 2 == 0)
    def _():
        process([(qi, True)])

    first_half = lax.broadcasted_iota(jnp.int32, (tq, LANES), 1) < MLA_V
    a0, a1 = acc_ref[0], acc_ref[1]
    num = jnp.where(first_half, a0, a1)
    den = jnp.where(first_half, pltpu.roll(a0, MLA_V, axis=1), pltpu.roll(a1, MLA_V, axis=1))
    o_ref[...] = (num / den).astype(o_ref.dtype)


def _mla_attention(qm, km, vm, batch, seq):
    t = qm.shape[0]
    tq = MLA_TQ
    nq = seq // tq
    return pl.pallas_call(
        _mla_kernel,
        grid=(batch, HEAD_PAIRS, nq),
        in_specs=[pl.BlockSpec((tq, 2 * MLA_HEAD_PAD), lambda b, p, i: (b * nq + i, p)),
                  pl.BlockSpec((seq, 2 * MLA_HEAD_PAD), lambda b, p, i: (b, p)),
                  pl.BlockSpec((seq, 2 * LANES), lambda b, p, i: (b, p))],
        out_specs=pl.BlockSpec((tq, LANES), lambda b, p, i: (b * nq + i, p)),
        out_shape=jax.ShapeDtypeStruct((t, HEAD_PAIRS * LANES), BF16),
        scratch_shapes=[pltpu.VMEM((2, tq, LANES), F32), pltpu.VMEM((2, tq, LANES), F32)],
        compiler_params=_params(("parallel", "parallel", "parallel")),
        name="mla_attention",
    )(qm, km, vm)


def _sb_kernel(q_ref, k_ref, v_ref, tri_ref, o_ref, qh_ref, carry_ref, acc_ref):
    qi = pl.program_id(2)
    tq, tk, blk = SB_TQ, SB_TK, SB_BLK
    carry_ref[...] = jnp.zeros(carry_ref.shape, F32)
    acc_ref[...] = jnp.zeros(acc_ref.shape, F32)
    n_heads = 2 * SB_PAIRS
    first_half = lax.broadcasted_iota(jnp.int32, (tq, LANES), 1) < SB_HEAD_DIM
    for p in range(SB_PAIRS):
        q2 = q_ref[:, p * LANES:(p + 1) * LANES]
        zero = jnp.zeros_like(q2)
        qh_ref[2 * p] = jnp.where(first_half, q2, zero)
        qh_ref[2 * p + 1] = jnp.where(first_half, zero, q2)

    def pair_lanes(h):
        return slice((h // 2) * LANES, (h // 2 + 1) * LANES)

    def logits(j):
        start = pl.multiple_of(j * tk, tk)
        k = k_ref[pl.ds(start, tk), :]
        return tuple(_dot_nt(qh_ref[h], k[:, pair_lanes(h)]) for h in range(n_heads))

    def weights_pv(j, z_all, masked):
        start = pl.multiple_of(j * tk, tk)
        v = v_ref[pl.ds(start, tk), :]
        if masked:
            row = lax.broadcasted_iota(jnp.int32, (tq, tk), 0)
            col = lax.broadcasted_iota(jnp.int32, (tq, tk), 1)
            valid = col < row
        for h, z in enumerate(z_all):
            sp = jnp.maximum(z, jnp.log2(1.0 + jnp.exp2(jnp.minimum(z, SP_CLAMP))))
            own = z - sp
            if masked:
                sp = jnp.where(valid, sp, 0.0)
            sp16 = sp.astype(BF16)
            c = carry_ref[h]
            expo = [None] * (tk // blk)
            for b in reversed(range(tk // blk)):
                cols = slice(b * blk, (b + 1) * blk)
                later = _dot(sp16[:, cols], tri_ref[...])
                expo[b] = own[:, cols] - later - jnp.concatenate([c] * (blk // LANES), axis=1)
                c = c + jnp.sum(sp[:, cols], axis=-1, keepdims=True)
            carry_ref[h] = c
            a = jnp.exp2(jnp.concatenate(expo, axis=1))
            if masked:
                a = jnp.where(valid, a, 0.0)
            acc_ref[h] += _dot(a.astype(BF16), v[:, pair_lanes(h)])

    def process(chunks):
        z_chunks = [logits(j) for j, _ in chunks]
        for (j, masked), z_all in zip(chunks, z_chunks):
            weights_pv(j, z_all, masked)

    def min_carry():
        c = carry_ref[0]
        for h in range(1, n_heads):
            c = jnp.minimum(c, carry_ref[h])
        return jnp.min(c)

    def more(state):
        j, cmin = state
        return jnp.logical_and(j >= 0, cmin < SB_SKIP)

    def step(state):
        j, _ = state
        process([(j, False)])
        return j - 1, min_carry()

    @pl.when(qi == 0)
    def _():
        process([(qi, True)])

    @pl.when(qi > 0)
    def _():
        process([(qi, True), (qi - 1, False)])

    lax.while_loop(more, step, (qi - 2, min_carry()))

    for p in range(SB_PAIRS):
        o_ref[:, p * LANES:(p + 1) * LANES] = jnp.where(
            first_half, acc_ref[2 * p], acc_ref[2 * p + 1]).astype(o_ref.dtype)


def _sb_attention(qb, kb, vb, tri, batch, seq):
    t = qb.shape[0]
    tq = SB_TQ
    nq = seq // tq
    wide = SB_PAIRS * LANES
    return pl.pallas_call(
        _sb_kernel,
        grid=(batch, HEAD_PAIRS // SB_PAIRS, nq),
        in_specs=[pl.BlockSpec((tq, wide), lambda b, p, i: (b * nq + i, p)),
                  pl.BlockSpec((seq, wide), lambda b, p, i: (b, p)),
                  pl.BlockSpec((seq, wide), lambda b, p, i: (b, p)),
                  _const_spec(tri.shape)],
        out_specs=pl.BlockSpec((tq, wide), lambda b, p, i: (b * nq + i, p)),
        out_shape=jax.ShapeDtypeStruct((t, HEAD_PAIRS * LANES), BF16),
        scratch_shapes=[pltpu.VMEM((2 * SB_PAIRS, tq, LANES), BF16), pltpu.VMEM((2 * SB_PAIRS, tq, LANES), F32),
                        pltpu.VMEM((2 * SB_PAIRS, tq, LANES), F32)],
        compiler_params=_params(("parallel", "parallel", "parallel")),
        name="sb_attention",
    )(qb, kb, vb, tri)


def _swa_kernel(sink_ref, q_ref, k_ref, v_ref, kp_ref, vp_ref, o_ref, *, layer, tiles_per_seq):
    i = pl.program_id(0)
    w, tq = SWA_WINDOW, SWA_TQ
    has_prev = (i % tiles_per_seq) != 0
    kcat = jnp.concatenate([kp_ref[...], k_ref[...]], axis=0)
    vcat = jnp.concatenate([vp_ref[...], v_ref[...]], axis=0)
    row = lax.broadcasted_iota(jnp.int32, (w, 2 * w), 0)
    col = lax.broadcasted_iota(jnp.int32, (w, 2 * w), 1)
    band = (col > row) & (col <= row + w)
    band_first = band & ((col >= w) | has_prev)
    first_half = lax.broadcasted_iota(jnp.int32, (w, LANES), 1) < SWA_HEAD_DIM
    for p in range(HEAD_PAIRS):
        g = p // (HEAD_PAIRS // SWA_KV_HEADS)
        q2 = q_ref[:, p * LANES:(p + 1) * LANES]
        kg = kcat[:, g * LANES:(g + 1) * LANES]
        half = lax.broadcasted_iota(jnp.int32, q2.shape, 1) < SWA_HEAD_DIM
        zero = jnp.zeros_like(q2)
        qh = [jnp.where(half, q2, zero), jnp.where(half, zero, q2)]
        s_full = [_dot_nt(qh[hh], kg) for hh in range(2)]
        for r in range(tq // w):
            keys = slice(r * w, (r + 2) * w)
            acc, esink = [], []
            for hh in range(2):
                sink = sink_ref[layer, 2 * p + hh] * LOG2E
                s = jnp.where(band_first if r == 0 else band, s_full[hh][r * w:(r + 1) * w, keys], NEG_BIG)
                m = jnp.maximum(jnp.broadcast_to(jnp.max(s, axis=-1, keepdims=True), (w, LANES)), sink)
                prob = jnp.exp2(s - jnp.concatenate([m, m], axis=1))
                vh = vcat[keys, (2 * g + hh) * LANES:(2 * g + hh + 1) * LANES]
                acc.append(_dot(prob.astype(BF16), vh))
                esink.append(jnp.exp2(sink - m))
            num = jnp.where(first_half, acc[0], acc[1])
            den = (jnp.where(first_half, pltpu.roll(acc[0], SWA_HEAD_DIM, axis=1),
                             pltpu.roll(acc[1], SWA_HEAD_DIM, axis=1))
                   + jnp.where(first_half, esink[0], esink[1]))
            o_ref[r * w:(r + 1) * w, p * LANES:(p + 1) * LANES] = (num / den).astype(o_ref.dtype)


def _swa_attention(sinks, layer, qs, ks, vs, seq):
    t = qs.shape[0]
    tq = SWA_TQ
    per_tile = tq // SWA_WINDOW
    cur = lambda n: pl.BlockSpec((tq, n), lambda i: (i, 0))
    prev = lambda n: pl.BlockSpec((SWA_WINDOW, n), lambda i: (jnp.maximum(i * per_tile - 1, 0), 0))
    return pl.pallas_call(
        functools.partial(_swa_kernel, layer=layer, tiles_per_seq=seq // tq),
        grid=(t // tq,),
        in_specs=[pl.BlockSpec(memory_space=pltpu.SMEM), cur(512), cur(256), cur(512), prev(256), prev(512)],
        out_specs=cur(512),
        out_shape=jax.ShapeDtypeStruct((t, 512), BF16),
        compiler_params=_params(("parallel",)),
        name="swa_attention",
    )(sinks, qs, ks, vs, ks, vs)


def _post_kernel(x_ref, oa_ref, ob_ref, oc_ref, gate_ref, woa_ref, wob_ref, woc_ref, wout_ref,
                 gpost_ref, gpre_ref, wup_ref, wdown_ref, gmlp_ref, out_ref):
    d = D_MODEL
    mixed = (gate_ref[:, 0:d] * _dot(oa_ref[...], woa_ref[...])
             + gate_ref[:, d:2 * d] * _dot(ob_ref[...], wob_ref[...])
             + gate_ref[:, 2 * d:3 * d] * _dot(oc_ref[...], woc_ref[...]))
    x1 = x_ref[...] + _rms(_dot(mixed.astype(BF16), wout_ref[...]), gpost_ref[...])
    h = _rms(x1, gpre_ref[...]).astype(BF16)
    u = jnp.square(jnp.maximum(_dot(h, wup_ref[...]), 0.0)).astype(BF16)
    out_ref[...] = x1 + _rms(_dot(u, wdown_ref[...]), gmlp_ref[...])


def _post(x, oa, ob, oc, gates, layer, woa, wob, woc, wout, g_post, g_pre, wup, wdown, g_mlp):
    t = x.shape[0]
    tm = POST_TM
    row = lambda n: pl.BlockSpec((tm, n), lambda i: (i, 0))
    consts = [woa, wob, woc, wout, g_post, g_pre, wup, wdown, g_mlp]
    return pl.pallas_call(
        _post_kernel,
        grid=(t // tm,),
        in_specs=[row(D_MODEL), row(512), row(512), row(512), row(N_BRANCHES * D_MODEL)]
                 + [_layer_spec(c, layer) for c in consts],
        out_specs=row(D_MODEL),
        out_shape=jax.ShapeDtypeStruct((t, D_MODEL), F32),
        compiler_params=_params(("parallel",)),
        name="post",
    )(x, oa, ob, oc, gates, *consts)


def _rot_cols(w, heads, dim):
    lead = w.shape[:-1]
    w = w.reshape(*lead, heads, dim)
    return jnp.concatenate([-w[..., dim // 2:], w[..., :dim // 2]], axis=-1).reshape(*lead, heads * dim)


def _dup_heads(w, heads, dim):
    lead = w.shape[:-1]
    w = w.reshape(*lead, heads, 1, dim)
    return jnp.broadcast_to(w, (*lead, heads, 2, dim)).reshape(*lead, heads * 2 * dim)


def _pair_pad(a, b):
    z = jnp.zeros_like(a)
    return jnp.concatenate([a, z, z, b], axis=-1).reshape(*a.shape[:-2], -1)


def _layout_weights(w_in, w_uq, w_ukv):
    dep = w_in.shape[0]
    w_in = w_in.astype(BF16)
    sizes = (MLA_Q_LORA, MLA_KV_LORA, MLA_ROPE, 512, 128, 128, 512, 512, 512, N_BRANCHES * D_MODEL)
    pts = []
    acc = 0
    for s in sizes[:-1]:
        acc += s
        pts.append(acc)
    c_q, c_kv, k_rope, q_s, k_s, v_s, q_b, k_b, v_b, gates = jnp.split(w_in, pts, axis=-1)

    def kpe_pad(w):
        z = jnp.zeros((dep, D_MODEL, MLA_NOPE), BF16)
        return jnp.concatenate([z, w, jnp.zeros((dep, D_MODEL, 32), BF16)], axis=-1)

    swa_scale = SWA_HEAD_DIM ** -0.5
    sb_scale = SB_HEAD_DIM ** -0.5
    v_sh = v_s.reshape(dep, D_MODEL, SWA_KV_HEADS, SWA_HEAD_DIM)
    w1 = jnp.concatenate([
        c_q, c_kv, kpe_pad(k_rope), kpe_pad(_rot_cols(k_rope, 1, MLA_ROPE)),
        q_s * swa_scale, _rot_cols(q_s, SWA_HEADS, SWA_HEAD_DIM) * swa_scale,
        _dup_heads(k_s, SWA_KV_HEADS, SWA_HEAD_DIM),
        _dup_heads(_rot_cols(k_s, SWA_KV_HEADS, SWA_HEAD_DIM), SWA_KV_HEADS, SWA_HEAD_DIM),
        _pair_pad(v_sh, v_sh),
        q_b * sb_scale, k_b, v_b, gates], axis=-1)
    assert w1.shape[-1] == W1_COLS and w1.dtype == BF16

    uq = w_uq.astype(BF16).reshape(dep, MLA_Q_LORA, MLA_HEADS, MLA_NOPE + MLA_ROPE)
    nope, pe = uq[..., :MLA_NOPE], uq[..., MLA_NOPE:]
    pe_rot = jnp.concatenate([-pe[..., MLA_ROPE // 2:], pe[..., :MLA_ROPE // 2]], axis=-1)
    zq = jnp.zeros((dep, MLA_Q_LORA, MLA_HEADS, 32), BF16)
    wq_main = jnp.concatenate([nope, pe, zq], axis=-1).reshape(dep, MLA_Q_LORA, -1)
    wq_rot = jnp.concatenate([jnp.zeros_like(nope), pe_rot, zq], axis=-1).reshape(dep, MLA_Q_LORA, -1)
    wq = jnp.concatenate([wq_main, wq_rot], axis=-1)

    ukv = w_ukv.astype(BF16).reshape(dep, MLA_KV_LORA, MLA_HEADS, MLA_NOPE + MLA_V)
    k_nope, v = ukv[..., :MLA_NOPE], ukv[..., MLA_NOPE:]
    wk = jnp.concatenate([k_nope, jnp.zeros_like(k_nope)], axis=-1).reshape(dep, MLA_KV_LORA, -1)
    vp = v.reshape(dep, MLA_KV_LORA, HEAD_PAIRS, 2, MLA_V)
    wkv = jnp.concatenate([wk, _pair_pad(vp[..., 0, :], vp[..., 1, :])], axis=-1)
    return w1, wq, wkv


def kernel(x, positions, g_mix_pre, w_in, b_gate, g_q_lat, g_kv_lat, w_uq, w_ukv, swa_sinks, w_o_mla, w_o_swa, w_o_sb, w_out, g_mix_post, g_mlp_pre, w_up, w_down, g_mlp_post):
    batch, seq, d = x.shape
    t = batch * seq
    tables = _rope_tables(positions)
    w1, wq, wkv = _layout_weights(w_in, w_uq, w_ukv)
    woa, wob, woc, wout, wup, wdown = (w.astype(BF16) for w in (w_o_mla, w_o_swa, w_o_sb, w_out, w_up, w_down))
    tri = (lax.broadcasted_iota(jnp.int32, (SB_BLK, SB_BLK), 0)
           > lax.broadcasted_iota(jnp.int32, (SB_BLK, SB_BLK), 1)).astype(BF16)
    rows = lambda g: g.reshape(DEPTH, 1, -1)
    g_pre, bg, g_q, g_kv = rows(g_mix_pre), rows(b_gate), rows(g_q_lat), rows(g_kv_lat)
    g_post, g_mlp_in, g_mlp_out = rows(g_mix_post), rows(g_mlp_pre), rows(g_mlp_post)

    xt = x.reshape(t, d)
    for l in range(DEPTH):
        qm, km, vm, qs, ks, vs, qb, kb, vb, gates = _prep(xt, tables, l, g_pre, w1, bg, g_q, g_kv, wq, wkv)
        oa = _mla_attention(qm, km, vm, batch, seq)
        ob = _swa_attention(swa_sinks, l, qs, ks, vs, seq)
        oc = _sb_attention(qb, kb, vb, tri, batch, seq)
        xt = _post(xt, oa, ob, oc, gates, l, woa, wob, woc, wout, g_post, g_mlp_in, wup, wdown, g_mlp_out)
    return xt.reshape(batch, seq, d)
```

```python
import functools

import jax
import jax.numpy as jnp
from jax import lax
from jax.experimental import pallas as pl
from jax.experimental.pallas import tpu as pltpu

F32 = jnp.float32
BF16 = jnp.bfloat16

D_MODEL = 1024
DEPTH = 4
MLA_HEADS = 8
MLA_Q_LORA = 256
MLA_KV_LORA = 128
MLA_NOPE = 64
MLA_ROPE = 32
MLA_V = 64
SWA_HEADS = 8
SWA_KV_HEADS = 2
SWA_HEAD_DIM = 64
SWA_WINDOW = 128
SB_HEADS = 8
SB_HEAD_DIM = 64
D_FF = 4 * D_MODEL
ROPE_THETA = 10000.0
EPS = 1e-6
N_BRANCHES = 3

LANES = 128
HEAD_PAIRS = 4
MLA_HEAD_PAD = 128
NEG_BIG = -1e30
LOG2E = 1.4426950408889634
SB_SKIP = 160.0
SP_CLAMP = 126.0
VMEM_LIMIT = 56 * 1024 * 1024

OFF_CQ = 0
OFF_CKV = OFF_CQ + MLA_Q_LORA
OFF_KPE = OFF_CKV + MLA_KV_LORA
OFF_KPE_ROT = OFF_KPE + LANES
OFF_QS = OFF_KPE_ROT + LANES
OFF_QS_ROT = OFF_QS + 512
OFF_KS = OFF_QS_ROT + 512
OFF_KS_ROT = OFF_KS + 256
OFF_VS = OFF_KS_ROT + 256
OFF_QB = OFF_VS + 512
OFF_KB = OFF_QB + 512
OFF_VB = OFF_KB + 512
OFF_GATE = OFF_VB + 512
W1_COLS = OFF_GATE + N_BRANCHES * D_MODEL

PREP_TM = 256
POST_TM = 256
MLA_TQ = 512
MLA_TK = 512
MLA_PAIRS = 2
SB_TQ = 256
SB_TK = 256
SB_BLK = 256
SB_PAIRS = 4
SWA_TQ = 256
ROPE_TM = 2048


def _rms(x, g):
    return x * lax.rsqrt(jnp.mean(x * x, axis=-1, keepdims=True) + EPS) * g


def _dot(a, b):
    return jnp.dot(a, b, preferred_element_type=F32)


def _dot_nt(a, b):
    return lax.dot_general(a, b, (((1,), (1,)), ((), ())), preferred_element_type=F32)


def _const_spec(shape):
    return pl.BlockSpec(shape, lambda *_: (0,) * len(shape), pipeline_mode=pl.Buffered(1))


def _layer_spec(arr, layer):
    return pl.BlockSpec((None,) + arr.shape[1:], lambda *_: (layer, 0, 0), pipeline_mode=pl.Buffered(1))


def _params(sem):
    return pltpu.CompilerParams(dimension_semantics=sem, vmem_limit_bytes=VMEM_LIMIT)


def _pair_ones(width):
    lane = lax.broadcasted_iota(jnp.int32, (1, width), 1) % (2 * LANES)
    return ((lane >= 64) & (lane < 2 * LANES - 64)).astype(F32)


def _rope_table_kernel(pos_ref, inva_ref, maska_ref, invb_ref, ca_ref, sa_ref, cb_ref, sb_ref):
    pos = pos_ref[...].astype(F32)
    ang_a = pos * inva_ref[...]
    ca_ref[...] = jnp.cos(ang_a) * maska_ref[...]
    sa_ref[...] = jnp.sin(ang_a)
    ang_b = pos * invb_ref[...]
    cb_ref[...] = jnp.cos(ang_b)
    sb_ref[...] = jnp.sin(ang_b)


def _rope_tables(positions):
    t = positions.size
    pos = positions.reshape(t, 1)
    inv_a16 = 1.0 / (ROPE_THETA ** (jnp.arange(0, MLA_ROPE, 2, dtype=F32) / MLA_ROPE))
    inv_b32 = 1.0 / (ROPE_THETA ** (jnp.arange(0, SWA_HEAD_DIM, 2, dtype=F32) / SWA_HEAD_DIM))
    zeros = jnp.zeros
    inv_a = jnp.concatenate([zeros((MLA_NOPE,), F32), inv_a16, inv_a16, zeros((32,), F32)]).reshape(1, LANES)
    mask_a = jnp.concatenate([jnp.ones((MLA_NOPE + MLA_ROPE,), F32), zeros((32,), F32)]).reshape(1, LANES)
    inv_b = jnp.tile(inv_b32, 4).reshape(1, LANES)
    row = pl.BlockSpec((ROPE_TM, LANES), lambda i: (i, 0))
    vec = pl.BlockSpec((1, LANES), lambda i: (0, 0))
    out = jax.ShapeDtypeStruct((t, LANES), F32)
    return pl.pallas_call(
        _rope_table_kernel,
        grid=(t // ROPE_TM,),
        in_specs=[pl.BlockSpec((ROPE_TM, 1), lambda i: (i, 0)), vec, vec, vec],
        out_specs=[row, row, row, row],
        out_shape=[out, out, out, out],
        compiler_params=_params(("parallel",)),
        name="rope_tables",
    )(pos, inv_a, mask_a, inv_b)


def _prep_kernel(x_ref, g_ref, ca_ref, sa_ref, cb_ref, sb_ref, w1_ref, bg_ref, gq_ref, gkv_ref,
                 wq_ref, wkv_ref,
                 qm_ref, km_ref, vm_ref, qs_ref, ks_ref, vs_ref, qb_ref, kb_ref, vb_ref, gate_ref):
    h = _rms(x_ref[...], g_ref[...]).astype(BF16)

    def mm(lo, n):
        return _dot(h, w1_ref[:, lo:lo + n])

    ca, sa, cb, sb = ca_ref[...], sa_ref[...], cb_ref[...], sb_ref[...]

    cb4 = jnp.concatenate([cb * LOG2E] * 4, axis=1)
    sb4 = jnp.concatenate([sb * LOG2E] * 4, axis=1)
    qs_ref[...] = (mm(OFF_QS, 512) * cb4 + mm(OFF_QS_ROT, 512) * sb4).astype(BF16)
    cb2 = jnp.concatenate([cb] * 2, axis=1)
    sb2 = jnp.concatenate([sb] * 2, axis=1)
    ks_ref[...] = (mm(OFF_KS, 256) * cb2 + mm(OFF_KS_ROT, 256) * sb2).astype(BF16)
    vs_ref[...] = (mm(OFF_VS, 512) + _pair_ones(512)).astype(BF16)

    qb_ref[...] = (mm(OFF_QB, 512) * LOG2E).astype(BF16)
    kb_ref[...] = mm(OFF_KB, 512).astype(BF16)
    vb_ref[...] = mm(OFF_VB, 512).astype(BF16)

    gate_ref[...] = jax.nn.sigmoid(mm(OFF_GATE, N_BRANCHES * D_MODEL) + bg_ref[...])

    scale = (MLA_NOPE + MLA_ROPE) ** -0.5 * LOG2E
    cqn = _rms(mm(OFF_CQ, MLA_Q_LORA), gq_ref[...]).astype(BF16)
    ca8 = jnp.concatenate([ca * scale] * MLA_HEADS, axis=1)
    sa8 = jnp.concatenate([sa * scale] * MLA_HEADS, axis=1)
    nq = MLA_HEADS * MLA_HEAD_PAD
    qm_ref[...] = (_dot(cqn, wq_ref[:, :nq]) * ca8 + _dot(cqn, wq_ref[:, nq:]) * sa8).astype(BF16)
    kpe = mm(OFF_KPE, LANES) * ca + mm(OFF_KPE_ROT, LANES) * sa
    ckvn = _rms(mm(OFF_CKV, MLA_KV_LORA), gkv_ref[...]).astype(BF16)
    kpe8 = jnp.concatenate([kpe] * MLA_HEADS, axis=1)
    km_ref[...] = (_dot(ckvn, wkv_ref[:, :nq]) + kpe8).astype(BF16)
    vm_ref[...] = (_dot(ckvn, wkv_ref[:, nq:]) + _pair_ones(nq)).astype(BF16)


def _prep(x, tables, layer, g_pre, w1, b_gate, g_q, g_kv, wq, wkv):
    t = x.shape[0]
    tm = PREP_TM
    ca, sa, cb, sb = tables

    def row(n):
        return pl.BlockSpec((tm, n), lambda i: (i, 0))

    def out(n, dt=BF16):
        return jax.ShapeDtypeStruct((t, n), dt)

    nq = MLA_HEADS * MLA_HEAD_PAD
    consts = [w1, b_gate, g_q, g_kv, wq, wkv]
    return pl.pallas_call(
        _prep_kernel,
        grid=(t // tm,),
        in_specs=[row(D_MODEL), _layer_spec(g_pre, layer), row(LANES), row(LANES), row(LANES), row(LANES)]
                 + [_layer_spec(c, layer) for c in consts],
        out_specs=[row(nq), row(nq), row(nq), row(512), row(256), row(512), row(512), row(512), row(512),
                   row(N_BRANCHES * D_MODEL)],
        out_shape=[out(nq), out(nq), out(nq), out(512), out(256), out(512), out(512), out(512), out(512),
                   out(N_BRANCHES * D_MODEL, F32)],
        compiler_params=_params(("parallel",)),
        name="prep",
    )(x, g_pre, ca, sa, cb, sb, *consts)


def _mla_kernel(q_ref, k_ref, v_ref, o_ref, m_ref, acc_ref):
    qi = pl.program_id(2)
    tq, tk = MLA_TQ, MLA_TK
    n_heads = 2 * MLA_PAIRS
    m_ref[...] = jnp.full(m_ref.shape, NEG_BIG, F32)
    acc_ref[...] = jnp.zeros(acc_ref.shape, F32)

    def head_lanes(h):
        return slice(h * MLA_HEAD_PAD, (h + 1) * MLA_HEAD_PAD)

    def scores(j):
        start = pl.multiple_of(j * tk, tk)
        k = k_ref[pl.ds(start, tk), :]
        return tuple(_dot_nt(q_ref[:, head_lanes(h)], k[:, head_lanes(h)]) for h in range(n_heads))

    def process(chunks):
        s_all = [scores(j) for j, _ in chunks]
        width = len(chunks) * tk
        start = pl.multiple_of(chunks[0][0] * tk, tk)
        v = v_ref[pl.ds(start, width), :]
        for h in range(n_heads):
            parts = []
            for (_, masked), s_pair in zip(chunks, s_all):
                s = s_pair[h]
                if masked:
                    row = lax.broadcasted_iota(jnp.int32, (tq, tk), 0)
                    col = lax.broadcasted_iota(jnp.int32, (tq, tk), 1)
                    s = jnp.where(col <= row, s, NEG_BIG)
                parts.append(s)
            s = parts[0] if len(parts) == 1 else jnp.concatenate(parts, axis=1)
            m_old = m_ref[h]
            m_new = jnp.maximum(m_old, jnp.max(s, axis=-1, keepdims=True))
            alpha = jnp.exp2(m_old - m_new)
            p = jnp.exp2(s - jnp.concatenate([m_new] * (width // LANES), axis=1))
            acc_ref[h] = alpha * acc_ref[h] + _dot(p.astype(BF16), v[:, head_lanes(h)])
            m_ref[h] = m_new

    def body(i, carry):
        process([(2 * i, False), (2 * i + 1, False)])
        return carry

    lax.fori_loop(0, qi // 2, body, 0)

    @pl.when(qi % 2 == 1)
    def _():
        process([(qi - 1, False), (qi, True)])

    @pl.when(qi % 2 == 0)
    def _():
        process([(qi, True)])

    first_half = lax.broadcasted_iota(jnp.int32, (tq, LANES), 1) < MLA_V
    for p in range(MLA_PAIRS):
        a0, a1 = acc_ref[2 * p], acc_ref[2 * p + 1]
        num = jnp.where(first_half, a0, a1)
        den = jnp.where(first_half, pltpu.roll(a0, MLA_V, axis=1), pltpu.roll(a1, MLA_V, axis=1))
        o_ref[:, p * LANES:(p + 1) * LANES] = (num / den).astype(o_ref.dtype)


def _mla_attention(qm, km, vm, batch, seq):
    t = qm.shape[0]
    tq = MLA_TQ
    nq = seq // tq
    wide = MLA_PAIRS * 2 * MLA_HEAD_PAD
    return pl.pallas_call(
        _mla_kernel,
        grid=(batch, HEAD_PAIRS // MLA_PAIRS, nq),
        in_specs=[pl.BlockSpec((tq, wide), lambda b, p, i: (b * nq + i, p)),
                  pl.BlockSpec((seq, wide), lambda b, p, i: (b, p)),
                  pl.BlockSpec((seq, wide), lambda b, p, i: (b, p))],
        out_specs=pl.BlockSpec((tq, MLA_PAIRS * LANES), lambda b, p, i: (b * nq + i, p)),
        out_shape=jax.ShapeDtypeStruct((t, HEAD_PAIRS * LANES), BF16),
        scratch_shapes=[pltpu.VMEM((2 * MLA_PAIRS, tq, LANES), F32),
                        pltpu.VMEM((2 * MLA_PAIRS, tq, LANES), F32)],
        compiler_params=_params(("parallel", "parallel", "parallel")),
        name="mla_attention",
    )(qm, km, vm)


def _sb_kernel(q_ref, k_ref, v_ref, tri_ref, o_ref, qh_ref, carry_ref, acc_ref):
    qi = pl.program_id(2)
    tq, tk, blk = SB_TQ, SB_TK, SB_BLK
    carry_ref[...] = jnp.zeros(carry_ref.shape, F32)
    acc_ref[...] = jnp.zeros(acc_ref.shape, F32)
    n_heads = 2 * SB_PAIRS
    first_half = lax.broadcasted_iota(jnp.int32, (tq, LANES), 1) < SB_HEAD_DIM
    for p in range(SB_PAIRS):
        q2 = q_ref[:, p * LANES:(p + 1) * LANES]
        zero = jnp.zeros_like(q2)
        qh_ref[2 * p] = jnp.where(first_half, q2, zero)
        qh_ref[2 * p + 1] = jnp.where(first_half, zero, q2)

    def pair_lanes(h):
        return slice((h // 2) * LANES, (h // 2 + 1) * LANES)

    def logits(j):
        start = pl.multiple_of(j * tk, tk)
        k = k_ref[pl.ds(start, tk), :]
        return tuple(_dot_nt(qh_ref[h], k[:, pair_lanes(h)]) for h in range(n_heads))

    def weights_pv(j, z_all, masked):
        start = pl.multiple_of(j * tk, tk)
        v = v_ref[pl.ds(start, tk), :]
        if masked:
            row = lax.broadcasted_iota(jnp.int32, (tq, tk), 0)
            col = lax.broadcasted_iota(jnp.int32, (tq, tk), 1)
            valid = col < row
        for h, z in enumerate(z_all):
            sp = jnp.maximum(z, jnp.log2(1.0 + jnp.exp2(jnp.minimum(z, SP_CLAMP))))
            own = z - sp
            if masked:
                sp = jnp.where(valid, sp, 0.0)
            sp16 = sp.astype(BF16)
            c = carry_ref[h]
            expo = [None] * (tk // blk)
            for b in reversed(range(tk // blk)):
                cols = slice(b * blk, (b + 1) * blk)
                later = _dot(sp16[:, cols], tri_ref[...])
                expo[b] = own[:, cols] - later - jnp.concatenate([c] * (blk // LANES), axis=1)
                c = c + jnp.sum(sp[:, cols], axis=-1, keepdims=True)
            carry_ref[h] = c
            a = jnp.exp2(jnp.concatenate(expo, axis=1))
            if masked:
                a = jnp.where(valid, a, 0.0)
            acc_ref[h] += _dot(a.astype(BF16), v[:, pair_lanes(h)])

    def process(chunks):
        z_chunks = [logits(j) for j, _ in chunks]
        for (j, masked), z_all in zip(chunks, z_chunks):
            weights_pv(j, z_all, masked)

    def min_carry():
        c = carry_ref[0]
        for h in range(1, n_heads):
            c = jnp.minimum(c, carry_ref[h])
        return jnp.min(c)

    def more(state):
        j, cmin = state
        return jnp.logical_and(j >= 0, cmin < SB_SKIP)

    def step(state):
        j, _ = state
        process([(j, False)])
        return j - 1, min_carry()

    @pl.when(qi == 0)
    def _():
        process([(qi, True)])

    @pl.when(qi > 0)
    def _():
        process([(qi, True), (qi - 1, False)])

    lax.while_loop(more, step, (qi - 2, min_carry()))

    for p in range(SB_PAIRS):
        o_ref[:, p * LANES:(p + 1) * LANES] = jnp.where(
            first_half, acc_ref[2 * p], acc_ref[2 * p + 1]).astype(o_ref.dtype)


def _sb_attention(qb, kb, vb, tri, batch, seq):
    t = qb.shape[0]
    tq = SB_TQ
    nq = seq // tq
    wide = SB_PAIRS * LANES
    return pl.pallas_call(
        _sb_kernel,
        grid=(batch, HEAD_PAIRS // SB_PAIRS, nq),
        in_specs=[pl.BlockSpec((tq, wide), lambda b, p, i: (b * nq + i, p)),
                  pl.BlockSpec((seq, wide), lambda b, p, i: (b, p)),
                  pl.BlockSpec((seq, wide), lambda b, p, i: (b, p)),
                  _const_spec(tri.shape)],
        out_specs=pl.BlockSpec((tq, wide), lambda b, p, i: (b * nq + i, p)),
        out_shape=jax.ShapeDtypeStruct((t, HEAD_PAIRS * LANES), BF16),
        scratch_shapes=[pltpu.VMEM((2 * SB_PAIRS, tq, LANES), BF16), pltpu.VMEM((2 * SB_PAIRS, tq, LANES), F32),
                        pltpu.VMEM((2 * SB_PAIRS, tq, LANES), F32)],
        compiler_params=_params(("parallel", "parallel", "parallel")),
        name="sb_attention",
    )(qb, kb, vb, tri)


def _swa_kernel(sink_ref, q_ref, k_ref, v_ref, kp_ref, vp_ref, o_ref, *, layer, tiles_per_seq):
    i = pl.program_id(0)
    w, tq = SWA_WINDOW, SWA_TQ
    has_prev = (i % tiles_per_seq) != 0
    kcat = jnp.concatenate([kp_ref[...], k_ref[...]], axis=0)
    vcat = jnp.concatenate([vp_ref[...], v_ref[...]], axis=0)
    row = lax.broadcasted_iota(jnp.int32, (w, 2 * w), 0)
    col = lax.broadcasted_iota(jnp.int32, (w, 2 * w), 1)
    band = (col > row) & (col <= row + w)
    band_first = band & ((col >= w) | has_prev)
    first_half = lax.broadcasted_iota(jnp.int32, (w, LANES), 1) < SWA_HEAD_DIM
    for p in range(HEAD_PAIRS):
        g = p // (HEAD_PAIRS // SWA_KV_HEADS)
        q2 = q_ref[:, p * LANES:(p + 1) * LANES]
        kg = kcat[:, g * LANES:(g + 1) * LANES]
        half = lax.broadcasted_iota(jnp.int32, q2.shape, 1) < SWA_HEAD_DIM
        zero = jnp.zeros_like(q2)
        qh = [jnp.where(half, q2, zero), jnp.where(half, zero, q2)]
        s_full = [_dot_nt(qh[hh], kg) for hh in range(2)]
        for r in range(tq // w):
            keys = slice(r * w, (r + 2) * w)
            acc, esink = [], []
            for hh in range(2):
                sink = sink_ref[layer, 2 * p + hh] * LOG2E
                s = jnp.where(band_first if r == 0 else band, s_full[hh][r * w:(r + 1) * w, keys], NEG_BIG)
                m = jnp.maximum(jnp.broadcast_to(jnp.max(s, axis=-1, keepdims=True), (w, LANES)), sink)
                prob = jnp.exp2(s - jnp.concatenate([m, m], axis=1))
                vh = vcat[keys, (2 * g + hh) * LANES:(2 * g + hh + 1) * LANES]
                acc.append(_dot(prob.astype(BF16), vh))
                esink.append(jnp.exp2(sink - m))
            num = jnp.where(first_half, acc[0], acc[1])
            den = (jnp.where(first_half, pltpu.roll(acc[0], SWA_HEAD_DIM, axis=1),
                             pltpu.roll(acc[1], SWA_HEAD_DIM, axis=1))
                   + jnp.where(first_half, esink[0], esink[1]))
            o_ref[r * w:(r + 1) * w, p * LANES:(p + 1) * LANES] = (num / den).astype(o_ref.dtype)


def _swa_attention(sinks, layer, qs, ks, vs, seq):
    t = qs.shape[0]
    tq = SWA_TQ
    per_tile = tq // SWA_WINDOW
    cur = lambda n: pl.BlockSpec((tq, n), lambda i: (i, 0))
    prev = lambda n: pl.BlockSpec((SWA_WINDOW, n), lambda i: (jnp.maximum(i * per_tile - 1, 0), 0))
    return pl.pallas_call(
        functools.partial(_swa_kernel, layer=layer, tiles_per_seq=seq // tq),
        grid=(t // tq,),
        in_specs=[pl.BlockSpec(memory_space=pltpu.SMEM), cur(512), cur(256), cur(512), prev(256), prev(512)],
        out_specs=cur(512),
        out_shape=jax.ShapeDtypeStruct((t, 512), BF16),
        compiler_params=_params(("parallel",)),
        name="swa_attention",
    )(sinks, qs, ks, vs, ks, vs)


def _post_kernel(x_ref, oa_ref, ob_ref, oc_ref, gate_ref, woa_ref, wob_ref, woc_ref, wout_ref,
                 gpost_ref, gpre_ref, wup_ref, wdown_ref, gmlp_ref, out_ref):
    d = D_MODEL
    mixed = (gate_ref[:, 0:d] * _dot(oa_ref[...], woa_ref[...])
             + gate_ref[:, d:2 * d] * _dot(ob_ref[...], wob_ref[...])
             + gate_ref[:, 2 * d:3 * d] * _dot(oc_ref[...], woc_ref[...]))
    x1 = x_ref[...] + _rms(_dot(mixed.astype(BF16), wout_ref[...]), gpost_ref[...])
    h = _rms(x1, gpre_ref[...]).astype(BF16)
    u = jnp.square(jnp.maximum(_dot(h, wup_ref[...]), 0.0)).astype(BF16)
    out_ref[...] = x1 + _rms(_dot(u, wdown_ref[...]), gmlp_ref[...])


def _post(x, oa, ob, oc, gates, layer, woa, wob, woc, wout, g_post, g_pre, wup, wdown, g_mlp):
    t = x.shape[0]
    tm = POST_TM
    row = lambda n: pl.BlockSpec((tm, n), lambda i: (i, 0))
    consts = [woa, wob, woc, wout, g_post, g_pre, wup, wdown, g_mlp]
    return pl.pallas_call(
        _post_kernel,
        grid=(t // tm,),
        in_specs=[row(D_MODEL), row(512), row(512), row(512), row(N_BRANCHES * D_MODEL)]
                 + [_layer_spec(c, layer) for c in consts],
        out_specs=row(D_MODEL),
        out_shape=jax.ShapeDtypeStruct((t, D_MODEL), F32),
        compiler_params=_params(("parallel",)),
        name="post",
    )(x, oa, ob, oc, gates, *consts)


def _rot_cols(w, heads, dim):
    lead = w.shape[:-1]
    w = w.reshape(*lead, heads, dim)
    return jnp.concatenate([-w[..., dim // 2:], w[..., :dim // 2]], axis=-1).reshape(*lead, heads * dim)


def _dup_heads(w, heads, dim):
    lead = w.shape[:-1]
    w = w.reshape(*lead, heads, 1, dim)
    return jnp.broadcast_to(w, (*lead, heads, 2, dim)).reshape(*lead, heads * 2 * dim)


def _pair_pad(a, b):
    z = jnp.zeros_like(a)
    return jnp.concatenate([a, z, z, b], axis=-1).reshape(*a.shape[:-2], -1)


def _layout_weights(w_in, w_uq, w_ukv):
    dep = w_in.shape[0]
    w_in = w_in.astype(BF16)
    sizes = (MLA_Q_LORA, MLA_KV_LORA, MLA_ROPE, 512, 128, 128, 512, 512, 512, N_BRANCHES * D_MODEL)
    pts = []
    acc = 0
    for s in sizes[:-1]:
        acc += s
        pts.append(acc)
    c_q, c_kv, k_rope, q_s, k_s, v_s, q_b, k_b, v_b, gates = jnp.split(w_in, pts, axis=-1)

    def kpe_pad(w):
        z = jnp.zeros((dep, D_MODEL, MLA_NOPE), BF16)
        return jnp.concatenate([z, w, jnp.zeros((dep, D_MODEL, 32), BF16)], axis=-1)

    swa_scale = SWA_HEAD_DIM ** -0.5
    sb_scale = SB_HEAD_DIM ** -0.5
    v_sh = v_s.reshape(dep, D_MODEL, SWA_KV_HEADS, SWA_HEAD_DIM)
    w1 = jnp.concatenate([
        c_q, c_kv, kpe_pad(k_rope), kpe_pad(_rot_cols(k_rope, 1, MLA_ROPE)),
        q_s * swa_scale, _rot_cols(q_s, SWA_HEADS, SWA_HEAD_DIM) * swa_scale,
        _dup_heads(k_s, SWA_KV_HEADS, SWA_HEAD_DIM),
        _dup_heads(_rot_cols(k_s, SWA_KV_HEADS, SWA_HEAD_DIM), SWA_KV_HEADS, SWA_HEAD_DIM),
        _pair_pad(v_sh, v_sh),
        q_b * sb_scale, k_b, v_b, gates], axis=-1)
    assert w1.shape[-1] == W1_COLS and w1.dtype == BF16

    uq = w_uq.astype(BF16).reshape(dep, MLA_Q_LORA, MLA_HEADS, MLA_NOPE + MLA_ROPE)
    nope, pe = uq[..., :MLA_NOPE], uq[..., MLA_NOPE:]
    pe_rot = jnp.concatenate([-pe[..., MLA_ROPE // 2:], pe[..., :MLA_ROPE // 2]], axis=-1)
    zq = jnp.zeros((dep, MLA_Q_LORA, MLA_HEADS, 32), BF16)
    wq_main = jnp.concatenate([nope, pe, zq], axis=-1).reshape(dep, MLA_Q_LORA, -1)
    wq_rot = jnp.concatenate([jnp.zeros_like(nope), pe_rot, zq], axis=-1).reshape(dep, MLA_Q_LORA, -1)
    wq = jnp.concatenate([wq_main, wq_rot], axis=-1)

    ukv = w_ukv.astype(BF16).reshape(dep, MLA_KV_LORA, MLA_HEADS, MLA_NOPE + MLA_V)
    k_nope, v = ukv[..., :MLA_NOPE], ukv[..., MLA_NOPE:]
    wk = jnp.concatenate([k_nope, jnp.zeros_like(k_nope)], axis=-1).reshape(dep, MLA_KV_LORA, -1)
    vp = v.reshape(dep, MLA_KV_LORA, HEAD_PAIRS, 2, MLA_V)
    wkv = jnp.concatenate([wk, _pair_pad(vp[..., 0, :], vp[..., 1, :])], axis=-1)
    return w1, wq, wkv


def kernel(x, positions, g_mix_pre, w_in, b_gate, g_q_lat, g_kv_lat, w_uq, w_ukv, swa_sinks, w_o_mla, w_o_swa, w_o_sb, w_out, g_mix_post, g_mlp_pre, w_up, w_down, g_mlp_post):
    batch, seq, d = x.shape
    t = batch * seq
    tables = _rope_tables(positions)
    w1, wq, wkv = _layout_weights(w_in, w_uq, w_ukv)
    woa, wob, woc, wout, wup, wdown = (w.astype(BF16) for w in (w_o_mla, w_o_swa, w_o_sb, w_out, w_up, w_down))
    tri = (lax.broadcasted_iota(jnp.int32, (SB_BLK, SB_BLK), 0)
           > lax.broadcasted_iota(jnp.int32, (SB_BLK, SB_BLK), 1)).astype(BF16)
    rows = lambda g: g.reshape(DEPTH, 1, -1)
    g_pre, bg, g_q, g_kv = rows(g_mix_pre), rows(b_gate), rows(g_q_lat), rows(g_kv_lat)
    g_post, g_mlp_in, g_mlp_out = rows(g_mix_post), rows(g_mlp_pre), rows(g_mlp_post)

    xt = x.reshape(t, d)
    for l in range(DEPTH):
        qm, km, vm, qs, ks, vs, qb, kb, vb, gates = _prep(xt, tables, l, g_pre, w1, bg, g_q, g_kv, wq, wkv)
        oa = _mla_attention(qm, km, vm, batch, seq)
        ob = _swa_attention(swa_sinks, l, qs, ks, vs, seq)
        oc = _sb_attention(qb, kb, vb, tri, batch, seq)
        xt = _post(xt, oa, ob, oc, gates, l, woa, wob, woc, wout, g_post, g_mlp_in, wup, wdown, g_mlp_out)
    return xt.reshape(batch, seq, d)
```

```python
import functools

import jax
import jax.numpy as jnp
from jax import lax
from jax.experimental import pallas as pl
from jax.experimental.pallas import tpu as pltpu

F32 = jnp.float32
BF16 = jnp.bfloat16

D_MODEL = 1024
DEPTH = 4
MLA_HEADS = 8
MLA_Q_LORA = 256
MLA_KV_LORA = 128
MLA_NOPE = 64
MLA_ROPE = 32
MLA_V = 64
SWA_HEADS = 8
SWA_KV_HEADS = 2
SWA_HEAD_DIM = 64
SWA_WINDOW = 128
SB_HEADS = 8
SB_HEAD_DIM = 64
D_FF = 4 * D_MODEL
ROPE_THETA = 10000.0
EPS = 1e-6
N_BRANCHES = 3

LANES = 128
HEAD_PAIRS = 4
MLA_HEAD_PAD = 128
NEG_BIG = -1e30
LOG2E = 1.4426950408889634
SB_SKIP = 160.0
SP_CLAMP = 126.0
VMEM_LIMIT = 56 * 1024 * 1024

OFF_CQ = 0
OFF_CKV = OFF_CQ + MLA_Q_LORA
OFF_KPE = OFF_CKV + MLA_KV_LORA
OFF_QS = OFF_KPE + LANES
OFF_KS = OFF_QS + 512
OFF_VS = OFF_KS + 128
OFF_QB = OFF_VS + 128
OFF_KB = OFF_QB + 512
OFF_VB = OFF_KB + 512
OFF_GATE = OFF_VB + 512
W1_COLS = OFF_GATE + N_BRANCHES * D_MODEL

PREP_TM = 256
POST_TM = 256
MLA_TQ = 512
MLA_TK = 512
MLA_PAIRS = 2
SB_TQ = 256
SB_TK = 256
SB_BLK = 256
SB_PAIRS = 4
SWA_TQ = 256
ROPE_TM = 2048


def _rms(x, g):
    return x * lax.rsqrt(jnp.mean(x * x, axis=-1, keepdims=True) + EPS) * g


def _dot(a, b):
    return jnp.dot(a, b, preferred_element_type=F32)


def _dot_nt(a, b):
    return lax.dot_general(a, b, (((1,), (1,)), ((), ())), preferred_element_type=F32)


def _const_spec(shape):
    return pl.BlockSpec(shape, lambda *_: (0,) * len(shape), pipeline_mode=pl.Buffered(1))


def _layer_spec(arr, layer):
    return pl.BlockSpec((None,) + arr.shape[1:], lambda *_: (layer, 0, 0), pipeline_mode=pl.Buffered(1))


def _params(sem):
    return pltpu.CompilerParams(dimension_semantics=sem, vmem_limit_bytes=VMEM_LIMIT)


def _pair_ones(width):
    lane = lax.broadcasted_iota(jnp.int32, (1, width), 1) % (2 * LANES)
    return ((lane >= 64) & (lane < 2 * LANES - 64)).astype(F32)


def _rope_table_kernel(pos_ref, inva_ref, maska_ref, invb_ref, ca_ref, sa_ref, cb_ref, sb_ref):
    pos = pos_ref[...].astype(F32)
    ang_a = pos * inva_ref[...]
    ca_ref[...] = jnp.cos(ang_a) * maska_ref[...]
    sa_ref[...] = jnp.sin(ang_a)
    ang_b = pos * invb_ref[...]
    cb_ref[...] = jnp.cos(ang_b)
    sb_ref[...] = jnp.sin(ang_b)


def _rope_tables(positions):
    t = positions.size
    pos = positions.reshape(t, 1)
    inv_a16 = 1.0 / (ROPE_THETA ** (jnp.arange(0, MLA_ROPE, 2, dtype=F32) / MLA_ROPE))
    inv_b32 = 1.0 / (ROPE_THETA ** (jnp.arange(0, SWA_HEAD_DIM, 2, dtype=F32) / SWA_HEAD_DIM))
    zeros = jnp.zeros
    inv_a = jnp.concatenate([zeros((MLA_NOPE,), F32), inv_a16, inv_a16, zeros((32,), F32)]).reshape(1, LANES)
    mask_a = jnp.concatenate([jnp.ones((MLA_NOPE + MLA_ROPE,), F32), zeros((32,), F32)]).reshape(1, LANES)
    inv_b = jnp.tile(inv_b32, 4).reshape(1, LANES)
    row = pl.BlockSpec((ROPE_TM, LANES), lambda i: (i, 0))
    vec = pl.BlockSpec((1, LANES), lambda i: (0, 0))
    out = jax.ShapeDtypeStruct((t, LANES), F32)
    return pl.pallas_call(
        _rope_table_kernel,
        grid=(t // ROPE_TM,),
        in_specs=[pl.BlockSpec((ROPE_TM, 1), lambda i: (i, 0)), vec, vec, vec],
        out_specs=[row, row, row, row],
        out_shape=[out, out, out, out],
        compiler_params=_params(("parallel",)),
        name="rope_tables",
    )(pos, inv_a, mask_a, inv_b)


def _prep_kernel(x_ref, g_ref, ca_ref, sa_ref, cb_ref, sb_ref, w1_ref, bg_ref, gq_ref, gkv_ref,
                 wq_ref, wkv_ref,
                 qm_ref, km_ref, vm_ref, qs_ref, ks_ref, vs_ref, qb_ref, kb_ref, vb_ref, gate_ref):
    h = _rms(x_ref[...], g_ref[...]).astype(BF16)

    def mm(lo, n):
        return _dot(h, w1_ref[:, lo:lo + n])

    ca, sa, cb, sb = ca_ref[...], sa_ref[...], cb_ref[...], sb_ref[...]

    lane = lax.broadcasted_iota(jnp.int32, (1, LANES), 1)
    low_head = lane < SWA_HEAD_DIM
    first_half = lane % SWA_HEAD_DIM < SWA_HEAD_DIM // 2

    def roll(v, shift):
        return pltpu.roll(v, shift, axis=1)

    def rope64(v, cos, sin):
        half = SWA_HEAD_DIM // 2
        rot = jnp.where(first_half, -roll(v, LANES - half), roll(v, half))
        return v * cos + rot * sin

    q_scale = SWA_HEAD_DIM ** -0.5 * LOG2E
    qs = mm(OFF_QS, 512)
    for p in range(HEAD_PAIRS):
        blk = slice(p * LANES, (p + 1) * LANES)
        qs_ref[:, blk] = rope64(qs[:, blk], cb * q_scale, sb * q_scale).astype(BF16)
    ks = rope64(mm(OFF_KS, LANES), cb, sb)
    ks_swap = roll(ks, SWA_HEAD_DIM)
    ks_ref[:, :LANES] = jnp.where(low_head, ks, ks_swap).astype(BF16)
    ks_ref[:, LANES:] = jnp.where(low_head, ks_swap, ks).astype(BF16)
    vs = mm(OFF_VS, LANES)
    vs_swap = roll(vs, SWA_HEAD_DIM)
    for n, blk_val in enumerate([jnp.where(low_head, vs, 1.0), jnp.where(low_head, 1.0, vs_swap),
                                 jnp.where(low_head, vs_swap, 1.0), jnp.where(low_head, 1.0, vs)]):
        vs_ref[:, n * LANES:(n + 1) * LANES] = blk_val.astype(BF16)

    qb_ref[...] = (mm(OFF_QB, 512) * (SB_HEAD_DIM ** -0.5 * LOG2E)).astype(BF16)
    kb_ref[...] = mm(OFF_KB, 512).astype(BF16)
    vb_ref[...] = mm(OFF_VB, 512).astype(BF16)

    gate_ref[...] = jax.nn.sigmoid(mm(OFF_GATE, N_BRANCHES * D_MODEL) + bg_ref[...])

    scale = (MLA_NOPE + MLA_ROPE) ** -0.5 * LOG2E
    cqn = _rms(mm(OFF_CQ, MLA_Q_LORA), gq_ref[...]).astype(BF16)
    ca8 = jnp.concatenate([ca * scale] * MLA_HEADS, axis=1)
    sa8 = jnp.concatenate([sa * scale] * MLA_HEADS, axis=1)
    nq = MLA_HEADS * MLA_HEAD_PAD
    qm_ref[...] = (_dot(cqn, wq_ref[:, :nq]) * ca8 + _dot(cqn, wq_ref[:, nq:]) * sa8).astype(BF16)
    kpe_blk = mm(OFF_KPE, LANES)
    rope_lanes = (lane >= MLA_NOPE) & (lane < MLA_NOPE + MLA_ROPE)
    kpe = jnp.where(rope_lanes, kpe_blk * ca + roll(kpe_blk, MLA_NOPE) * sa, 0.0)
    ckvn = _rms(mm(OFF_CKV, MLA_KV_LORA), gkv_ref[...]).astype(BF16)
    kpe8 = jnp.concatenate([kpe] * MLA_HEADS, axis=1)
    km_ref[...] = (_dot(ckvn, wkv_ref[:, :nq]) + kpe8).astype(BF16)
    vm_ref[...] = (_dot(ckvn, wkv_ref[:, nq:]) + _pair_ones(nq)).astype(BF16)


def _prep(x, tables, layer, g_pre, w1, b_gate, g_q, g_kv, wq, wkv):
    t = x.shape[0]
    tm = PREP_TM
    ca, sa, cb, sb = tables

    def row(n):
        return pl.BlockSpec((tm, n), lambda i: (i, 0))

    def out(n, dt=BF16):
        return jax.ShapeDtypeStruct((t, n), dt)

    nq = MLA_HEADS * MLA_HEAD_PAD
    consts = [w1, b_gate, g_q, g_kv, wq, wkv]
    return pl.pallas_call(
        _prep_kernel,
        grid=(t // tm,),
        in_specs=[row(D_MODEL), _layer_spec(g_pre, layer), row(LANES), row(LANES), row(LANES), row(LANES)]
                 + [_layer_spec(c, layer) for c in consts],
        out_specs=[row(nq), row(nq), row(nq), row(512), row(256), row(512), row(512), row(512), row(512),
                   row(N_BRANCHES * D_MODEL)],
        out_shape=[out(nq), out(nq), out(nq), out(512), out(256), out(512), out(512), out(512), out(512),
                   out(N_BRANCHES * D_MODEL, F32)],
        compiler_params=_params(("parallel",)),
        name="prep",
    )(x, g_pre, ca, sa, cb, sb, *consts)


def _mla_kernel(q_ref, k_ref, v_ref, o_ref, m_ref, acc_ref):
    qi = pl.program_id(2)
    tq, tk = MLA_TQ, MLA_TK
    n_heads = 2 * MLA_PAIRS
    m_ref[...] = jnp.full(m_ref.shape, NEG_BIG, F32)
    acc_ref[...] = jnp.zeros(acc_ref.shape, F32)

    def head_lanes(h):
        return slice(h * MLA_HEAD_PAD, (h + 1) * MLA_HEAD_PAD)

    def scores(j):
        start = pl.multiple_of(j * tk, tk)
        k = k_ref[pl.ds(start, tk), :]
        return tuple(_dot_nt(q_ref[:, head_lanes(h)], k[:, head_lanes(h)]) for h in range(n_heads))

    def process(chunks):
        s_all = [scores(j) for j, _ in chunks]
        width = len(chunks) * tk
        start = pl.multiple_of(chunks[0][0] * tk, tk)
        v = v_ref[pl.ds(start, width), :]
        for h in range(n_heads):
            parts = []
            for (_, masked), s_pair in zip(chunks, s_all):
                s = s_pair[h]
                if masked:
                    row = lax.broadcasted_iota(jnp.int32, (tq, tk), 0)
                    col = lax.broadcasted_iota(jnp.int32, (tq, tk), 1)
                    s = jnp.where(col <= row, s, NEG_BIG)
                parts.append(s)
            s = parts[0] if len(parts) == 1 else jnp.concatenate(parts, axis=1)
            m_old = m_ref[h]
            m_new = jnp.maximum(m_old, jnp.max(s, axis=-1, keepdims=True))
            alpha = jnp.exp2(m_old - m_new)
            p = jnp.exp2(s - jnp.concatenate([m_new] * (width // LANES), axis=1))
            acc_ref[h] = alpha * acc_ref[h] + _dot(p.astype(BF16), v[:, head_lanes(h)])
            m_ref[h] = m_new

    def body(i, carry):
        process([(2 * i, False), (2 * i + 1, False)])
        return carry

    lax.fori_loop(0, qi // 2, body, 0)

    @pl.when(qi % 2 == 1)
    def _():
        process([(qi - 1, False), (qi, True)])

    @pl.when(qi % 2 == 0)
    def _():
        process([(qi, True)])

    first_half = lax.broadcasted_iota(jnp.int32, (tq, LANES), 1) < MLA_V
    for p in range(MLA_PAIRS):
        a0, a1 = acc_ref[2 * p], acc_ref[2 * p + 1]
        num = jnp.where(first_half, a0, a1)
        den = jnp.where(first_half, pltpu.roll(a0, MLA_V, axis=1), pltpu.roll(a1, MLA_V, axis=1))
        o_ref[:, p * LANES:(p + 1) * LANES] = (num / den).astype(o_ref.dtype)


def _mla_attention(qm, km, vm, batch, seq):
    t = qm.shape[0]
    tq = MLA_TQ
    nq = seq // tq
    wide = MLA_PAIRS * 2 * MLA_HEAD_PAD
    return pl.pallas_call(
        _mla_kernel,
        grid=(batch, HEAD_PAIRS // MLA_PAIRS, nq),
        in_specs=[pl.BlockSpec((tq, wide), lambda b, p, i: (b * nq + i, p)),
                  pl.BlockSpec((seq, wide), lambda b, p, i: (b, p)),
                  pl.BlockSpec((seq, wide), lambda b, p, i: (b, p))],
        out_specs=pl.BlockSpec((tq, MLA_PAIRS * LANES), lambda b, p, i: (b * nq + i, p)),
        out_shape=jax.ShapeDtypeStruct((t, HEAD_PAIRS * LANES), BF16),
        scratch_shapes=[pltpu.VMEM((2 * MLA_PAIRS, tq, LANES), F32),
                        pltpu.VMEM((2 * MLA_PAIRS, tq, LANES), F32)],
        compiler_params=_params(("parallel", "parallel", "parallel")),
        name="mla_attention",
    )(qm, km, vm)


def _sb_kernel(q_ref, k_ref, v_ref, tri_ref, o_ref, qh_ref, carry_ref, acc_ref):
    qi = pl.program_id(2)
    tq, tk, blk = SB_TQ, SB_TK, SB_BLK
    carry_ref[...] = jnp.zeros(carry_ref.shape, F32)
    acc_ref[...] = jnp.zeros(acc_ref.shape, F32)
    n_heads = 2 * SB_PAIRS
    first_half = lax.broadcasted_iota(jnp.int32, (tq, LANES), 1) < SB_HEAD_DIM
    for p in range(SB_PAIRS):
        q2 = q_ref[:, p * LANES:(p + 1) * LANES]
        zero = jnp.zeros_like(q2)
        qh_ref[2 * p] = jnp.where(first_half, q2, zero)
        qh_ref[2 * p + 1] = jnp.where(first_half, zero, q2)

    def pair_lanes(h):
        return slice((h // 2) * LANES, (h // 2 + 1) * LANES)

    def logits(j):
        start = pl.multiple_of(j * tk, tk)
        k = k_ref[pl.ds(start, tk), :]
        return tuple(_dot_nt(qh_ref[h], k[:, pair_lanes(h)]) for h in range(n_heads))

    def weights_pv(j, z_all, masked):
        start = pl.multiple_of(j * tk, tk)
        v = v_ref[pl.ds(start, tk), :]
        if masked:
            row = lax.broadcasted_iota(jnp.int32, (tq, tk), 0)
            col = lax.broadcasted_iota(jnp.int32, (tq, tk), 1)
            valid = col < row
        for h, z in enumerate(z_all):
            sp = jnp.maximum(z, jnp.log2(1.0 + jnp.exp2(jnp.minimum(z, SP_CLAMP))))
            own = z - sp
            if masked:
                sp = jnp.where(valid, sp, 0.0)
            sp16 = sp.astype(BF16)
            c = carry_ref[h]
            expo = [None] * (tk // blk)
            for b in reversed(range(tk // blk)):
                cols = slice(b * blk, (b + 1) * blk)
                later = _dot(sp16[:, cols], tri_ref[...])
                expo[b] = own[:, cols] - later - jnp.concatenate([c] * (blk // LANES), axis=1)
                c = c + jnp.sum(sp[:, cols], axis=-1, keepdims=True)
            carry_ref[h] = c
            a = jnp.exp2(jnp.concatenate(expo, axis=1))
            if masked:
                a = jnp.where(valid, a, 0.0)
            acc_ref[h] += _dot(a.astype(BF16), v[:, pair_lanes(h)])

    def process(chunks):
        z_chunks = [logits(j) for j, _ in chunks]
        for (j, masked), z_all in zip(chunks, z_chunks):
            weights_pv(j, z_all, masked)

    def min_carry():
        c = carry_ref[0]
        for h in range(1, n_heads):
            c = jnp.minimum(c, carry_ref[h])
        return jnp.min(c)

    def more(state):
        j, cmin = state
        return jnp.logical_and(j >= 0, cmin < SB_SKIP)

    def step(state):
        j, _ = state
        process([(j, False)])
        return j - 1, min_carry()

    @pl.when(qi == 0)
    def _():
        process([(qi, True)])

    @pl.when(qi > 0)
    def _():
        process([(qi, True), (qi - 1, False)])

    lax.while_loop(more, step, (qi - 2, min_carry()))

    for p in range(SB_PAIRS):
        o_ref[:, p * LANES:(p + 1) * LANES] = jnp.where(
            first_half, acc_ref[2 * p], acc_ref[2 * p + 1]).astype(o_ref.dtype)


def _sb_attention(qb, kb, vb, tri, batch, seq):
    t = qb.shape[0]
    tq = SB_TQ
    nq = seq // tq
    wide = SB_PAIRS * LANES
    return pl.pallas_call(
        _sb_kernel,
        grid=(batch, HEAD_PAIRS // SB_PAIRS, nq),
        in_specs=[pl.BlockSpec((tq, wide), lambda b, p, i: (b * nq + i, p)),
                  pl.BlockSpec((seq, wide), lambda b, p, i: (b, p)),
                  pl.BlockSpec((seq, wide), lambda b, p, i: (b, p)),
                  _const_spec(tri.shape)],
        out_specs=pl.BlockSpec((tq, wide), lambda b, p, i: (b * nq + i, p)),
        out_shape=jax.ShapeDtypeStruct((t, HEAD_PAIRS * LANES), BF16),
        scratch_shapes=[pltpu.VMEM((2 * SB_PAIRS, tq, LANES), BF16), pltpu.VMEM((2 * SB_PAIRS, tq, LANES), F32),
                        pltpu.VMEM((2 * SB_PAIRS, tq, LANES), F32)],
        compiler_params=_params(("parallel", "parallel", "parallel")),
        name="sb_attention",
    )(qb, kb, vb, tri)


def _swa_kernel(sink_ref, q_ref, k_ref, v_ref, kp_ref, vp_ref, o_ref, *, layer, tiles_per_seq):
    i = pl.program_id(0)
    w, tq = SWA_WINDOW, SWA_TQ
    has_prev = (i % tiles_per_seq) != 0
    kcat = jnp.concatenate([kp_ref[...], k_ref[...]], axis=0)
    vcat = jnp.concatenate([vp_ref[...], v_ref[...]], axis=0)
    row = lax.broadcasted_iota(jnp.int32, (w, 2 * w), 0)
    col = lax.broadcasted_iota(jnp.int32, (w, 2 * w), 1)
    band = (col > row) & (col <= row + w)
    band_first = band & ((col >= w) | has_prev)
    first_half = lax.broadcasted_iota(jnp.int32, (w, LANES), 1) < SWA_HEAD_DIM
    for p in range(HEAD_PAIRS):
        g = p // (HEAD_PAIRS // SWA_KV_HEADS)
        q2 = q_ref[:, p * LANES:(p + 1) * LANES]
        kg = kcat[:, g * LANES:(g + 1) * LANES]
        half = lax.broadcasted_iota(jnp.int32, q2.shape, 1) < SWA_HEAD_DIM
        zero = jnp.zeros_like(q2)
        qh = [jnp.where(half, q2, zero), jnp.where(half, zero, q2)]
        s_full = [_dot_nt(qh[hh], kg) for hh in range(2)]
        for r in range(tq // w):
            keys = slice(r * w, (r + 2) * w)
            acc, esink = [], []
            for hh in range(2):
                sink = sink_ref[layer, 2 * p + hh] * LOG2E
                s = jnp.where(band_first if r == 0 else band, s_full[hh][r * w:(r + 1) * w, keys], NEG_BIG)
                m = jnp.maximum(jnp.broadcast_to(jnp.max(s, axis=-1, keepdims=True), (w, LANES)), sink)
                prob = jnp.exp2(s - jnp.concatenate([m, m], axis=1))
                vh = vcat[keys, (2 * g + hh) * LANES:(2 * g + hh + 1) * LANES]
                acc.append(_dot(prob.astype(BF16), vh))
                esink.append(jnp.exp2(sink - m))
            num = jnp.where(first_half, acc[0], acc[1])
            den = (jnp.where(first_half, pltpu.roll(acc[0], SWA_HEAD_DIM, axis=1),
                             pltpu.roll(acc[1], SWA_HEAD_DIM, axis=1))
                   + jnp.where(first_half, esink[0], esink[1]))
            o_ref[r * w:(r + 1) * w, p * LANES:(p + 1) * LANES] = (num / den).astype(o_ref.dtype)


def _swa_attention(sinks, layer, qs, ks, vs, seq):
    t = qs.shape[0]
    tq = SWA_TQ
    per_tile = tq // SWA_WINDOW
    cur = lambda n: pl.BlockSpec((tq, n), lambda i: (i, 0))
    prev = lambda n: pl.BlockSpec((SWA_WINDOW, n), lambda i: (jnp.maximum(i * per_tile - 1, 0), 0))
    return pl.pallas_call(
        functools.partial(_swa_kernel, layer=layer, tiles_per_seq=seq // tq),
        grid=(t // tq,),
        in_specs=[pl.BlockSpec(memory_space=pltpu.SMEM), cur(512), cur(256), cur(512), prev(256), prev(512)],
        out_specs=cur(512),
        out_shape=jax.ShapeDtypeStruct((t, 512), BF16),
        compiler_params=_params(("parallel",)),
        name="swa_attention",
    )(sinks, qs, ks, vs, ks, vs)


def _post_kernel(x_ref, oa_ref, ob_ref, oc_ref, gate_ref, woa_ref, wob_ref, woc_ref, wout_ref,
                 gpost_ref, gpre_ref, wup_ref, wdown_ref, gmlp_ref, out_ref):
    d = D_MODEL
    mixed = (gate_ref[:, 0:d] * _dot(oa_ref[...], woa_ref[...])
             + gate_ref[:, d:2 * d] * _dot(ob_ref[...], wob_ref[...])
             + gate_ref[:, 2 * d:3 * d] * _dot(oc_ref[...], woc_ref[...]))
    x1 = x_ref[...] + _rms(_dot(mixed.astype(BF16), wout_ref[...]), gpost_ref[...])
    h = _rms(x1, gpre_ref[...]).astype(BF16)
    u = jnp.square(jnp.maximum(_dot(h, wup_ref[...]), 0.0)).astype(BF16)
    out_ref[...] = x1 + _rms(_dot(u, wdown_ref[...]), gmlp_ref[...])


def _post(x, oa, ob, oc, gates, layer, woa, wob, woc, wout, g_post, g_pre, wup, wdown, g_mlp):
    t = x.shape[0]
    tm = POST_TM
    row = lambda n: pl.BlockSpec((tm, n), lambda i: (i, 0))
    consts = [woa, wob, woc, wout, g_post, g_pre, wup, wdown, g_mlp]
    return pl.pallas_call(
        _post_kernel,
        grid=(t // tm,),
        in_specs=[row(D_MODEL), row(512), row(512), row(512), row(N_BRANCHES * D_MODEL)]
                 + [_layer_spec(c, layer) for c in consts],
        out_specs=row(D_MODEL),
        out_shape=jax.ShapeDtypeStruct((t, D_MODEL), F32),
        compiler_params=_params(("parallel",)),
        name="post",
    )(x, oa, ob, oc, gates, *consts)


def _rot_half(w):
    half = w.shape[-1] // 2
    return jnp.concatenate([-w[..., half:], w[..., :half]], axis=-1)


def _pair_pad(a, b):
    z = jnp.zeros_like(a)
    return jnp.concatenate([a, z, z, b], axis=-1).reshape(*a.shape[:-2], -1)


def _layout_weights(w_in, w_uq, w_ukv):
    dep = w_in.shape[0]
    w_in = w_in.astype(BF16)
    rope_lo = MLA_Q_LORA + MLA_KV_LORA
    k_rope = w_in[..., rope_lo:rope_lo + MLA_ROPE]
    z32 = jnp.zeros((dep, D_MODEL, 32), BF16)
    w1 = jnp.concatenate([w_in[..., :rope_lo], _rot_half(k_rope), z32, k_rope, z32,
                          w_in[..., rope_lo + MLA_ROPE:]], axis=-1)
    assert w1.shape[-1] == W1_COLS and w1.dtype == BF16

    uq = w_uq.astype(BF16).reshape(dep, MLA_Q_LORA, MLA_HEADS, MLA_NOPE + MLA_ROPE)
    nope, pe = uq[..., :MLA_NOPE], uq[..., MLA_NOPE:]
    pe_rot = _rot_half(pe)
    zq = jnp.zeros((dep, MLA_Q_LORA, MLA_HEADS, 32), BF16)
    wq_main = jnp.concatenate([nope, pe, zq], axis=-1).reshape(dep, MLA_Q_LORA, -1)
    wq_rot = jnp.concatenate([jnp.zeros_like(nope), pe_rot, zq], axis=-1).reshape(dep, MLA_Q_LORA, -1)
    wq = jnp.concatenate([wq_main, wq_rot], axis=-1)

    ukv = w_ukv.astype(BF16).reshape(dep, MLA_KV_LORA, MLA_HEADS, MLA_NOPE + MLA_V)
    k_nope, v = ukv[..., :MLA_NOPE], ukv[..., MLA_NOPE:]
    wk = jnp.concatenate([k_nope, jnp.zeros_like(k_nope)], axis=-1).reshape(dep, MLA_KV_LORA, -1)
    vp = v.reshape(dep, MLA_KV_LORA, HEAD_PAIRS, 2, MLA_V)
    wkv = jnp.concatenate([wk, _pair_pad(vp[..., 0, :], vp[..., 1, :])], axis=-1)
    return w1, wq, wkv


def kernel(x, positions, g_mix_pre, w_in, b_gate, g_q_lat, g_kv_lat, w_uq, w_ukv, swa_sinks, w_o_mla, w_o_swa, w_o_sb, w_out, g_mix_post, g_mlp_pre, w_up, w_down, g_mlp_post):
    batch, seq, d = x.shape
    t = batch * seq
    tables = _rope_tables(positions)
    w1, wq, wkv = _layout_weights(w_in, w_uq, w_ukv)
    woa, wob, woc, wout, wup, wdown = (w.astype(BF16) for w in (w_o_mla, w_o_swa, w_o_sb, w_out, w_up, w_down))
    tri = (lax.broadcasted_iota(jnp.int32, (SB_BLK, SB_BLK), 0)
           > lax.broadcasted_iota(jnp.int32, (SB_BLK, SB_BLK), 1)).astype(BF16)
    rows = lambda g: g.reshape(DEPTH, 1, -1)
    g_pre, bg, g_q, g_kv = rows(g_mix_pre), rows(b_gate), rows(g_q_lat), rows(g_kv_lat)
    g_post, g_mlp_in, g_mlp_out = rows(g_mix_post), rows(g_mlp_pre), rows(g_mlp_post)

    xt = x.reshape(t, d)
    for l in range(DEPTH):
        qm, km, vm, qs, ks, vs, qb, kb, vb, gates = _prep(xt, tables, l, g_pre, w1, bg, g_q, g_kv, wq, wkv)
        oa = _mla_attention(qm, km, vm, batch, seq)
        ob = _swa_attention(swa_sinks, l, qs, ks, vs, seq)
        oc = _sb_attention(qb, kb, vb, tri, batch, seq)
        xt = _post(xt, oa, ob, oc, gates, l, woa, wob, woc, wout, g_post, g_mlp_in, wup, wdown, g_mlp_out)
    return xt.reshape(batch, seq, d)
```

```python
import functools

import jax
import jax.numpy as jnp
from jax import lax
from jax.experimental import pallas as pl
from jax.experimental.pallas import tpu as pltpu

F32 = jnp.float32
BF16 = jnp.bfloat16

D_MODEL = 1024
DEPTH = 4
MLA_HEADS = 8
MLA_Q_LORA = 256
MLA_KV_LORA = 128
MLA_NOPE = 64
MLA_ROPE = 32
MLA_V = 64
SWA_HEADS = 8
SWA_KV_HEADS = 2
SWA_HEAD_DIM = 64
SWA_WINDOW = 128
SB_HEADS = 8
SB_HEAD_DIM = 64
D_FF = 4 * D_MODEL
ROPE_THETA = 10000.0
EPS = 1e-6
N_BRANCHES = 3

LANES = 128
HEAD_PAIRS = 4
MLA_HEAD_PAD = 128
NEG_BIG = -1e30
LOG2E = 1.4426950408889634
SB_SKIP = 160.0
SP_CLAMP = 126.0
VMEM_LIMIT = 56 * 1024 * 1024

OFF_CQ = 0
OFF_CKV = OFF_CQ + MLA_Q_LORA
OFF_KPE = OFF_CKV + MLA_KV_LORA
OFF_QS = OFF_KPE + LANES
OFF_KS = OFF_QS + 512
OFF_VS = OFF_KS + 128
OFF_QB = OFF_VS + 128
OFF_KB = OFF_QB + 512
OFF_VB = OFF_KB + 512
OFF_GATE = OFF_VB + 512
W1_COLS = OFF_GATE + N_BRANCHES * D_MODEL

PREP_TM = 256
POST_TM = 256
MLA_TQ = 512
MLA_TK = 512
MLA_PAIRS = 2
SB_TQ = 256
SB_TK = 256
SB_BLK = 256
SB_PAIRS = 4
SWA_TQ = 256
ROPE_TM = 2048


def _rms(x, g):
    return x * lax.rsqrt(jnp.mean(x * x, axis=-1, keepdims=True) + EPS) * g


def _dot(a, b):
    return jnp.dot(a, b, preferred_element_type=F32)


def _dot_nt(a, b):
    return lax.dot_general(a, b, (((1,), (1,)), ((), ())), preferred_element_type=F32)


def _const_spec(shape):
    return pl.BlockSpec(shape, lambda *_: (0,) * len(shape), pipeline_mode=pl.Buffered(1))


def _layer_spec(arr, layer):
    return pl.BlockSpec((None,) + arr.shape[1:], lambda *_: (layer, 0, 0), pipeline_mode=pl.Buffered(1))


def _params(sem):
    return pltpu.CompilerParams(dimension_semantics=sem, vmem_limit_bytes=VMEM_LIMIT)


def _pair_ones(width):
    lane = lax.broadcasted_iota(jnp.int32, (1, width), 1) % (2 * LANES)
    return ((lane >= 64) & (lane < 2 * LANES - 64)).astype(F32)


def _rope_table_kernel(pos_ref, inv_ref, ca_ref, sa_ref, cb_ref, sb_ref):
    ang = pos_ref[...].astype(F32) * inv_ref[...]
    cos, sin = jnp.cos(ang), jnp.sin(ang)
    lane = lax.broadcasted_iota(jnp.int32, (1, LANES), 1)
    low = lane < SWA_HEAD_DIM
    rope_lanes = (lane >= MLA_NOPE) & (lane < MLA_NOPE + MLA_ROPE)
    ca_ref[...] = jnp.where(rope_lanes, cos, jnp.where(low, 1.0, 0.0))
    sa_ref[...] = jnp.where(rope_lanes, sin, 0.0)
    cb_ref[...] = jnp.where(low, cos, pltpu.roll(cos, SWA_HEAD_DIM, axis=1))
    sb_ref[...] = jnp.where(low, sin, pltpu.roll(sin, SWA_HEAD_DIM, axis=1))


def _rope_tables(positions):
    t = positions.size
    pos = positions.reshape(t, 1)
    inv_a16 = 1.0 / (ROPE_THETA ** (jnp.arange(0, MLA_ROPE, 2, dtype=F32) / MLA_ROPE))
    inv_b32 = 1.0 / (ROPE_THETA ** (jnp.arange(0, SWA_HEAD_DIM, 2, dtype=F32) / SWA_HEAD_DIM))
    inv = jnp.concatenate([inv_b32, inv_b32, inv_a16, inv_a16, jnp.zeros((32,), F32)]).reshape(1, LANES)
    row = pl.BlockSpec((ROPE_TM, LANES), lambda i: (i, 0))
    out = jax.ShapeDtypeStruct((t, LANES), F32)
    return pl.pallas_call(
        _rope_table_kernel,
        grid=(t // ROPE_TM,),
        in_specs=[pl.BlockSpec((ROPE_TM, 1), lambda i: (i, 0)), pl.BlockSpec((1, LANES), lambda i: (0, 0))],
        out_specs=[row, row, row, row],
        out_shape=[out, out, out, out],
        compiler_params=_params(("parallel",)),
        name="rope_tables",
    )(pos, inv)


def _prep_kernel(x_ref, g_ref, ca_ref, sa_ref, cb_ref, sb_ref, w1_ref, bg_ref, gq_ref, gkv_ref,
                 wq_ref, wkv_ref,
                 qm_ref, km_ref, vm_ref, qs_ref, ks_ref, vs_ref, qb_ref, kb_ref, vb_ref, gate_ref):
    h = _rms(x_ref[...], g_ref[...]).astype(BF16)

    def mm(lo, n):
        return _dot(h, w1_ref[:, lo:lo + n])

    ca, sa, cb, sb = ca_ref[...], sa_ref[...], cb_ref[...], sb_ref[...]

    lane = lax.broadcasted_iota(jnp.int32, (1, LANES), 1)
    low_head = lane < SWA_HEAD_DIM
    first_half = lane % SWA_HEAD_DIM < SWA_HEAD_DIM // 2

    def roll(v, shift):
        return pltpu.roll(v, shift, axis=1)

    def rope64(v, cos, sin):
        half = SWA_HEAD_DIM // 2
        rot = jnp.where(first_half, -roll(v, LANES - half), roll(v, half))
        return v * cos + rot * sin

    q_scale = SWA_HEAD_DIM ** -0.5 * LOG2E
    qs = mm(OFF_QS, 512)
    for p in range(HEAD_PAIRS):
        blk = slice(p * LANES, (p + 1) * LANES)
        qs_ref[:, blk] = rope64(qs[:, blk], cb * q_scale, sb * q_scale).astype(BF16)
    ks = rope64(mm(OFF_KS, LANES), cb, sb)
    ks_swap = roll(ks, SWA_HEAD_DIM)
    ks_ref[:, :LANES] = jnp.where(low_head, ks, ks_swap).astype(BF16)
    ks_ref[:, LANES:] = jnp.where(low_head, ks_swap, ks).astype(BF16)
    vs = mm(OFF_VS, LANES)
    vs_swap = roll(vs, SWA_HEAD_DIM)
    for n, blk_val in enumerate([jnp.where(low_head, vs, 1.0), jnp.where(low_head, 1.0, vs_swap),
                                 jnp.where(low_head, vs_swap, 1.0), jnp.where(low_head, 1.0, vs)]):
        vs_ref[:, n * LANES:(n + 1) * LANES] = blk_val.astype(BF16)

    qb_ref[...] = (mm(OFF_QB, 512) * (SB_HEAD_DIM ** -0.5 * LOG2E)).astype(BF16)
    kb_ref[...] = mm(OFF_KB, 512).astype(BF16)
    vb_ref[...] = mm(OFF_VB, 512).astype(BF16)

    gate_ref[...] = jax.nn.sigmoid(mm(OFF_GATE, N_BRANCHES * D_MODEL) + bg_ref[...])

    scale = (MLA_NOPE + MLA_ROPE) ** -0.5 * LOG2E
    cqn = _rms(mm(OFF_CQ, MLA_Q_LORA), gq_ref[...]).astype(BF16)
    ca8 = jnp.concatenate([ca * scale] * MLA_HEADS, axis=1)
    sa8 = jnp.concatenate([sa * scale] * MLA_HEADS, axis=1)
    nq = MLA_HEADS * MLA_HEAD_PAD
    qm_ref[...] = (_dot(cqn, wq_ref[:, :nq]) * ca8 + _dot(cqn, wq_ref[:, nq:]) * sa8).astype(BF16)
    kpe_blk = mm(OFF_KPE, LANES)
    rope_lanes = (lane >= MLA_NOPE) & (lane < MLA_NOPE + MLA_ROPE)
    kpe = jnp.where(rope_lanes, kpe_blk * ca + roll(kpe_blk, MLA_NOPE) * sa, 0.0)
    ckvn = _rms(mm(OFF_CKV, MLA_KV_LORA), gkv_ref[...]).astype(BF16)
    kpe8 = jnp.concatenate([kpe] * MLA_HEADS, axis=1)
    km_ref[...] = (_dot(ckvn, wkv_ref[:, :nq]) + kpe8).astype(BF16)
    vm_ref[...] = (_dot(ckvn, wkv_ref[:, nq:]) + _pair_ones(nq)).astype(BF16)


def _prep(x, tables, layer, g_pre, w1, b_gate, g_q, g_kv, wq, wkv):
    t = x.shape[0]
    tm = PREP_TM
    ca, sa, cb, sb = tables

    def row(n):
        return pl.BlockSpec((tm, n), lambda i: (i, 0))

    def out(n, dt=BF16):
        return jax.ShapeDtypeStruct((t, n), dt)

    nq = MLA_HEADS * MLA_HEAD_PAD
    consts = [w1, b_gate, g_q, g_kv, wq, wkv]
    return pl.pallas_call(
        _prep_kernel,
        grid=(t // tm,),
        in_specs=[row(D_MODEL), _layer_spec(g_pre, layer), row(LANES), row(LANES), row(LANES), row(LANES)]
                 + [_layer_spec(c, layer) for c in consts],
        out_specs=[row(nq), row(nq), row(nq), row(512), row(256), row(512), row(512), row(512), row(512),
                   row(N_BRANCHES * D_MODEL)],
        out_shape=[out(nq), out(nq), out(nq), out(512), out(256), out(512), out(512), out(512), out(512),
                   out(N_BRANCHES * D_MODEL, F32)],
        compiler_params=_params(("parallel",)),
        name="prep",
    )(x, g_pre, ca, sa, cb, sb, *consts)


def _mla_kernel(q_ref, k_ref, v_ref, o_ref, m_ref, acc_ref):
    qi = pl.program_id(2)
    tq, tk = MLA_TQ, MLA_TK
    n_heads = 2 * MLA_PAIRS
    m_ref[...] = jnp.full(m_ref.shape, NEG_BIG, F32)
    acc_ref[...] = jnp.zeros(acc_ref.shape, F32)

    def head_lanes(h):
        return slice(h * MLA_HEAD_PAD, (h + 1) * MLA_HEAD_PAD)

    def scores(j):
        start = pl.multiple_of(j * tk, tk)
        k = k_ref[pl.ds(start, tk), :]
        return tuple(_dot_nt(q_ref[:, head_lanes(h)], k[:, head_lanes(h)]) for h in range(n_heads))

    def process(chunks):
        s_all = [scores(j) for j, _ in chunks]
        width = len(chunks) * tk
        start = pl.multiple_of(chunks[0][0] * tk, tk)
        v = v_ref[pl.ds(start, width), :]
        for h in range(n_heads):
            parts = []
            for (_, masked), s_pair in zip(chunks, s_all):
                s = s_pair[h]
                if masked:
                    row = lax.broadcasted_iota(jnp.int32, (tq, tk), 0)
                    col = lax.broadcasted_iota(jnp.int32, (tq, tk), 1)
                    s = jnp.where(col <= row, s, NEG_BIG)
                parts.append(s)
            s = parts[0] if len(parts) == 1 else jnp.concatenate(parts, axis=1)
            m_old = m_ref[h]
            m_new = jnp.maximum(m_old, jnp.max(s, axis=-1, keepdims=True))
            alpha = jnp.exp2(m_old - m_new)
            p = jnp.exp2(s - jnp.concatenate([m_new] * (width // LANES), axis=1))
            acc_ref[h] = alpha * acc_ref[h] + _dot(p.astype(BF16), v[:, head_lanes(h)])
            m_ref[h] = m_new

    def body(i, carry):
        process([(2 * i, False), (2 * i + 1, False)])
        return carry

    lax.fori_loop(0, qi // 2, body, 0)

    @pl.when(qi % 2 == 1)
    def _():
        process([(qi - 1, False), (qi, True)])

    @pl.when(qi % 2 == 0)
    def _():
        process([(qi, True)])

    first_half = lax.broadcasted_iota(jnp.int32, (tq, LANES), 1) < MLA_V
    for p in range(MLA_PAIRS):
        a0, a1 = acc_ref[2 * p], acc_ref[2 * p + 1]
        num = jnp.where(first_half, a0, a1)
        den = jnp.where(first_half, pltpu.roll(a0, MLA_V, axis=1), pltpu.roll(a1, MLA_V, axis=1))
        o_ref[:, p * LANES:(p + 1) * LANES] = (num / den).astype(o_ref.dtype)


def _mla_attention(qm, km, vm, batch, seq):
    t = qm.shape[0]
    tq = MLA_TQ
    nq = seq // tq
    wide = MLA_PAIRS * 2 * MLA_HEAD_PAD
    return pl.pallas_call(
        _mla_kernel,
        grid=(batch, HEAD_PAIRS // MLA_PAIRS, nq),
        in_specs=[pl.BlockSpec((tq, wide), lambda b, p, i: (b * nq + i, p)),
                  pl.BlockSpec((seq, wide), lambda b, p, i: (b, p)),
                  pl.BlockSpec((seq, wide), lambda b, p, i: (b, p))],
        out_specs=pl.BlockSpec((tq, MLA_PAIRS * LANES), lambda b, p, i: (b * nq + i, p)),
        out_shape=jax.ShapeDtypeStruct((t, HEAD_PAIRS * LANES), BF16),
        scratch_shapes=[pltpu.VMEM((2 * MLA_PAIRS, tq, LANES), F32),
                        pltpu.VMEM((2 * MLA_PAIRS, tq, LANES), F32)],
        compiler_params=_params(("parallel", "parallel", "parallel")),
        name="mla_attention",
    )(qm, km, vm)


def _sb_kernel(q_ref, k_ref, v_ref, tri_ref, o_ref, qh_ref, carry_ref, acc_ref):
    qi = pl.program_id(2)
    tq, tk, blk = SB_TQ, SB_TK, SB_BLK
    carry_ref[...] = jnp.zeros(carry_ref.shape, F32)
    acc_ref[...] = jnp.zeros(acc_ref.shape, F32)
    n_heads = 2 * SB_PAIRS
    first_half = lax.broadcasted_iota(jnp.int32, (tq, LANES), 1) < SB_HEAD_DIM
    for p in range(SB_PAIRS):
        q2 = q_ref[:, p * LANES:(p + 1) * LANES]
        zero = jnp.zeros_like(q2)
        qh_ref[2 * p] = jnp.where(first_half, q2, zero)
        qh_ref[2 * p + 1] = jnp.where(first_half, zero, q2)

    def pair_lanes(h):
        return slice((h // 2) * LANES, (h // 2 + 1) * LANES)

    def logits(j):
        start = pl.multiple_of(j * tk, tk)
        k = k_ref[pl.ds(start, tk), :]
        return tuple(_dot_nt(qh_ref[h], k[:, pair_lanes(h)]) for h in range(n_heads))

    def weights_pv(j, z_all, masked):
        start = pl.multiple_of(j * tk, tk)
        v = v_ref[pl.ds(start, tk), :]
        if masked:
            row = lax.broadcasted_iota(jnp.int32, (tq, tk), 0)
            col = lax.broadcasted_iota(jnp.int32, (tq, tk), 1)
            valid = col < row
        for h, z in enumerate(z_all):
            sp = jnp.maximum(z, jnp.log2(1.0 + jnp.exp2(jnp.minimum(z, SP_CLAMP))))
            own = z - sp
            if masked:
                sp = jnp.where(valid, sp, 0.0)
            sp16 = sp.astype(BF16)
            c = carry_ref[h]
            expo = [None] * (tk // blk)
            for b in reversed(range(tk // blk)):
                cols = slice(b * blk, (b + 1) * blk)
                later = _dot(sp16[:, cols], tri_ref[...])
                expo[b] = own[:, cols] - later - jnp.concatenate([c] * (blk // LANES), axis=1)
                c = c + jnp.sum(sp[:, cols], axis=-1, keepdims=True)
            carry_ref[h] = c
            a = jnp.exp2(jnp.concatenate(expo, axis=1))
            if masked:
                a = jnp.where(valid, a, 0.0)
            acc_ref[h] += _dot(a.astype(BF16), v[:, pair_lanes(h)])

    def process(chunks):
        z_chunks = [logits(j) for j, _ in chunks]
        for (j, masked), z_all in zip(chunks, z_chunks):
            weights_pv(j, z_all, masked)

    def min_carry():
        c = carry_ref[0]
        for h in range(1, n_heads):
            c = jnp.minimum(c, carry_ref[h])
        return jnp.min(c)

    def more(state):
        j, cmin = state
        return jnp.logical_and(j >= 0, cmin < SB_SKIP)

    def step(state):
        j, _ = state
        process([(j, False)])
        return j - 1, min_carry()

    @pl.when(qi == 0)
    def _():
        process([(qi, True)])

    @pl.when(qi > 0)
    def _():
        process([(qi, True), (qi - 1, False)])

    lax.while_loop(more, step, (qi - 2, min_carry()))

    for p in range(SB_PAIRS):
        o_ref[:, p * LANES:(p + 1) * LANES] = jnp.where(
            first_half, acc_ref[2 * p], acc_ref[2 * p + 1]).astype(o_ref.dtype)


def _sb_attention(qb, kb, vb, tri, batch, seq):
    t = qb.shape[0]
    tq = SB_TQ
    nq = seq // tq
    wide = SB_PAIRS * LANES
    return pl.pallas_call(
        _sb_kernel,
        grid=(batch, HEAD_PAIRS // SB_PAIRS, nq),
        in_specs=[pl.BlockSpec((tq, wide), lambda b, p, i: (b * nq + i, p)),
                  pl.BlockSpec((seq, wide), lambda b, p, i: (b, p)),
                  pl.BlockSpec((seq, wide), lambda b, p, i: (b, p)),
                  _const_spec(tri.shape)],
        out_specs=pl.BlockSpec((tq, wide), lambda b, p, i: (b * nq + i, p)),
        out_shape=jax.ShapeDtypeStruct((t, HEAD_PAIRS * LANES), BF16),
        scratch_shapes=[pltpu.VMEM((2 * SB_PAIRS, tq, LANES), BF16), pltpu.VMEM((2 * SB_PAIRS, tq, LANES), F32),
                        pltpu.VMEM((2 * SB_PAIRS, tq, LANES), F32)],
        compiler_params=_params(("parallel", "parallel", "parallel")),
        name="sb_attention",
    )(qb, kb, vb, tri)


def _swa_kernel(sink_ref, q_ref, k_ref, v_ref, kp_ref, vp_ref, o_ref, *, layer, tiles_per_seq):
    i = pl.program_id(0)
    w, tq = SWA_WINDOW, SWA_TQ
    has_prev = (i % tiles_per_seq) != 0
    kcat = jnp.concatenate([kp_ref[...], k_ref[...]], axis=0)
    vcat = jnp.concatenate([vp_ref[...], v_ref[...]], axis=0)
    row = lax.broadcasted_iota(jnp.int32, (w, 2 * w), 0)
    col = lax.broadcasted_iota(jnp.int32, (w, 2 * w), 1)
    band = (col > row) & (col <= row + w)
    band_first = band & ((col >= w) | has_prev)
    first_half = lax.broadcasted_iota(jnp.int32, (w, LANES), 1) < SWA_HEAD_DIM
    for p in range(HEAD_PAIRS):
        g = p // (HEAD_PAIRS // SWA_KV_HEADS)
        q2 = q_ref[:, p * LANES:(p + 1) * LANES]
        kg = kcat[:, g * LANES:(g + 1) * LANES]
        half = lax.broadcasted_iota(jnp.int32, q2.shape, 1) < SWA_HEAD_DIM
        zero = jnp.zeros_like(q2)
        qh = [jnp.where(half, q2, zero), jnp.where(half, zero, q2)]
        s_full = [_dot_nt(qh[hh], kg) for hh in range(2)]
        for r in range(tq // w):
            keys = slice(r * w, (r + 2) * w)
            acc, esink = [], []
            for hh in range(2):
                sink = sink_ref[layer, 2 * p + hh] * LOG2E
                s = jnp.where(band_first if r == 0 else band, s_full[hh][r * w:(r + 1) * w, keys], NEG_BIG)
                m = jnp.maximum(jnp.broadcast_to(jnp.max(s, axis=-1, keepdims=True), (w, LANES)), sink)
                prob = jnp.exp2(s - jnp.concatenate([m, m], axis=1))
                vh = vcat[keys, (2 * g + hh) * LANES:(2 * g + hh + 1) * LANES]
                acc.append(_dot(prob.astype(BF16), vh))
                esink.append(jnp.exp2(sink - m))
            num = jnp.where(first_half, acc[0], acc[1])
            den = (jnp.where(first_half, pltpu.roll(acc[0], SWA_HEAD_DIM, axis=1),
                             pltpu.roll(acc[1], SWA_HEAD_DIM, axis=1))
                   + jnp.where(first_half, esink[0], esink[1]))
            o_ref[r * w:(r + 1) * w, p * LANES:(p + 1) * LANES] = (num / den).astype(o_ref.dtype)


def _swa_attention(sinks, layer, qs, ks, vs, seq):
    t = qs.shape[0]
    tq = SWA_TQ
    per_tile = tq // SWA_WINDOW
    cur = lambda n: pl.BlockSpec((tq, n), lambda i: (i, 0))
    prev = lambda n: pl.BlockSpec((SWA_WINDOW, n), lambda i: (jnp.maximum(i * per_tile - 1, 0), 0))
    return pl.pallas_call(
        functools.partial(_swa_kernel, layer=layer, tiles_per_seq=seq // tq),
        grid=(t // tq,),
        in_specs=[pl.BlockSpec(memory_space=pltpu.SMEM), cur(512), cur(256), cur(512), prev(256), prev(512)],
        out_specs=cur(512),
        out_shape=jax.ShapeDtypeStruct((t, 512), BF16),
        compiler_params=_params(("parallel",)),
        name="swa_attention",
    )(sinks, qs, ks, vs, ks, vs)


def _post_kernel(x_ref, oa_ref, ob_ref, oc_ref, gate_ref, woa_ref, wob_ref, woc_ref, wout_ref,
                 gpost_ref, gpre_ref, wup_ref, wdown_ref, gmlp_ref, out_ref):
    d = D_MODEL
    mixed = (gate_ref[:, 0:d] * _dot(oa_ref[...], woa_ref[...])
             + gate_ref[:, d:2 * d] * _dot(ob_ref[...], wob_ref[...])
             + gate_ref[:, 2 * d:3 * d] * _dot(oc_ref[...], woc_ref[...]))
    x1 = x_ref[...] + _rms(_dot(mixed.astype(BF16), wout_ref[...]), gpost_ref[...])
    h = _rms(x1, gpre_ref[...]).astype(BF16)
    u = jnp.square(jnp.maximum(_dot(h, wup_ref[...]), 0.0)).astype(BF16)
    out_ref[...] = x1 + _rms(_dot(u, wdown_ref[...]), gmlp_ref[...])


def _post(x, oa, ob, oc, gates, layer, woa, wob, woc, wout, g_post, g_pre, wup, wdown, g_mlp):
    t = x.shape[0]
    tm = POST_TM
    row = lambda n: pl.BlockSpec((tm, n), lambda i: (i, 0))
    consts = [woa, wob, woc, wout, g_post, g_pre, wup, wdown, g_mlp]
    return pl.pallas_call(
        _post_kernel,
        grid=(t // tm,),
        in_specs=[row(D_MODEL), row(512), row(512), row(512), row(N_BRANCHES * D_MODEL)]
                 + [_layer_spec(c, layer) for c in consts],
        out_specs=row(D_MODEL),
        out_shape=jax.ShapeDtypeStruct((t, D_MODEL), F32),
        compiler_params=_params(("parallel",)),
        name="post",
    )(x, oa, ob, oc, gates, *consts)


def _rot_half(w):
    half = w.shape[-1] // 2
    return jnp.concatenate([-w[..., half:], w[..., :half]], axis=-1)


def _pair_pad(a, b):
    z = jnp.zeros_like(a)
    return jnp.concatenate([a, z, z, b], axis=-1).reshape(*a.shape[:-2], -1)


def _layout_weights(w_in, w_uq, w_ukv):
    dep = w_in.shape[0]
    rope_lo = MLA_Q_LORA + MLA_KV_LORA
    k_rope = w_in[..., rope_lo:rope_lo + MLA_ROPE]
    z32 = jnp.zeros((dep, D_MODEL, 32), F32)
    rope_blk = jnp.concatenate([_rot_half(k_rope), z32, k_rope, z32], axis=-1)

    def relayout(w_ref, blk_ref, o_ref):
        o_ref[:, :rope_lo] = w_ref[:, :rope_lo].astype(BF16)
        o_ref[:, rope_lo:rope_lo + LANES] = blk_ref[...].astype(BF16)
        o_ref[:, rope_lo + LANES:] = w_ref[:, rope_lo + MLA_ROPE:].astype(BF16)

    rows = 256
    w1 = pl.pallas_call(
        relayout,
        grid=(dep, D_MODEL // rows),
        in_specs=[pl.BlockSpec((None, rows, w_in.shape[-1]), lambda l, i: (l, i, 0)),
                  pl.BlockSpec((None, rows, LANES), lambda l, i: (l, i, 0))],
        out_specs=pl.BlockSpec((None, rows, W1_COLS), lambda l, i: (l, i, 0)),
        out_shape=jax.ShapeDtypeStruct((dep, D_MODEL, W1_COLS), BF16),
        compiler_params=_params(("parallel", "parallel")),
        name="w_in_layout",
    )(w_in, rope_blk)

    uq = w_uq.astype(BF16).reshape(dep, MLA_Q_LORA, MLA_HEADS, MLA_NOPE + MLA_ROPE)
    nope, pe = uq[..., :MLA_NOPE], uq[..., MLA_NOPE:]
    pe_rot = _rot_half(pe)
    zq = jnp.zeros((dep, MLA_Q_LORA, MLA_HEADS, 32), BF16)
    wq_main = jnp.concatenate([nope, pe, zq], axis=-1).reshape(dep, MLA_Q_LORA, -1)
    wq_rot = jnp.concatenate([jnp.zeros_like(nope), pe_rot, zq], axis=-1).reshape(dep, MLA_Q_LORA, -1)
    wq = jnp.concatenate([wq_main, wq_rot], axis=-1)

    ukv = w_ukv.astype(BF16).reshape(dep, MLA_KV_LORA, MLA_HEADS, MLA_NOPE + MLA_V)
    k_nope, v = ukv[..., :MLA_NOPE], ukv[..., MLA_NOPE:]
    wk = jnp.concatenate([k_nope, jnp.zeros_like(k_nope)], axis=-1).reshape(dep, MLA_KV_LORA, -1)
    vp = v.reshape(dep, MLA_KV_LORA, HEAD_PAIRS, 2, MLA_V)
    wkv = jnp.concatenate([wk, _pair_pad(vp[..., 0, :], vp[..., 1, :])], axis=-1)
    return w1, wq, wkv


def kernel(x, positions, g_mix_pre, w_in, b_gate, g_q_lat, g_kv_lat, w_uq, w_ukv, swa_sinks, w_o_mla, w_o_swa, w_o_sb, w_out, g_mix_post, g_mlp_pre, w_up, w_down, g_mlp_post):
    batch, seq, d = x.shape
    t = batch * seq
    tables = _rope_tables(positions)
    w1, wq, wkv = _layout_weights(w_in, w_uq, w_ukv)
    woa, wob, woc, wout, wup, wdown = (w.astype(BF16) for w in (w_o_mla, w_o_swa, w_o_sb, w_out, w_up, w_down))
    tri = (lax.broadcasted_iota(jnp.int32, (SB_BLK, SB_BLK), 0)
           > lax.broadcasted_iota(jnp.int32, (SB_BLK, SB_BLK), 1)).astype(BF16)
    rows = lambda g: g.reshape(DEPTH, 1, -1)
    g_pre, bg, g_q, g_kv = rows(g_mix_pre), rows(b_gate), rows(g_q_lat), rows(g_kv_lat)
    g_post, g_mlp_in, g_mlp_out = rows(g_mix_post), rows(g_mlp_pre), rows(g_mlp_post)

    xt = x.reshape(t, d)
    for l in range(DEPTH):
        qm, km, vm, qs, ks, vs, qb, kb, vb, gates = _prep(xt, tables, l, g_pre, w1, bg, g_q, g_kv, wq, wkv)
        oa = _mla_attention(qm, km, vm, batch, seq)
        ob = _swa_attention(swa_sinks, l, qs, ks, vs, seq)
        oc = _sb_attention(qb, kb, vb, tri, batch, seq)
        xt = _post(xt, oa, ob, oc, gates, l, woa, wob, woc, wout, g_post, g_mlp_in, wup, wdown, g_mlp_out)
    return xt.reshape(batch, seq, d)
```

```python
import functools

import jax
import jax.numpy as jnp
from jax import lax
from jax.experimental import pallas as pl
from jax.experimental.pallas import tpu as pltpu

F32 = jnp.float32
BF16 = jnp.bfloat16

D_MODEL = 1024
DEPTH = 4
MLA_HEADS = 8
MLA_Q_LORA = 256
MLA_KV_LORA = 128
MLA_NOPE = 64
MLA_ROPE = 32
MLA_V = 64
SWA_HEADS = 8
SWA_KV_HEADS = 2
SWA_HEAD_DIM = 64
SWA_WINDOW = 128
SB_HEADS = 8
SB_HEAD_DIM = 64
D_FF = 4 * D_MODEL
ROPE_THETA = 10000.0
EPS = 1e-6
N_BRANCHES = 3

LANES = 128
HEAD_PAIRS = 4
MLA_HEAD_PAD = 128
NEG_BIG = -1e30
LOG2E = 1.4426950408889634
SB_SKIP = 160.0
SP_CLAMP = 126.0
VMEM_LIMIT = 56 * 1024 * 1024

OFF_CQ = 0
OFF_CKV = OFF_CQ + MLA_Q_LORA
OFF_KPE = OFF_CKV + MLA_KV_LORA
OFF_QS = OFF_KPE + LANES
OFF_KS = OFF_QS + 512
OFF_VS = OFF_KS + 128
OFF_QB = OFF_VS + 128
OFF_KB = OFF_QB + 512
OFF_VB = OFF_KB + 512
OFF_GATE = OFF_VB + 512
W1_COLS = OFF_GATE + N_BRANCHES * D_MODEL

PREP_TM = 256
POST_TM = 256
MLA_TQ = 512
MLA_TK = 512
MLA_PAIRS = 2
SB_TQ = 256
SB_TK = 256
SB_BLK = 256
SB_PAIRS = 4
SWA_TQ = 512
ROPE_TM = 2048


def _rms(x, g):
    return x * lax.rsqrt(jnp.mean(x * x, axis=-1, keepdims=True) + EPS) * g


def _dot(a, b):
    return jnp.dot(a, b, preferred_element_type=F32)


def _dot_nt(a, b):
    return lax.dot_general(a, b, (((1,), (1,)), ((), ())), preferred_element_type=F32)


def _const_spec(shape):
    return pl.BlockSpec(shape, lambda *_: (0,) * len(shape), pipeline_mode=pl.Buffered(1))


def _layer_spec(arr, layer):
    return pl.BlockSpec((None,) + arr.shape[1:], lambda *_: (layer, 0, 0), pipeline_mode=pl.Buffered(1))


def _params(sem):
    return pltpu.CompilerParams(dimension_semantics=sem, vmem_limit_bytes=VMEM_LIMIT)


def _pair_ones(width):
    lane = lax.broadcasted_iota(jnp.int32, (1, width), 1) % (2 * LANES)
    return ((lane >= 64) & (lane < 2 * LANES - 64)).astype(F32)


def _rope_table_kernel(pos_ref, inv_ref, ca_ref, sa_ref, cb_ref, sb_ref):
    ang = pos_ref[...].astype(F32) * inv_ref[...]
    cos, sin = jnp.cos(ang), jnp.sin(ang)
    lane = lax.broadcasted_iota(jnp.int32, (1, LANES), 1)
    low = lane < SWA_HEAD_DIM
    rope_lanes = (lane >= MLA_NOPE) & (lane < MLA_NOPE + MLA_ROPE)
    ca_ref[...] = jnp.where(rope_lanes, cos, jnp.where(low, 1.0, 0.0))
    sa_ref[...] = jnp.where(rope_lanes, sin, 0.0)
    cb_ref[...] = jnp.where(low, cos, pltpu.roll(cos, SWA_HEAD_DIM, axis=1))
    sb_ref[...] = jnp.where(low, sin, pltpu.roll(sin, SWA_HEAD_DIM, axis=1))


def _rope_tables(positions):
    t = positions.size
    pos = positions.reshape(t, 1)
    inv_a16 = 1.0 / (ROPE_THETA ** (jnp.arange(0, MLA_ROPE, 2, dtype=F32) / MLA_ROPE))
    inv_b32 = 1.0 / (ROPE_THETA ** (jnp.arange(0, SWA_HEAD_DIM, 2, dtype=F32) / SWA_HEAD_DIM))
    inv = jnp.concatenate([inv_b32, inv_b32, inv_a16, inv_a16, jnp.zeros((32,), F32)]).reshape(1, LANES)
    row = pl.BlockSpec((ROPE_TM, LANES), lambda i: (i, 0))
    out = jax.ShapeDtypeStruct((t, LANES), F32)
    return pl.pallas_call(
        _rope_table_kernel,
        grid=(t // ROPE_TM,),
        in_specs=[pl.BlockSpec((ROPE_TM, 1), lambda i: (i, 0)), pl.BlockSpec((1, LANES), lambda i: (0, 0))],
        out_specs=[row, row, row, row],
        out_shape=[out, out, out, out],
        compiler_params=_params(("parallel",)),
        name="rope_tables",
    )(pos, inv)


def _prep_kernel(x_ref, g_ref, ca_ref, sa_ref, cb_ref, sb_ref, w1_ref, bg_ref, gq_ref, gkv_ref,
                 wq_ref, wkv_ref,
                 qm_ref, km_ref, vm_ref, qs_ref, ks_ref, vs_ref, qb_ref, kb_ref, vb_ref, gate_ref):
    h = _rms(x_ref[...], g_ref[...]).astype(BF16)

    def mm(lo, n):
        return _dot(h, w1_ref[:, lo:lo + n])

    ca, sa, cb, sb = ca_ref[...], sa_ref[...], cb_ref[...], sb_ref[...]

    lane = lax.broadcasted_iota(jnp.int32, (1, LANES), 1)
    low_head = lane < SWA_HEAD_DIM
    first_half = lane % SWA_HEAD_DIM < SWA_HEAD_DIM // 2

    def roll(v, shift):
        return pltpu.roll(v, shift, axis=1)

    def rope64(v, cos, sin):
        half = SWA_HEAD_DIM // 2
        rot = jnp.where(first_half, -roll(v, LANES - half), roll(v, half))
        return v * cos + rot * sin

    cqn = _rms(mm(OFF_CQ, MLA_Q_LORA), gq_ref[...]).astype(BF16)
    ckvn = _rms(mm(OFF_CKV, MLA_KV_LORA), gkv_ref[...]).astype(BF16)
    kpe_blk = mm(OFF_KPE, LANES)

    q_scale = SWA_HEAD_DIM ** -0.5 * LOG2E
    qs = mm(OFF_QS, 512)
    for p in range(HEAD_PAIRS):
        blk = slice(p * LANES, (p + 1) * LANES)
        qs_ref[:, blk] = rope64(qs[:, blk], cb * q_scale, sb * q_scale).astype(BF16)
    ks = rope64(mm(OFF_KS, LANES), cb, sb)
    ks_swap = roll(ks, SWA_HEAD_DIM)
    ks_ref[:, :LANES] = jnp.where(low_head, ks, ks_swap).astype(BF16)
    ks_ref[:, LANES:] = jnp.where(low_head, ks_swap, ks).astype(BF16)
    vs = mm(OFF_VS, LANES)
    vs_swap = roll(vs, SWA_HEAD_DIM)
    for n, blk_val in enumerate([jnp.where(low_head, vs, 1.0), jnp.where(low_head, 1.0, vs_swap),
                                 jnp.where(low_head, vs_swap, 1.0), jnp.where(low_head, 1.0, vs)]):
        vs_ref[:, n * LANES:(n + 1) * LANES] = blk_val.astype(BF16)

    qb_ref[...] = (mm(OFF_QB, 512) * (SB_HEAD_DIM ** -0.5 * LOG2E)).astype(BF16)
    kb_ref[...] = mm(OFF_KB, 512).astype(BF16)
    vb_ref[...] = mm(OFF_VB, 512).astype(BF16)

    scale = (MLA_NOPE + MLA_ROPE) ** -0.5 * LOG2E
    ca8 = jnp.concatenate([ca * scale] * MLA_HEADS, axis=1)
    sa8 = jnp.concatenate([sa * scale] * MLA_HEADS, axis=1)
    nq = MLA_HEADS * MLA_HEAD_PAD
    qm_ref[...] = (_dot(cqn, wq_ref[:, :nq]) * ca8 + _dot(cqn, wq_ref[:, nq:]) * sa8).astype(BF16)
    rope_lanes = (lane >= MLA_NOPE) & (lane < MLA_NOPE + MLA_ROPE)
    kpe = jnp.where(rope_lanes, kpe_blk * ca + roll(kpe_blk, MLA_NOPE) * sa, 0.0)
    kpe8 = jnp.concatenate([kpe] * MLA_HEADS, axis=1)
    km_ref[...] = (_dot(ckvn, wkv_ref[:, :nq]) + kpe8).astype(BF16)
    vm_ref[...] = (_dot(ckvn, wkv_ref[:, nq:]) + _pair_ones(nq)).astype(BF16)

    gate_ref[...] = jax.nn.sigmoid(mm(OFF_GATE, N_BRANCHES * D_MODEL) + bg_ref[...])


def _prep(x, tables, layer, g_pre, w1, b_gate, g_q, g_kv, wq, wkv):
    t = x.shape[0]
    tm = PREP_TM
    ca, sa, cb, sb = tables

    def row(n):
        return pl.BlockSpec((tm, n), lambda i: (i, 0))

    def out(n, dt=BF16):
        return jax.ShapeDtypeStruct((t, n), dt)

    nq = MLA_HEADS * MLA_HEAD_PAD
    consts = [w1, b_gate, g_q, g_kv, wq, wkv]
    return pl.pallas_call(
        _prep_kernel,
        grid=(t // tm,),
        in_specs=[row(D_MODEL), _layer_spec(g_pre, layer), row(LANES), row(LANES), row(LANES), row(LANES)]
                 + [_layer_spec(c, layer) for c in consts],
        out_specs=[row(nq), row(nq), row(nq), row(512), row(256), row(512), row(512), row(512), row(512),
                   row(N_BRANCHES * D_MODEL)],
        out_shape=[out(nq), out(nq), out(nq), out(512), out(256), out(512), out(512), out(512), out(512),
                   out(N_BRANCHES * D_MODEL, F32)],
        compiler_params=_params(("parallel",)),
        name="prep",
    )(x, g_pre, ca, sa, cb, sb, *consts)


def _mla_kernel(q_ref, k_ref, v_ref, o_ref, m_ref, acc_ref):
    qi = pl.program_id(2)
    tq, tk = MLA_TQ, MLA_TK
    n_heads = 2 * MLA_PAIRS
    m_ref[...] = jnp.full(m_ref.shape, NEG_BIG, F32)
    acc_ref[...] = jnp.zeros(acc_ref.shape, F32)

    def head_lanes(h):
        return slice(h * MLA_HEAD_PAD, (h + 1) * MLA_HEAD_PAD)

    def scores(j):
        start = pl.multiple_of(j * tk, tk)
        k = k_ref[pl.ds(start, tk), :]
        return tuple(_dot_nt(q_ref[:, head_lanes(h)], k[:, head_lanes(h)]) for h in range(n_heads))

    def process(chunks):
        s_all = [scores(j) for j, _ in chunks]
        width = len(chunks) * tk
        start = pl.multiple_of(chunks[0][0] * tk, tk)
        v = v_ref[pl.ds(start, width), :]
        for h in range(n_heads):
            parts = []
            for (_, masked), s_pair in zip(chunks, s_all):
                s = s_pair[h]
                if masked:
                    row = lax.broadcasted_iota(jnp.int32, (tq, tk), 0)
                    col = lax.broadcasted_iota(jnp.int32, (tq, tk), 1)
                    s = jnp.where(col <= row, s, NEG_BIG)
                parts.append(s)
            s = parts[0] if len(parts) == 1 else jnp.concatenate(parts, axis=1)
            m_old = m_ref[h]
            m_new = jnp.maximum(m_old, jnp.max(s, axis=-1, keepdims=True))
            alpha = jnp.exp2(m_old - m_new)
            p = jnp.exp2(s - jnp.concatenate([m_new] * (width // LANES), axis=1))
            acc_ref[h] = alpha * acc_ref[h] + _dot(p.astype(BF16), v[:, head_lanes(h)])
            m_ref[h] = m_new

    def body(i, carry):
        process([(2 * i, False), (2 * i + 1, False)])
        return carry

    lax.fori_loop(0, qi // 2, body, 0)

    @pl.when(qi % 2 == 1)
    def _():
        process([(qi - 1, False), (qi, True)])

    @pl.when(qi % 2 == 0)
    def _():
        process([(qi, True)])

    first_half = lax.broadcasted_iota(jnp.int32, (tq, LANES), 1) < MLA_V
    for p in range(MLA_PAIRS):
        a0, a1 = acc_ref[2 * p], acc_ref[2 * p + 1]
        num = jnp.where(first_half, a0, a1)
        den = jnp.where(first_half, pltpu.roll(a0, MLA_V, axis=1), pltpu.roll(a1, MLA_V, axis=1))
        o_ref[:, p * LANES:(p + 1) * LANES] = (num / den).astype(o_ref.dtype)


def _mla_attention(qm, km, vm, batch, seq):
    t = qm.shape[0]
    tq = MLA_TQ
    nq = seq // tq
    wide = MLA_PAIRS * 2 * MLA_HEAD_PAD
    return pl.pallas_call(
        _mla_kernel,
        grid=(batch, HEAD_PAIRS // MLA_PAIRS, nq),
        in_specs=[pl.BlockSpec((tq, wide), lambda b, p, i: (b * nq + i, p)),
                  pl.BlockSpec((seq, wide), lambda b, p, i: (b, p)),
                  pl.BlockSpec((seq, wide), lambda b, p, i: (b, p))],
        out_specs=pl.BlockSpec((tq, MLA_PAIRS * LANES), lambda b, p, i: (b * nq + i, p)),
        out_shape=jax.ShapeDtypeStruct((t, HEAD_PAIRS * LANES), BF16),
        scratch_shapes=[pltpu.VMEM((2 * MLA_PAIRS, tq, LANES), F32),
                        pltpu.VMEM((2 * MLA_PAIRS, tq, LANES), F32)],
        compiler_params=_params(("parallel", "parallel", "parallel")),
        name="mla_attention",
    )(qm, km, vm)


def _sb_kernel(q_ref, k_ref, v_ref, tri_ref, o_ref, qh_ref, carry_ref, acc_ref):
    qi = pl.program_id(2)
    tq, tk, blk = SB_TQ, SB_TK, SB_BLK
    carry_ref[...] = jnp.zeros(carry_ref.shape, F32)
    acc_ref[...] = jnp.zeros(acc_ref.shape, F32)
    n_heads = 2 * SB_PAIRS
    first_half = lax.broadcasted_iota(jnp.int32, (tq, LANES), 1) < SB_HEAD_DIM
    for p in range(SB_PAIRS):
        q2 = q_ref[:, p * LANES:(p + 1) * LANES]
        zero = jnp.zeros_like(q2)
        qh_ref[2 * p] = jnp.where(first_half, q2, zero)
        qh_ref[2 * p + 1] = jnp.where(first_half, zero, q2)

    def pair_lanes(h):
        return slice((h // 2) * LANES, (h // 2 + 1) * LANES)

    def logits(j):
        start = pl.multiple_of(j * tk, tk)
        k = k_ref[pl.ds(start, tk), :]
        return tuple(_dot_nt(qh_ref[h], k[:, pair_lanes(h)]) for h in range(n_heads))

    def weights_pv(j, z_all, masked):
        start = pl.multiple_of(j * tk, tk)
        v = v_ref[pl.ds(start, tk), :]
        if masked:
            row = lax.broadcasted_iota(jnp.int32, (tq, tk), 0)
            col = lax.broadcasted_iota(jnp.int32, (tq, tk), 1)
            valid = col < row
        for h, z in enumerate(z_all):
            sp = jnp.maximum(z, jnp.log2(1.0 + jnp.exp2(jnp.minimum(z, SP_CLAMP))))
            own = z - sp
            if masked:
                sp = jnp.where(valid, sp, 0.0)
            sp16 = sp.astype(BF16)
            c = carry_ref[h]
            expo = [None] * (tk // blk)
            for b in reversed(range(tk // blk)):
                cols = slice(b * blk, (b + 1) * blk)
                later = _dot(sp16[:, cols], tri_ref[...])
                expo[b] = own[:, cols] - later - jnp.concatenate([c] * (blk // LANES), axis=1)
                c = c + jnp.sum(sp[:, cols], axis=-1, keepdims=True)
            carry_ref[h] = c
            a = jnp.exp2(jnp.concatenate(expo, axis=1))
            if masked:
                a = jnp.where(valid, a, 0.0)
            acc_ref[h] += _dot(a.astype(BF16), v[:, pair_lanes(h)])

    def process(chunks):
        z_chunks = [logits(j) for j, _ in chunks]
        for (j, masked), z_all in zip(chunks, z_chunks):
            weights_pv(j, z_all, masked)

    def min_carry():
        c = carry_ref[0]
        for h in range(1, n_heads):
            c = jnp.minimum(c, carry_ref[h])
        return jnp.min(c)

    def more(state):
        j, cmin = state
        return jnp.logical_and(j >= 0, cmin < SB_SKIP)

    def step(state):
        j, _ = state
        process([(j, False)])
        return j - 1, min_carry()

    @pl.when(qi == 0)
    def _():
        process([(qi, True)])

    @pl.when(qi > 0)
    def _():
        process([(qi, True), (qi - 1, False)])

    lax.while_loop(more, step, (qi - 2, min_carry()))

    for p in range(SB_PAIRS):
        o_ref[:, p * LANES:(p + 1) * LANES] = jnp.where(
            first_half, acc_ref[2 * p], acc_ref[2 * p + 1]).astype(o_ref.dtype)


def _sb_attention(qb, kb, vb, tri, batch, seq):
    t = qb.shape[0]
    tq = SB_TQ
    nq = seq // tq
    wide = SB_PAIRS * LANES
    return pl.pallas_call(
        _sb_kernel,
        grid=(batch, HEAD_PAIRS // SB_PAIRS, nq),
        in_specs=[pl.BlockSpec((tq, wide), lambda b, p, i: (b * nq + i, p)),
                  pl.BlockSpec((seq, wide), lambda b, p, i: (b, p)),
                  pl.BlockSpec((seq, wide), lambda b, p, i: (b, p)),
                  _const_spec(tri.shape)],
        out_specs=pl.BlockSpec((tq, wide), lambda b, p, i: (b * nq + i, p)),
        out_shape=jax.ShapeDtypeStruct((t, HEAD_PAIRS * LANES), BF16),
        scratch_shapes=[pltpu.VMEM((2 * SB_PAIRS, tq, LANES), BF16), pltpu.VMEM((2 * SB_PAIRS, tq, LANES), F32),
                        pltpu.VMEM((2 * SB_PAIRS, tq, LANES), F32)],
        compiler_params=_params(("parallel", "parallel", "parallel")),
        name="sb_attention",
    )(qb, kb, vb, tri)


def _swa_kernel(sink_ref, q_ref, k_ref, v_ref, kp_ref, vp_ref, o_ref, *, layer, tiles_per_seq):
    i = pl.program_id(0)
    w, tq = SWA_WINDOW, SWA_TQ
    has_prev = (i % tiles_per_seq) != 0
    kcat = jnp.concatenate([kp_ref[...], k_ref[...]], axis=0)
    vcat = jnp.concatenate([vp_ref[...], v_ref[...]], axis=0)
    row = lax.broadcasted_iota(jnp.int32, (w, 2 * w), 0)
    col = lax.broadcasted_iota(jnp.int32, (w, 2 * w), 1)
    band = (col > row) & (col <= row + w)
    band_first = band & ((col >= w) | has_prev)
    first_half = lax.broadcasted_iota(jnp.int32, (w, LANES), 1) < SWA_HEAD_DIM
    for p in range(HEAD_PAIRS):
        g = p // (HEAD_PAIRS // SWA_KV_HEADS)
        q2 = q_ref[:, p * LANES:(p + 1) * LANES]
        kg = kcat[:, g * LANES:(g + 1) * LANES]
        half = lax.broadcasted_iota(jnp.int32, q2.shape, 1) < SWA_HEAD_DIM
        zero = jnp.zeros_like(q2)
        qh = [jnp.where(half, q2, zero), jnp.where(half, zero, q2)]
        s_full = [_dot_nt(qh[hh], kg) for hh in range(2)]
        for r in range(tq // w):
            keys = slice(r * w, (r + 2) * w)
            acc, esink = [], []
            for hh in range(2):
                sink = sink_ref[layer, 2 * p + hh] * LOG2E
                s = jnp.where(band_first if r == 0 else band, s_full[hh][r * w:(r + 1) * w, keys], NEG_BIG)
                m = jnp.maximum(jnp.broadcast_to(jnp.max(s, axis=-1, keepdims=True), (w, LANES)), sink)
                prob = jnp.exp2(s - jnp.concatenate([m, m], axis=1))
                vh = vcat[keys, (2 * g + hh) * LANES:(2 * g + hh + 1) * LANES]
                acc.append(_dot(prob.astype(BF16), vh))
                esink.append(jnp.exp2(sink - m))
            num = jnp.where(first_half, acc[0], acc[1])
            den = (jnp.where(first_half, pltpu.roll(acc[0], SWA_HEAD_DIM, axis=1),
                             pltpu.roll(acc[1], SWA_HEAD_DIM, axis=1))
                   + jnp.where(first_half, esink[0], esink[1]))
            o_ref[r * w:(r + 1) * w, p * LANES:(p + 1) * LANES] = (num / den).astype(o_ref.dtype)


def _swa_attention(sinks, layer, qs, ks, vs, seq):
    t = qs.shape[0]
    tq = SWA_TQ
    per_tile = tq // SWA_WINDOW
    cur = lambda n: pl.BlockSpec((tq, n), lambda i: (i, 0))
    prev = lambda n: pl.BlockSpec((SWA_WINDOW, n), lambda i: (jnp.maximum(i * per_tile - 1, 0), 0))
    return pl.pallas_call(
        functools.partial(_swa_kernel, layer=layer, tiles_per_seq=seq // tq),
        grid=(t // tq,),
        in_specs=[pl.BlockSpec(memory_space=pltpu.SMEM), cur(512), cur(256), cur(512), prev(256), prev(512)],
        out_specs=cur(512),
        out_shape=jax.ShapeDtypeStruct((t, 512), BF16),
        compiler_params=_params(("parallel",)),
        name="swa_attention",
    )(sinks, qs, ks, vs, ks, vs)


def _post_kernel(x_ref, oa_ref, ob_ref, oc_ref, gate_ref, woa_ref, wob_ref, woc_ref, wout_ref,
                 gpost_ref, gpre_ref, wup_ref, wdown_ref, gmlp_ref, out_ref):
    d = D_MODEL
    mixed = (gate_ref[:, 0:d] * _dot(oa_ref[...], woa_ref[...])
             + gate_ref[:, d:2 * d] * _dot(ob_ref[...], wob_ref[...])
             + gate_ref[:, 2 * d:3 * d] * _dot(oc_ref[...], woc_ref[...]))
    x1 = x_ref[...] + _rms(_dot(mixed.astype(BF16), wout_ref[...]), gpost_ref[...])
    h = _rms(x1, gpre_ref[...]).astype(BF16)
    u = jnp.square(jnp.maximum(_dot(h, wup_ref[...]), 0.0)).astype(BF16)
    out_ref[...] = x1 + _rms(_dot(u, wdown_ref[...]), gmlp_ref[...])


def _post(x, oa, ob, oc, gates, layer, woa, wob, woc, wout, g_post, g_pre, wup, wdown, g_mlp):
    t = x.shape[0]
    tm = POST_TM
    row = lambda n: pl.BlockSpec((tm, n), lambda i: (i, 0))
    consts = [woa, wob, woc, wout, g_post, g_pre, wup, wdown, g_mlp]
    return pl.pallas_call(
        _post_kernel,
        grid=(t // tm,),
        in_specs=[row(D_MODEL), row(512), row(512), row(512), row(N_BRANCHES * D_MODEL)]
                 + [_layer_spec(c, layer) for c in consts],
        out_specs=row(D_MODEL),
        out_shape=jax.ShapeDtypeStruct((t, D_MODEL), F32),
        compiler_params=_params(("parallel",)),
        name="post",
    )(x, oa, ob, oc, gates, *consts)


def _rot_half(w):
    half = w.shape[-1] // 2
    return jnp.concatenate([-w[..., half:], w[..., :half]], axis=-1)


def _pair_pad(a, b):
    z = jnp.zeros_like(a)
    return jnp.concatenate([a, z, z, b], axis=-1).reshape(*a.shape[:-2], -1)


def _layout_weights(w_in, w_uq, w_ukv):
    dep = w_in.shape[0]
    rope_lo = MLA_Q_LORA + MLA_KV_LORA
    k_rope = w_in[..., rope_lo:rope_lo + MLA_ROPE]
    z32 = jnp.zeros((dep, D_MODEL, 32), F32)
    rope_blk = jnp.concatenate([_rot_half(k_rope), z32, k_rope, z32], axis=-1)

    def relayout(w_ref, blk_ref, o_ref):
        o_ref[:, :rope_lo] = w_ref[:, :rope_lo].astype(BF16)
        o_ref[:, rope_lo:rope_lo + LANES] = blk_ref[...].astype(BF16)
        o_ref[:, rope_lo + LANES:] = w_ref[:, rope_lo + MLA_ROPE:].astype(BF16)

    rows = 256
    w1 = pl.pallas_call(
        relayout,
        grid=(dep, D_MODEL // rows),
        in_specs=[pl.BlockSpec((None, rows, w_in.shape[-1]), lambda l, i: (l, i, 0)),
                  pl.BlockSpec((None, rows, LANES), lambda l, i: (l, i, 0))],
        out_specs=pl.BlockSpec((None, rows, W1_COLS), lambda l, i: (l, i, 0)),
        out_shape=jax.ShapeDtypeStruct((dep, D_MODEL, W1_COLS), BF16),
        compiler_params=_params(("parallel", "parallel")),
        name="w_in_layout",
    )(w_in, rope_blk)

    uq = w_uq.astype(BF16).reshape(dep, MLA_Q_LORA, MLA_HEADS, MLA_NOPE + MLA_ROPE)
    nope, pe = uq[..., :MLA_NOPE], uq[..., MLA_NOPE:]
    pe_rot = _rot_half(pe)
    zq = jnp.zeros((dep, MLA_Q_LORA, MLA_HEADS, 32), BF16)
    wq_main = jnp.concatenate([nope, pe, zq], axis=-1).reshape(dep, MLA_Q_LORA, -1)
    wq_rot = jnp.concatenate([jnp.zeros_like(nope), pe_rot, zq], axis=-1).reshape(dep, MLA_Q_LORA, -1)
    wq = jnp.concatenate([wq_main, wq_rot], axis=-1)

    ukv = w_ukv.astype(BF16).reshape(dep, MLA_KV_LORA, MLA_HEADS, MLA_NOPE + MLA_V)
    k_nope, v = ukv[..., :MLA_NOPE], ukv[..., MLA_NOPE:]
    wk = jnp.concatenate([k_nope, jnp.zeros_like(k_nope)], axis=-1).reshape(dep, MLA_KV_LORA, -1)
    vp = v.reshape(dep, MLA_KV_LORA, HEAD_PAIRS, 2, MLA_V)
    wkv = jnp.concatenate([wk, _pair_pad(vp[..., 0, :], vp[..., 1, :])], axis=-1)
    return w1, wq, wkv


def kernel(x, positions, g_mix_pre, w_in, b_gate, g_q_lat, g_kv_lat, w_uq, w_ukv, swa_sinks, w_o_mla, w_o_swa, w_o_sb, w_out, g_mix_post, g_mlp_pre, w_up, w_down, g_mlp_post):
    batch, seq, d = x.shape
    t = batch * seq
    tables = _rope_tables(positions)
    w1, wq, wkv = _layout_weights(w_in, w_uq, w_ukv)
    woa, wob, woc, wout, wup, wdown = (w.astype(BF16) for w in (w_o_mla, w_o_swa, w_o_sb, w_out, w_up, w_down))
    tri = (lax.broadcasted_iota(jnp.int32, (SB_BLK, SB_BLK), 0)
           > lax.broadcasted_iota(jnp.int32, (SB_BLK, SB_BLK), 1)).astype(BF16)
    rows = lambda g: g.reshape(DEPTH, 1, -1)
    g_pre, bg, g_q, g_kv = rows(g_mix_pre), rows(b_gate), rows(g_q_lat), rows(g_kv_lat)
    g_post, g_mlp_in, g_mlp_out = rows(g_mix_post), rows(g_mlp_pre), rows(g_mlp_post)

    xt = x.reshape(t, d)
    for l in range(DEPTH):
        qm, km, vm, qs, ks, vs, qb, kb, vb, gates = _prep(xt, tables, l, g_pre, w1, bg, g_q, g_kv, wq, wkv)
        oa = _mla_attention(qm, km, vm, batch, seq)
        ob = _swa_attention(swa_sinks, l, qs, ks, vs, seq)
        oc = _sb_attention(qb, kb, vb, tri, batch, seq)
        xt = _post(xt, oa, ob, oc, gates, l, woa, wob, woc, wout, g_post, g_mlp_in, wup, wdown, g_mlp_out)
    return xt.reshape(batch, seq, d)
```

```python
import functools

import jax
import jax.numpy as jnp
from jax import lax
from jax.experimental import pallas as pl
from jax.experimental.pallas import tpu as pltpu

F32 = jnp.float32
BF16 = jnp.bfloat16

D_MODEL = 1024
DEPTH = 4
MLA_HEADS = 8
MLA_Q_LORA = 256
MLA_KV_LORA = 128
MLA_NOPE = 64
MLA_ROPE = 32
MLA_V = 64
SWA_HEADS = 8
SWA_KV_HEADS = 2
SWA_HEAD_DIM = 64
SWA_WINDOW = 128
SB_HEADS = 8
SB_HEAD_DIM = 64
D_FF = 4 * D_MODEL
ROPE_THETA = 10000.0
EPS = 1e-6
N_BRANCHES = 3

LANES = 128
HEAD_PAIRS = 4
MLA_HEAD_PAD = 128
NEG_BIG = -1e30
LOG2E = 1.4426950408889634
SB_SKIP = 160.0
SP_CLAMP = 126.0
VMEM_LIMIT = 56 * 1024 * 1024

OFF_CQ = 0
OFF_CKV = OFF_CQ + MLA_Q_LORA
OFF_KPE = OFF_CKV + MLA_KV_LORA
OFF_QS = OFF_KPE + LANES
OFF_KS = OFF_QS + 512
OFF_VS = OFF_KS + 128
OFF_QB = OFF_VS + 128
OFF_KB = OFF_QB + 512
OFF_VB = OFF_KB + 512
OFF_GATE = OFF_VB + 512
W1_COLS = OFF_GATE + N_BRANCHES * D_MODEL

PREP_TM = 512
POST_TM = 256
MLA_TQ = 512
MLA_TK = 512
MLA_PAIRS = 2
SB_TQ = 256
SB_TK = 256
SB_BLK = 256
SB_PAIRS = 4
SB_LOOKAHEAD = 16
SWA_TQ = 512
ROPE_TM = 2048


def _rms(x, g):
    return x * lax.rsqrt(jnp.mean(x * x, axis=-1, keepdims=True) + EPS) * g


def _dot(a, b):
    return jnp.dot(a, b, preferred_element_type=F32)


def _dot_nt(a, b):
    return lax.dot_general(a, b, (((1,), (1,)), ((), ())), preferred_element_type=F32)


def _const_spec(shape):
    return pl.BlockSpec(shape, lambda *_: (0,) * len(shape), pipeline_mode=pl.Buffered(1))


def _layer_spec(arr, layer):
    return pl.BlockSpec((None,) + arr.shape[1:], lambda *_: (layer, 0, 0), pipeline_mode=pl.Buffered(1))


def _params(sem):
    return pltpu.CompilerParams(dimension_semantics=sem, vmem_limit_bytes=VMEM_LIMIT)


def _pair_ones(width):
    lane = lax.broadcasted_iota(jnp.int32, (1, width), 1) % (2 * LANES)
    return ((lane >= 64) & (lane < 2 * LANES - 64)).astype(F32)


def _rope_table_kernel(pos_ref, inv_ref, ca_ref, sa_ref, cb_ref, sb_ref):
    ang = pos_ref[...].astype(F32) * inv_ref[...]
    cos, sin = jnp.cos(ang), jnp.sin(ang)
    lane = lax.broadcasted_iota(jnp.int32, (1, LANES), 1)
    low = lane < SWA_HEAD_DIM
    rope_lanes = (lane >= MLA_NOPE) & (lane < MLA_NOPE + MLA_ROPE)
    ca_ref[...] = jnp.where(rope_lanes, cos, jnp.where(low, 1.0, 0.0))
    sa_ref[...] = jnp.where(rope_lanes, sin, 0.0)
    cb_ref[...] = jnp.where(low, cos, pltpu.roll(cos, SWA_HEAD_DIM, axis=1))
    sb_ref[...] = jnp.where(low, sin, pltpu.roll(sin, SWA_HEAD_DIM, axis=1))


def _rope_tables(positions):
    t = positions.size
    pos = positions.reshape(t, 1)
    inv_a16 = 1.0 / (ROPE_THETA ** (jnp.arange(0, MLA_ROPE, 2, dtype=F32) / MLA_ROPE))
    inv_b32 = 1.0 / (ROPE_THETA ** (jnp.arange(0, SWA_HEAD_DIM, 2, dtype=F32) / SWA_HEAD_DIM))
    inv = jnp.concatenate([inv_b32, inv_b32, inv_a16, inv_a16, jnp.zeros((32,), F32)]).reshape(1, LANES)
    row = pl.BlockSpec((ROPE_TM, LANES), lambda i: (i, 0))
    out = jax.ShapeDtypeStruct((t, LANES), F32)
    return pl.pallas_call(
        _rope_table_kernel,
        grid=(t // ROPE_TM,),
        in_specs=[pl.BlockSpec((ROPE_TM, 1), lambda i: (i, 0)), pl.BlockSpec((1, LANES), lambda i: (0, 0))],
        out_specs=[row, row, row, row],
        out_shape=[out, out, out, out],
        compiler_params=_params(("parallel",)),
        name="rope_tables",
    )(pos, inv)


def _prep_kernel(x_ref, g_ref, ca_ref, sa_ref, cb_ref, sb_ref, w1_ref, bg_ref, gq_ref, gkv_ref,
                 wq_ref, wkv_ref,
                 qm_ref, km_ref, vm_ref, qs_ref, ks_ref, vs_ref, qb_ref, kb_ref, vb_ref, gate_ref):
    h = _rms(x_ref[...], g_ref[...]).astype(BF16)

    def mm(lo, n):
        return _dot(h, w1_ref[:, lo:lo + n])

    ca, sa, cb, sb = ca_ref[...], sa_ref[...], cb_ref[...], sb_ref[...]

    lane = lax.broadcasted_iota(jnp.int32, (1, LANES), 1)
    low_head = lane < SWA_HEAD_DIM
    first_half = lane % SWA_HEAD_DIM < SWA_HEAD_DIM // 2

    def roll(v, shift):
        return pltpu.roll(v, shift, axis=1)

    def rope64(v, cos, sin):
        half = SWA_HEAD_DIM // 2
        rot = jnp.where(first_half, -roll(v, LANES - half), roll(v, half))
        return v * cos + rot * sin

    cqn = _rms(mm(OFF_CQ, MLA_Q_LORA), gq_ref[...]).astype(BF16)
    ckvn = _rms(mm(OFF_CKV, MLA_KV_LORA), gkv_ref[...]).astype(BF16)
    kpe_blk = mm(OFF_KPE, LANES)

    q_scale = SWA_HEAD_DIM ** -0.5 * LOG2E
    qs = mm(OFF_QS, 512)
    for p in range(HEAD_PAIRS):
        blk = slice(p * LANES, (p + 1) * LANES)
        qs_ref[:, blk] = rope64(qs[:, blk], cb * q_scale, sb * q_scale).astype(BF16)
    ks = rope64(mm(OFF_KS, LANES), cb, sb)
    ks_swap = roll(ks, SWA_HEAD_DIM)
    ks_ref[:, :LANES] = jnp.where(low_head, ks, ks_swap).astype(BF16)
    ks_ref[:, LANES:] = jnp.where(low_head, ks_swap, ks).astype(BF16)
    vs = mm(OFF_VS, LANES)
    vs_swap = roll(vs, SWA_HEAD_DIM)
    for n, blk_val in enumerate([jnp.where(low_head, vs, 1.0), jnp.where(low_head, 1.0, vs_swap),
                                 jnp.where(low_head, vs_swap, 1.0), jnp.where(low_head, 1.0, vs)]):
        vs_ref[:, n * LANES:(n + 1) * LANES] = blk_val.astype(BF16)

    qb_ref[...] = (mm(OFF_QB, 512) * (SB_HEAD_DIM ** -0.5 * LOG2E)).astype(BF16)
    kb_ref[...] = mm(OFF_KB, 512).astype(BF16)
    vb_ref[...] = mm(OFF_VB, 512).astype(BF16)

    scale = (MLA_NOPE + MLA_ROPE) ** -0.5 * LOG2E
    ca8 = jnp.concatenate([ca * scale] * MLA_HEADS, axis=1)
    sa8 = jnp.concatenate([sa * scale] * MLA_HEADS, axis=1)
    nq = MLA_HEADS * MLA_HEAD_PAD
    qm_ref[...] = (_dot(cqn, wq_ref[:, :nq]) * ca8 + _dot(cqn, wq_ref[:, nq:]) * sa8).astype(BF16)
    rope_lanes = (lane >= MLA_NOPE) & (lane < MLA_NOPE + MLA_ROPE)
    kpe = jnp.where(rope_lanes, kpe_blk * ca + roll(kpe_blk, MLA_NOPE) * sa, 0.0)
    kpe8 = jnp.concatenate([kpe] * MLA_HEADS, axis=1)
    km_ref[...] = (_dot(ckvn, wkv_ref[:, :nq]) + kpe8).astype(BF16)
    vm_ref[...] = (_dot(ckvn, wkv_ref[:, nq:]) + _pair_ones(nq)).astype(BF16)

    gate_ref[...] = jax.nn.sigmoid(mm(OFF_GATE, N_BRANCHES * D_MODEL) + bg_ref[...])


def _prep(x, tables, layer, g_pre, w1, b_gate, g_q, g_kv, wq, wkv):
    t = x.shape[0]
    tm = PREP_TM
    ca, sa, cb, sb = tables

    def row(n):
        return pl.BlockSpec((tm, n), lambda i: (i, 0))

    def out(n, dt=BF16):
        return jax.ShapeDtypeStruct((t, n), dt)

    nq = MLA_HEADS * MLA_HEAD_PAD
    consts = [w1, b_gate, g_q, g_kv, wq, wkv]
    return pl.pallas_call(
        _prep_kernel,
        grid=(t // tm,),
        in_specs=[row(D_MODEL), _layer_spec(g_pre, layer), row(LANES), row(LANES), row(LANES), row(LANES)]
                 + [_layer_spec(c, layer) for c in consts],
        out_specs=[row(nq), row(nq), row(nq), row(512), row(256), row(512), row(512), row(512), row(512),
                   row(N_BRANCHES * D_MODEL)],
        out_shape=[out(nq), out(nq), out(nq), out(512), out(256), out(512), out(512), out(512), out(512),
                   out(N_BRANCHES * D_MODEL, F32)],
        compiler_params=_params(("parallel",)),
        name="prep",
    )(x, g_pre, ca, sa, cb, sb, *consts)


def _mla_kernel(q_ref, k_ref, v_ref, o_ref, m_ref, acc_ref):
    qi = pl.program_id(2)
    tq, tk = MLA_TQ, MLA_TK
    n_heads = 2 * MLA_PAIRS
    m_ref[...] = jnp.full(m_ref.shape, NEG_BIG, F32)
    acc_ref[...] = jnp.zeros(acc_ref.shape, F32)

    def head_lanes(h):
        return slice(h * MLA_HEAD_PAD, (h + 1) * MLA_HEAD_PAD)

    def process(chunks):
        keys = [k_ref[pl.ds(pl.multiple_of(j * tk, tk), tk), :] for j, _ in chunks]

        def scores(h):
            return [_dot_nt(q_ref[:, head_lanes(h)], k[:, head_lanes(h)]) for k in keys]

        width = len(chunks) * tk
        start = pl.multiple_of(chunks[0][0] * tk, tk)
        v = v_ref[pl.ds(start, width), :]
        ahead = scores(0)
        for h in range(n_heads):
            s_head, ahead = ahead, (scores(h + 1) if h + 1 < n_heads else None)
            parts = []
            for (_, masked), s in zip(chunks, s_head):
                if masked:
                    row = lax.broadcasted_iota(jnp.int32, (tq, tk), 0)
                    col = lax.broadcasted_iota(jnp.int32, (tq, tk), 1)
                    s = jnp.where(col <= row, s, NEG_BIG)
                parts.append(s)
            s = parts[0] if len(parts) == 1 else jnp.concatenate(parts, axis=1)
            m_old = m_ref[h]
            m_new = jnp.maximum(m_old, jnp.max(s, axis=-1, keepdims=True))
            alpha = jnp.exp2(m_old - m_new)
            p = jnp.exp2(s - jnp.concatenate([m_new] * (width // LANES), axis=1))
            acc_ref[h] = alpha * acc_ref[h] + _dot(p.astype(BF16), v[:, head_lanes(h)])
            m_ref[h] = m_new

    def body(i, carry):
        process([(2 * i, False), (2 * i + 1, False)])
        return carry

    lax.fori_loop(0, qi // 2, body, 0)

    @pl.when(qi % 2 == 1)
    def _():
        process([(qi - 1, False), (qi, True)])

    @pl.when(qi % 2 == 0)
    def _():
        process([(qi, True)])

    first_half = lax.broadcasted_iota(jnp.int32, (tq, LANES), 1) < MLA_V
    for p in range(MLA_PAIRS):
        a0, a1 = acc_ref[2 * p], acc_ref[2 * p + 1]
        num = jnp.where(first_half, a0, a1)
        den = jnp.where(first_half, pltpu.roll(a0, MLA_V, axis=1), pltpu.roll(a1, MLA_V, axis=1))
        o_ref[:, p * LANES:(p + 1) * LANES] = (num / den).astype(o_ref.dtype)


def _mla_attention(qm, km, vm, batch, seq):
    t = qm.shape[0]
    tq = MLA_TQ
    nq = seq // tq
    wide = MLA_PAIRS * 2 * MLA_HEAD_PAD
    return pl.pallas_call(
        _mla_kernel,
        grid=(batch, HEAD_PAIRS // MLA_PAIRS, nq),
        in_specs=[pl.BlockSpec((tq, wide), lambda b, p, i: (b * nq + i, p)),
                  pl.BlockSpec((seq, wide), lambda b, p, i: (b, p)),
                  pl.BlockSpec((seq, wide), lambda b, p, i: (b, p))],
        out_specs=pl.BlockSpec((tq, MLA_PAIRS * LANES), lambda b, p, i: (b * nq + i, p)),
        out_shape=jax.ShapeDtypeStruct((t, HEAD_PAIRS * LANES), BF16),
        scratch_shapes=[pltpu.VMEM((2 * MLA_PAIRS, tq, LANES), F32),
                        pltpu.VMEM((2 * MLA_PAIRS, tq, LANES), F32)],
        compiler_params=_params(("parallel", "parallel", "parallel")),
        name="mla_attention",
    )(qm, km, vm)


def _sb_kernel(q_ref, k_ref, v_ref, tri_ref, o_ref, qh_ref, carry_ref, acc_ref):
    qi = pl.program_id(2)
    tq, tk, blk = SB_TQ, SB_TK, SB_BLK
    carry_ref[...] = jnp.zeros(carry_ref.shape, F32)
    acc_ref[...] = jnp.zeros(acc_ref.shape, F32)
    n_heads = 2 * SB_PAIRS
    first_half = lax.broadcasted_iota(jnp.int32, (tq, LANES), 1) < SB_HEAD_DIM
    for p in range(SB_PAIRS):
        q2 = q_ref[:, p * LANES:(p + 1) * LANES]
        zero = jnp.zeros_like(q2)
        qh_ref[2 * p] = jnp.where(first_half, q2, zero)
        qh_ref[2 * p + 1] = jnp.where(first_half, zero, q2)

    def pair_lanes(h):
        return slice((h // 2) * LANES, (h // 2 + 1) * LANES)

    def weights_pv(h, z, v, masked):
        sp = jnp.maximum(z, jnp.log2(1.0 + jnp.exp2(jnp.minimum(z, SP_CLAMP))))
        own = z - sp
        if masked:
            row = lax.broadcasted_iota(jnp.int32, (tq, tk), 0)
            col = lax.broadcasted_iota(jnp.int32, (tq, tk), 1)
            valid = col < row
            sp = jnp.where(valid, sp, 0.0)
        sp16 = sp.astype(BF16)
        c = carry_ref[h]
        expo = [None] * (tk // blk)
        for b in reversed(range(tk // blk)):
            cols = slice(b * blk, (b + 1) * blk)
            later = _dot(sp16[:, cols], tri_ref[...])
            expo[b] = own[:, cols] - later - jnp.concatenate([c] * (blk // LANES), axis=1)
            c = c + jnp.sum(sp[:, cols], axis=-1, keepdims=True)
        carry_ref[h] = c
        a = jnp.exp2(jnp.concatenate(expo, axis=1))
        if masked:
            a = jnp.where(valid, a, 0.0)
        acc_ref[h] += _dot(a.astype(BF16), v)

    def process(chunks):
        starts = [pl.multiple_of(j * tk, tk) for j, _ in chunks]
        keys = [k_ref[pl.ds(st, tk), :] for st in starts]
        vals = [v_ref[pl.ds(st, tk), :] for st in starts]
        units = [(c, h) for c in range(len(chunks)) for h in range(n_heads)]

        def logits(unit):
            c, h = unit
            return _dot_nt(qh_ref[h], keys[c][:, pair_lanes(h)])

        pending = [logits(u) for u in units[:SB_LOOKAHEAD]]
        for i, (c, h) in enumerate(units):
            z = pending.pop(0)
            if i + SB_LOOKAHEAD < len(units):
                pending.append(logits(units[i + SB_LOOKAHEAD]))
            weights_pv(h, z, vals[c][:, pair_lanes(h)], chunks[c][1])

    def min_carry():
        c = carry_ref[0]
        for h in range(1, n_heads):
            c = jnp.minimum(c, carry_ref[h])
        return jnp.min(c)

    def more(state):
        j, cmin = state
        return jnp.logical_and(j >= 0, cmin < SB_SKIP)

    def step(state):
        j, _ = state
        process([(j, False)])
        return j - 1, min_carry()

    @pl.when(qi == 0)
    def _():
        process([(qi, True)])

    @pl.when(qi > 0)
    def _():
        process([(qi, True), (qi - 1, False)])

    lax.while_loop(more, step, (qi - 2, min_carry()))

    for p in range(SB_PAIRS):
        o_ref[:, p * LANES:(p + 1) * LANES] = jnp.where(
            first_half, acc_ref[2 * p], acc_ref[2 * p + 1]).astype(o_ref.dtype)


def _sb_attention(qb, kb, vb, tri, batch, seq):
    t = qb.shape[0]
    tq = SB_TQ
    nq = seq // tq
    wide = SB_PAIRS * LANES
    return pl.pallas_call(
        _sb_kernel,
        grid=(batch, HEAD_PAIRS // SB_PAIRS, nq),
        in_specs=[pl.BlockSpec((tq, wide), lambda b, p, i: (b * nq + i, p)),
                  pl.BlockSpec((seq, wide), lambda b, p, i: (b, p)),
                  pl.BlockSpec((seq, wide), lambda b, p, i: (b, p)),
                  _const_spec(tri.shape)],
        out_specs=pl.BlockSpec((tq, wide), lambda b, p, i: (b * nq + i, p)),
        out_shape=jax.ShapeDtypeStruct((t, HEAD_PAIRS * LANES), BF16),
        scratch_shapes=[pltpu.VMEM((2 * SB_PAIRS, tq, LANES), BF16), pltpu.VMEM((2 * SB_PAIRS, tq, LANES), F32),
                        pltpu.VMEM((2 * SB_PAIRS, tq, LANES), F32)],
        compiler_params=_params(("parallel", "parallel", "parallel")),
        name="sb_attention",
    )(qb, kb, vb, tri)


def _swa_kernel(sink_ref, q_ref, k_ref, v_ref, kp_ref, vp_ref, o_ref, *, layer, tiles_per_seq):
    i = pl.program_id(0)
    w, tq = SWA_WINDOW, SWA_TQ
    has_prev = (i % tiles_per_seq) != 0
    kcat = jnp.concatenate([kp_ref[...], k_ref[...]], axis=0)
    vcat = jnp.concatenate([vp_ref[...], v_ref[...]], axis=0)
    row = lax.broadcasted_iota(jnp.int32, (w, 2 * w), 0)
    col = lax.broadcasted_iota(jnp.int32, (w, 2 * w), 1)
    band = (col > row) & (col <= row + w)
    band_first = band & ((col >= w) | has_prev)
    first_half = lax.broadcasted_iota(jnp.int32, (w, LANES), 1) < SWA_HEAD_DIM
    for p in range(HEAD_PAIRS):
        g = p // (HEAD_PAIRS // SWA_KV_HEADS)
        q2 = q_ref[:, p * LANES:(p + 1) * LANES]
        kg = kcat[:, g * LANES:(g + 1) * LANES]
        half = lax.broadcasted_iota(jnp.int32, q2.shape, 1) < SWA_HEAD_DIM
        zero = jnp.zeros_like(q2)
        qh = [jnp.where(half, q2, zero), jnp.where(half, zero, q2)]
        s_full = [_dot_nt(qh[hh], kg) for hh in range(2)]
        for r in range(tq // w):
            keys = slice(r * w, (r + 2) * w)
            acc, esink = [], []
            for hh in range(2):
                sink = sink_ref[layer, 2 * p + hh] * LOG2E
                s = jnp.where(band_first if r == 0 else band, s_full[hh][r * w:(r + 1) * w, keys], NEG_BIG)
                m = jnp.maximum(jnp.broadcast_to(jnp.max(s, axis=-1, keepdims=True), (w, LANES)), sink)
                prob = jnp.exp2(s - jnp.concatenate([m, m], axis=1))
                vh = vcat[keys, (2 * g + hh) * LANES:(2 * g + hh + 1) * LANES]
                acc.append(_dot(prob.astype(BF16), vh))
                esink.append(jnp.exp2(sink - m))
            num = jnp.where(first_half, acc[0], acc[1])
            den = (jnp.where(first_half, pltpu.roll(acc[0], SWA_HEAD_DIM, axis=1),
                             pltpu.roll(acc[1], SWA_HEAD_DIM, axis=1))
                   + jnp.where(first_half, esink[0], esink[1]))
            o_ref[r * w:(r + 1) * w, p * LANES:(p + 1) * LANES] = (num / den).astype(o_ref.dtype)


def _swa_attention(sinks, layer, qs, ks, vs, seq):
    t = qs.shape[0]
    tq = SWA_TQ
    per_tile = tq // SWA_WINDOW
    cur = lambda n: pl.BlockSpec((tq, n), lambda i: (i, 0))
    prev = lambda n: pl.BlockSpec((SWA_WINDOW, n), lambda i: (jnp.maximum(i * per_tile - 1, 0), 0))
    return pl.pallas_call(
        functools.partial(_swa_kernel, layer=layer, tiles_per_seq=seq // tq),
        grid=(t // tq,),
        in_specs=[pl.BlockSpec(memory_space=pltpu.SMEM), cur(512), cur(256), cur(512), prev(256), prev(512)],
        out_specs=cur(512),
        out_shape=jax.ShapeDtypeStruct((t, 512), BF16),
        compiler_params=_params(("parallel",)),
        name="swa_attention",
    )(sinks, qs, ks, vs, ks, vs)


def _post_kernel(x_ref, oa_ref, ob_ref, oc_ref, gate_ref, woa_ref, wob_ref, woc_ref, wout_ref,
                 gpost_ref, gpre_ref, wup_ref, wdown_ref, gmlp_ref, out_ref):
    d = D_MODEL
    tm = x_ref.shape[0]
    halves = [slice(0, tm // 2), slice(tm // 2, tm)]
    merged = []
    for r in halves:
        mixed = (gate_ref[r, 0:d] * _dot(oa_ref[r, :], woa_ref[...])
                 + gate_ref[r, d:2 * d] * _dot(ob_ref[r, :], wob_ref[...])
                 + gate_ref[r, 2 * d:3 * d] * _dot(oc_ref[r, :], woc_ref[...]))
        merged.append(_dot(mixed.astype(BF16), wout_ref[...]))
    x1 = [x_ref[r, :] + _rms(y, gpost_ref[...]) for r, y in zip(halves, merged)]
    hidden = [_rms(v, gpre_ref[...]).astype(BF16) for v in x1]
    up = [jnp.square(jnp.maximum(_dot(h, wup_ref[...]), 0.0)).astype(BF16) for h in hidden]
    down = [_dot(u, wdown_ref[...]) for u in up]
    for r, v, y in zip(halves, x1, down):
        out_ref[r, :] = v + _rms(y, gmlp_ref[...])


def _post(x, oa, ob, oc, gates, layer, woa, wob, woc, wout, g_post, g_pre, wup, wdown, g_mlp):
    t = x.shape[0]
    tm = POST_TM
    row = lambda n: pl.BlockSpec((tm, n), lambda i: (i, 0))
    consts = [woa, wob, woc, wout, g_post, g_pre, wup, wdown, g_mlp]
    return pl.pallas_call(
        _post_kernel,
        grid=(t // tm,),
        in_specs=[row(D_MODEL), row(512), row(512), row(512), row(N_BRANCHES * D_MODEL)]
                 + [_layer_spec(c, layer) for c in consts],
        out_specs=row(D_MODEL),
        out_shape=jax.ShapeDtypeStruct((t, D_MODEL), F32),
        compiler_params=_params(("parallel",)),
        name="post",
    )(x, oa, ob, oc, gates, *consts)


def _rot_half(w):
    half = w.shape[-1] // 2
    return jnp.concatenate([-w[..., half:], w[..., :half]], axis=-1)


def _pair_pad(a, b):
    z = jnp.zeros_like(a)
    return jnp.concatenate([a, z, z, b], axis=-1).reshape(*a.shape[:-2], -1)


def _layout_weights(w_in, w_uq, w_ukv):
    dep = w_in.shape[0]
    rope_lo = MLA_Q_LORA + MLA_KV_LORA
    k_rope = w_in[..., rope_lo:rope_lo + MLA_ROPE]
    z32 = jnp.zeros((dep, D_MODEL, 32), F32)
    rope_blk = jnp.concatenate([_rot_half(k_rope), z32, k_rope, z32], axis=-1)

    def relayout(w_ref, blk_ref, o_ref):
        o_ref[:, :rope_lo] = w_ref[:, :rope_lo].astype(BF16)
        o_ref[:, rope_lo:rope_lo + LANES] = blk_ref[...].astype(BF16)
        o_ref[:, rope_lo + LANES:] = w_ref[:, rope_lo + MLA_ROPE:].astype(BF16)

    rows = 256
    w1 = pl.pallas_call(
        relayout,
        grid=(dep, D_MODEL // rows),
        in_specs=[pl.BlockSpec((None, rows, w_in.shape[-1]), lambda l, i: (l, i, 0)),
                  pl.BlockSpec((None, rows, LANES), lambda l, i: (l, i, 0))],
        out_specs=pl.BlockSpec((None, rows, W1_COLS), lambda l, i: (l, i, 0)),
        out_shape=jax.ShapeDtypeStruct((dep, D_MODEL, W1_COLS), BF16),
        compiler_params=_params(("parallel", "parallel")),
        name="w_in_layout",
    )(w_in, rope_blk)

    uq = w_uq.astype(BF16).reshape(dep, MLA_Q_LORA, MLA_HEADS, MLA_NOPE + MLA_ROPE)
    nope, pe = uq[..., :MLA_NOPE], uq[..., MLA_NOPE:]
    pe_rot = _rot_half(pe)
    zq = jnp.zeros((dep, MLA_Q_LORA, MLA_HEADS, 32), BF16)
    wq_main = jnp.concatenate([nope, pe, zq], axis=-1).reshape(dep, MLA_Q_LORA, -1)
    wq_rot = jnp.concatenate([jnp.zeros_like(nope), pe_rot, zq], axis=-1).reshape(dep, MLA_Q_LORA, -1)
    wq = jnp.concatenate([wq_main, wq_rot], axis=-1)

    ukv = w_ukv.astype(BF16).reshape(dep, MLA_KV_LORA, MLA_HEADS, MLA_NOPE + MLA_V)
    k_nope, v = ukv[..., :MLA_NOPE], ukv[..., MLA_NOPE:]
    wk = jnp.concatenate([k_nope, jnp.zeros_like(k_nope)], axis=-1).reshape(dep, MLA_KV_LORA, -1)
    vp = v.reshape(dep, MLA_KV_LORA, HEAD_PAIRS, 2, MLA_V)
    wkv = jnp.concatenate([wk, _pair_pad(vp[..., 0, :], vp[..., 1, :])], axis=-1)
    return w1, wq, wkv


def kernel(x, positions, g_mix_pre, w_in, b_gate, g_q_lat, g_kv_lat, w_uq, w_ukv, swa_sinks, w_o_mla, w_o_swa, w_o_sb, w_out, g_mix_post, g_mlp_pre, w_up, w_down, g_mlp_post):
    batch, seq, d = x.shape
    t = batch * seq
    tables = _rope_tables(positions)
    w1, wq, wkv = _layout_weights(w_in, w_uq, w_ukv)
    woa, wob, woc, wout, wup, wdown = (w.astype(BF16) for w in (w_o_mla, w_o_swa, w_o_sb, w_out, w_up, w_down))
    tri = (lax.broadcasted_iota(jnp.int32, (SB_BLK, SB_BLK), 0)
           > lax.broadcasted_iota(jnp.int32, (SB_BLK, SB_BLK), 1)).astype(BF16)
    rows = lambda g: g.reshape(DEPTH, 1, -1)
    g_pre, bg, g_q, g_kv = rows(g_mix_pre), rows(b_gate), rows(g_q_lat), rows(g_kv_lat)
    g_post, g_mlp_in, g_mlp_out = rows(g_mix_post), rows(g_mlp_pre), rows(g_mlp_post)

    xt = x.reshape(t, d)
    for l in range(DEPTH):
        qm, km, vm, qs, ks, vs, qb, kb, vb, gates = _prep(xt, tables, l, g_pre, w1, bg, g_q, g_kv, wq, wkv)
        oa = _mla_attention(qm, km, vm, batch, seq)
        ob = _swa_attention(swa_sinks, l, qs, ks, vs, seq)
        oc = _sb_attention(qb, kb, vb, tri, batch, seq)
        xt = _post(xt, oa, ob, oc, gates, l, woa, wob, woc, wout, g_post, g_mlp_in, wup, wdown, g_mlp_out)
    return xt.reshape(batch, seq, d)
```

```python
import functools

import jax
import jax.numpy as jnp
from jax import lax
from jax.experimental import pallas as pl
from jax.experimental.pallas import tpu as pltpu

F32 = jnp.float32
BF16 = jnp.bfloat16

D_MODEL = 1024
DEPTH = 4
MLA_HEADS = 8
MLA_Q_LORA = 256
MLA_KV_LORA = 128
MLA_NOPE = 64
MLA_ROPE = 32
MLA_V = 64
SWA_HEADS = 8
SWA_KV_HEADS = 2
SWA_HEAD_DIM = 64
SWA_WINDOW = 128
SB_HEADS = 8
SB_HEAD_DIM = 64
D_FF = 4 * D_MODEL
ROPE_THETA = 10000.0
EPS = 1e-6
N_BRANCHES = 3

LANES = 128
HEAD_PAIRS = 4
MLA_HEAD_PAD = 128
NEG_BIG = -1e30
LOG2E = 1.4426950408889634
SB_SKIP = 160.0
SP_CLAMP = 126.0
VMEM_LIMIT = 56 * 1024 * 1024

OFF_CQ = 0
OFF_CKV = OFF_CQ + MLA_Q_LORA
OFF_KROPE = OFF_CKV + MLA_KV_LORA
OFF_QS = OFF_KROPE + MLA_ROPE
OFF_KS = OFF_QS + 512
OFF_VS = OFF_KS + 128
OFF_QB = OFF_VS + 128
OFF_KB = OFF_QB + 512
OFF_VB = OFF_KB + 512
OFF_GATE = OFF_VB + 512
W1_COLS = OFF_GATE + N_BRANCHES * D_MODEL

PREP_TM = 256
POST_TM = 256
MLA_TQ = 512
MLA_TK = 512
MLA_PAIRS = 2
SB_TQ = 256
SB_TK = 256
SB_BLK = 256
SB_PAIRS = 4
SB_LOOKAHEAD = 16
SWA_TQ = 512
ROPE_TM = 2048


def _rms(x, g):
    return x * lax.rsqrt(jnp.mean(x * x, axis=-1, keepdims=True) + EPS) * g


def _dot(a, b):
    return jnp.dot(a, b, preferred_element_type=F32)


def _dot_nt(a, b):
    return lax.dot_general(a, b, (((1,), (1,)), ((), ())), preferred_element_type=F32)


def _const_spec(shape):
    return pl.BlockSpec(shape, lambda *_: (0,) * len(shape), pipeline_mode=pl.Buffered(1))


def _layer_spec(arr, layer):
    return pl.BlockSpec((None,) + arr.shape[1:], lambda *_: (layer, 0, 0), pipeline_mode=pl.Buffered(1))


def _params(sem):
    return pltpu.CompilerParams(dimension_semantics=sem, vmem_limit_bytes=VMEM_LIMIT)


def _pair_ones(width):
    lane = lax.broadcasted_iota(jnp.int32, (1, width), 1) % (2 * LANES)
    return ((lane >= 64) & (lane < 2 * LANES - 64)).astype(F32)


def _rope_table_kernel(pos_ref, inv_ref, ca_ref, sa_ref, cb_ref, sb_ref):
    ang = pos_ref[...].astype(F32) * inv_ref[...]
    cos, sin = jnp.cos(ang), jnp.sin(ang)
    lane = lax.broadcasted_iota(jnp.int32, (1, LANES), 1)
    low = lane < SWA_HEAD_DIM
    rope_lanes = (lane >= MLA_NOPE) & (lane < MLA_NOPE + MLA_ROPE)
    ca_ref[...] = jnp.where(rope_lanes, cos, jnp.where(low, 1.0, 0.0))
    sa_ref[...] = jnp.where(rope_lanes, sin, 0.0)
    cb_ref[...] = jnp.where(low, cos, pltpu.roll(cos, SWA_HEAD_DIM, axis=1))
    sb_ref[...] = jnp.where(low, sin, pltpu.roll(sin, SWA_HEAD_DIM, axis=1))


def _rope_tables(positions):
    t = positions.size
    pos = positions.reshape(t, 1)
    inv_a16 = 1.0 / (ROPE_THETA ** (jnp.arange(0, MLA_ROPE, 2, dtype=F32) / MLA_ROPE))
    inv_b32 = 1.0 / (ROPE_THETA ** (jnp.arange(0, SWA_HEAD_DIM, 2, dtype=F32) / SWA_HEAD_DIM))
    inv = jnp.concatenate([inv_b32, inv_b32, inv_a16, inv_a16, jnp.zeros((32,), F32)]).reshape(1, LANES)
    row = pl.BlockSpec((ROPE_TM, LANES), lambda i: (i, 0))
    out = jax.ShapeDtypeStruct((t, LANES), F32)
    return pl.pallas_call(
        _rope_table_kernel,
        grid=(t // ROPE_TM,),
        in_specs=[pl.BlockSpec((ROPE_TM, 1), lambda i: (i, 0)), pl.BlockSpec((1, LANES), lambda i: (0, 0))],
        out_specs=[row, row, row, row],
        out_shape=[out, out, out, out],
        compiler_params=_params(("parallel",)),
        name="rope_tables",
    )(pos, inv)


def _prep_kernel(x_ref, g_ref, ca_ref, sa_ref, cb_ref, sb_ref, w1_ref, wrope_ref, bg_ref, gq_ref, gkv_ref,
                 wq_ref, wkv_ref,
                 qm_ref, km_ref, vm_ref, qs_ref, ks_ref, vs_ref, qb_ref, kb_ref, vb_ref, gate_ref):
    h = _rms(x_ref[...], g_ref[...]).astype(BF16)

    def mm(lo, n):
        return _dot_nt(h, w1_ref[lo:lo + n, :])

    ca, sa, cb, sb = ca_ref[...], sa_ref[...], cb_ref[...], sb_ref[...]

    lane = lax.broadcasted_iota(jnp.int32, (1, LANES), 1)
    low_head = lane < SWA_HEAD_DIM
    first_half = lane % SWA_HEAD_DIM < SWA_HEAD_DIM // 2

    def roll(v, shift):
        return pltpu.roll(v, shift, axis=1)

    def rope64(v, cos, sin):
        half = SWA_HEAD_DIM // 2
        rot = jnp.where(first_half, -roll(v, LANES - half), roll(v, half))
        return v * cos + rot * sin

    cqn = _rms(mm(OFF_CQ, MLA_Q_LORA), gq_ref[...]).astype(BF16)
    ckvn = _rms(mm(OFF_CKV, MLA_KV_LORA), gkv_ref[...]).astype(BF16)
    kpe_blk = _dot_nt(h, wrope_ref[...])

    q_scale = SWA_HEAD_DIM ** -0.5 * LOG2E
    qs = mm(OFF_QS, 512)
    for p in range(HEAD_PAIRS):
        blk = slice(p * LANES, (p + 1) * LANES)
        qs_ref[:, blk] = rope64(qs[:, blk], cb * q_scale, sb * q_scale).astype(BF16)
    ks = rope64(mm(OFF_KS, LANES), cb, sb)
    ks_swap = roll(ks, SWA_HEAD_DIM)
    ks_ref[:, :LANES] = jnp.where(low_head, ks, ks_swap).astype(BF16)
    ks_ref[:, LANES:] = jnp.where(low_head, ks_swap, ks).astype(BF16)
    vs = mm(OFF_VS, LANES)
    vs_swap = roll(vs, SWA_HEAD_DIM)
    for n, blk_val in enumerate([jnp.where(low_head, vs, 1.0), jnp.where(low_head, 1.0, vs_swap),
                                 jnp.where(low_head, vs_swap, 1.0), jnp.where(low_head, 1.0, vs)]):
        vs_ref[:, n * LANES:(n + 1) * LANES] = blk_val.astype(BF16)

    qb_ref[...] = (mm(OFF_QB, 512) * (SB_HEAD_DIM ** -0.5 * LOG2E)).astype(BF16)
    kb_ref[...] = mm(OFF_KB, 512).astype(BF16)
    vb_ref[...] = mm(OFF_VB, 512).astype(BF16)

    scale = (MLA_NOPE + MLA_ROPE) ** -0.5 * LOG2E
    ca8 = jnp.concatenate([ca * scale] * MLA_HEADS, axis=1)
    sa8 = jnp.concatenate([sa * scale] * MLA_HEADS, axis=1)
    nq = MLA_HEADS * MLA_HEAD_PAD
    qm_ref[...] = (_dot(cqn, wq_ref[:, :nq]) * ca8 + _dot(cqn, wq_ref[:, nq:]) * sa8).astype(BF16)
    rope_lanes = (lane >= MLA_NOPE) & (lane < MLA_NOPE + MLA_ROPE)
    kpe = jnp.where(rope_lanes, kpe_blk * ca + roll(kpe_blk, MLA_NOPE) * sa, 0.0)
    kpe8 = jnp.concatenate([kpe] * MLA_HEADS, axis=1)
    km_ref[...] = (_dot(ckvn, wkv_ref[:, :nq]) + kpe8).astype(BF16)
    vm_ref[...] = (_dot(ckvn, wkv_ref[:, nq:]) + _pair_ones(nq)).astype(BF16)

    gate_ref[...] = jax.nn.sigmoid(mm(OFF_GATE, N_BRANCHES * D_MODEL) + bg_ref[...])


def _prep(x, tables, layer, g_pre, w1, wrope, b_gate, g_q, g_kv, wq, wkv):
    t = x.shape[0]
    tm = PREP_TM
    ca, sa, cb, sb = tables

    def row(n):
        return pl.BlockSpec((tm, n), lambda i: (i, 0))

    def out(n, dt=BF16):
        return jax.ShapeDtypeStruct((t, n), dt)

    nq = MLA_HEADS * MLA_HEAD_PAD
    consts = [w1, wrope, b_gate, g_q, g_kv, wq, wkv]
    return pl.pallas_call(
        _prep_kernel,
        grid=(t // tm,),
        in_specs=[row(D_MODEL), _layer_spec(g_pre, layer), row(LANES), row(LANES), row(LANES), row(LANES)]
                 + [_layer_spec(c, layer) for c in consts],
        out_specs=[row(nq), row(nq), row(nq), row(512), row(256), row(512), row(512), row(512), row(512),
                   row(N_BRANCHES * D_MODEL)],
        out_shape=[out(nq), out(nq), out(nq), out(512), out(256), out(512), out(512), out(512), out(512),
                   out(N_BRANCHES * D_MODEL, F32)],
        compiler_params=_params(("parallel",)),
        name="prep",
    )(x, g_pre, ca, sa, cb, sb, *consts)


def _mla_kernel(q_ref, k_ref, v_ref, o_ref, m_ref, acc_ref):
    qi = pl.program_id(2)
    tq, tk = MLA_TQ, MLA_TK
    n_heads = 2 * MLA_PAIRS
    m_ref[...] = jnp.full(m_ref.shape, NEG_BIG, F32)
    acc_ref[...] = jnp.zeros(acc_ref.shape, F32)

    def head_lanes(h):
        return slice(h * MLA_HEAD_PAD, (h + 1) * MLA_HEAD_PAD)

    def process(chunks):
        keys = [k_ref[pl.ds(pl.multiple_of(j * tk, tk), tk), :] for j, _ in chunks]

        def scores(h):
            return [_dot_nt(q_ref[:, head_lanes(h)], k[:, head_lanes(h)]) for k in keys]

        width = len(chunks) * tk
        start = pl.multiple_of(chunks[0][0] * tk, tk)
        v = v_ref[pl.ds(start, width), :]
        ahead = scores(0)
        for h in range(n_heads):
            s_head, ahead = ahead, (scores(h + 1) if h + 1 < n_heads else None)
            parts = []
            for (_, masked), s in zip(chunks, s_head):
                if masked:
                    row = lax.broadcasted_iota(jnp.int32, (tq, tk), 0)
                    col = lax.broadcasted_iota(jnp.int32, (tq, tk), 1)
                    s = jnp.where(col <= row, s, NEG_BIG)
                parts.append(s)
            s = parts[0] if len(parts) == 1 else jnp.concatenate(parts, axis=1)
            m_old = m_ref[h]
            m_new = jnp.maximum(m_old, jnp.max(s, axis=-1, keepdims=True))
            alpha = jnp.exp2(m_old - m_new)
            p = jnp.exp2(s - jnp.concatenate([m_new] * (width // LANES), axis=1))
            acc_ref[h] = alpha * acc_ref[h] + _dot(p.astype(BF16), v[:, head_lanes(h)])
            m_ref[h] = m_new

    def body(i, carry):
        process([(2 * i, False), (2 * i + 1, False)])
        return carry

    lax.fori_loop(0, qi // 2, body, 0)

    @pl.when(qi % 2 == 1)
    def _():
        process([(qi - 1, False), (qi, True)])

    @pl.when(qi % 2 == 0)
    def _():
        process([(qi, True)])

    first_half = lax.broadcasted_iota(jnp.int32, (tq, LANES), 1) < MLA_V
    for p in range(MLA_PAIRS):
        a0, a1 = acc_ref[2 * p], acc_ref[2 * p + 1]
        num = jnp.where(first_half, a0, a1)
        den = jnp.where(first_half, pltpu.roll(a0, MLA_V, axis=1), pltpu.roll(a1, MLA_V, axis=1))
        o_ref[:, p * LANES:(p + 1) * LANES] = (num / den).astype(o_ref.dtype)


def _mla_attention(qm, km, vm, batch, seq):
    t = qm.shape[0]
    tq = MLA_TQ
    nq = seq // tq
    wide = MLA_PAIRS * 2 * MLA_HEAD_PAD
    return pl.pallas_call(
        _mla_kernel,
        grid=(batch, HEAD_PAIRS // MLA_PAIRS, nq),
        in_specs=[pl.BlockSpec((tq, wide), lambda b, p, i: (b * nq + i, p)),
                  pl.BlockSpec((seq, wide), lambda b, p, i: (b, p)),
                  pl.BlockSpec((seq, wide), lambda b, p, i: (b, p))],
        out_specs=pl.BlockSpec((tq, MLA_PAIRS * LANES), lambda b, p, i: (b * nq + i, p)),
        out_shape=jax.ShapeDtypeStruct((t, HEAD_PAIRS * LANES), BF16),
        scratch_shapes=[pltpu.VMEM((2 * MLA_PAIRS, tq, LANES), F32),
                        pltpu.VMEM((2 * MLA_PAIRS, tq, LANES), F32)],
        compiler_params=_params(("parallel", "parallel", "parallel")),
        name="mla_attention",
    )(qm, km, vm)


def _sb_kernel(q_ref, k_ref, v_ref, tri_ref, o_ref, qh_ref, carry_ref, acc_ref):
    qi = pl.program_id(2)
    tq, tk, blk = SB_TQ, SB_TK, SB_BLK
    carry_ref[...] = jnp.zeros(carry_ref.shape, F32)
    acc_ref[...] = jnp.zeros(acc_ref.shape, F32)
    n_heads = 2 * SB_PAIRS
    first_half = lax.broadcasted_iota(jnp.int32, (tq, LANES), 1) < SB_HEAD_DIM
    for p in range(SB_PAIRS):
        q2 = q_ref[:, p * LANES:(p + 1) * LANES]
        zero = jnp.zeros_like(q2)
        qh_ref[2 * p] = jnp.where(first_half, q2, zero)
        qh_ref[2 * p + 1] = jnp.where(first_half, zero, q2)

    def pair_lanes(h):
        return slice((h // 2) * LANES, (h // 2 + 1) * LANES)

    def weights_pv(h, z, v, masked):
        sp = jnp.maximum(z, jnp.log2(1.0 + jnp.exp2(jnp.minimum(z, SP_CLAMP))))
        own = z - sp
        if masked:
            row = lax.broadcasted_iota(jnp.int32, (tq, tk), 0)
            col = lax.broadcasted_iota(jnp.int32, (tq, tk), 1)
            valid = col < row
            sp = jnp.where(valid, sp, 0.0)
        sp16 = sp.astype(BF16)
        c = carry_ref[h]
        expo = [None] * (tk // blk)
        for b in reversed(range(tk // blk)):
            cols = slice(b * blk, (b + 1) * blk)
            later = _dot(sp16[:, cols], tri_ref[...])
            expo[b] = own[:, cols] - later - jnp.concatenate([c] * (blk // LANES), axis=1)
            c = c + jnp.sum(sp[:, cols], axis=-1, keepdims=True)
        carry_ref[h] = c
        a = jnp.exp2(jnp.concatenate(expo, axis=1))
        if masked:
            a = jnp.where(valid, a, 0.0)
        acc_ref[h] += _dot(a.astype(BF16), v)

    def process(chunks):
        starts = [pl.multiple_of(j * tk, tk) for j, _ in chunks]
        keys = [k_ref[pl.ds(st, tk), :] for st in starts]
        vals = [v_ref[pl.ds(st, tk), :] for st in starts]
        units = [(c, h) for c in range(len(chunks)) for h in range(n_heads)]

        def logits(unit):
            c, h = unit
            return _dot_nt(qh_ref[h], keys[c][:, pair_lanes(h)])

        pending = [logits(u) for u in units[:SB_LOOKAHEAD]]
        for i, (c, h) in enumerate(units):
            z = pending.pop(0)
            if i + SB_LOOKAHEAD < len(units):
                pending.append(logits(units[i + SB_LOOKAHEAD]))
            weights_pv(h, z, vals[c][:, pair_lanes(h)], chunks[c][1])

    def min_carry():
        c = carry_ref[0]
        for h in range(1, n_heads):
            c = jnp.minimum(c, carry_ref[h])
        return jnp.min(c)

    def more(state):
        j, cmin = state
        return jnp.logical_and(j >= 0, cmin < SB_SKIP)

    def step(state):
        j, _ = state
        process([(j, False)])
        return j - 1, min_carry()

    @pl.when(qi == 0)
    def _():
        process([(qi, True)])

    @pl.when(qi > 0)
    def _():
        process([(qi, True), (qi - 1, False)])

    lax.while_loop(more, step, (qi - 2, min_carry()))

    for p in range(SB_PAIRS):
        o_ref[:, p * LANES:(p + 1) * LANES] = jnp.where(
            first_half, acc_ref[2 * p], acc_ref[2 * p + 1]).astype(o_ref.dtype)


def _sb_attention(qb, kb, vb, tri, batch, seq):
    t = qb.shape[0]
    tq = SB_TQ
    nq = seq // tq
    wide = SB_PAIRS * LANES
    return pl.pallas_call(
        _sb_kernel,
        grid=(batch, HEAD_PAIRS // SB_PAIRS, nq),
        in_specs=[pl.BlockSpec((tq, wide), lambda b, p, i: (b * nq + i, p)),
                  pl.BlockSpec((seq, wide), lambda b, p, i: (b, p)),
                  pl.BlockSpec((seq, wide), lambda b, p, i: (b, p)),
                  _const_spec(tri.shape)],
        out_specs=pl.BlockSpec((tq, wide), lambda b, p, i: (b * nq + i, p)),
        out_shape=jax.ShapeDtypeStruct((t, HEAD_PAIRS * LANES), BF16),
        scratch_shapes=[pltpu.VMEM((2 * SB_PAIRS, tq, LANES), BF16), pltpu.VMEM((2 * SB_PAIRS, tq, LANES), F32),
                        pltpu.VMEM((2 * SB_PAIRS, tq, LANES), F32)],
        compiler_params=_params(("parallel", "parallel", "parallel")),
        name="sb_attention",
    )(qb, kb, vb, tri)


def _swa_kernel(sink_ref, q_ref, k_ref, v_ref, kp_ref, vp_ref, o_ref, *, layer, tiles_per_seq):
    i = pl.program_id(0)
    w, tq = SWA_WINDOW, SWA_TQ
    has_prev = (i % tiles_per_seq) != 0
    kcat = jnp.concatenate([kp_ref[...], k_ref[...]], axis=0)
    vcat = jnp.concatenate([vp_ref[...], v_ref[...]], axis=0)
    row = lax.broadcasted_iota(jnp.int32, (w, 2 * w), 0)
    col = lax.broadcasted_iota(jnp.int32, (w, 2 * w), 1)
    band = (col > row) & (col <= row + w)
    band_first = band & ((col >= w) | has_prev)
    first_half = lax.broadcasted_iota(jnp.int32, (w, LANES), 1) < SWA_HEAD_DIM
    for p in range(HEAD_PAIRS):
        g = p // (HEAD_PAIRS // SWA_KV_HEADS)
        q2 = q_ref[:, p * LANES:(p + 1) * LANES]
        kg = kcat[:, g * LANES:(g + 1) * LANES]
        half = lax.broadcasted_iota(jnp.int32, q2.shape, 1) < SWA_HEAD_DIM
        zero = jnp.zeros_like(q2)
        qh = [jnp.where(half, q2, zero), jnp.where(half, zero, q2)]
        s_full = [_dot_nt(qh[hh], kg) for hh in range(2)]
        for r in range(tq // w):
            keys = slice(r * w, (r + 2) * w)
            acc, esink = [], []
            for hh in range(2):
                sink = sink_ref[layer, 2 * p + hh] * LOG2E
                s = jnp.where(band_first if r == 0 else band, s_full[hh][r * w:(r + 1) * w, keys], NEG_BIG)
                m = jnp.maximum(jnp.broadcast_to(jnp.max(s, axis=-1, keepdims=True), (w, LANES)), sink)
                prob = jnp.exp2(s - jnp.concatenate([m, m], axis=1))
                vh = vcat[keys, (2 * g + hh) * LANES:(2 * g + hh + 1) * LANES]
                acc.append(_dot(prob.astype(BF16), vh))
                esink.append(jnp.exp2(sink - m))
            num = jnp.where(first_half, acc[0], acc[1])
            den = (jnp.where(first_half, pltpu.roll(acc[0], SWA_HEAD_DIM, axis=1),
                             pltpu.roll(acc[1], SWA_HEAD_DIM, axis=1))
                   + jnp.where(first_half, esink[0], esink[1]))
            o_ref[r * w:(r + 1) * w, p * LANES:(p + 1) * LANES] = (num / den).astype(o_ref.dtype)


def _swa_attention(sinks, layer, qs, ks, vs, seq):
    t = qs.shape[0]
    tq = SWA_TQ
    per_tile = tq // SWA_WINDOW
    cur = lambda n: pl.BlockSpec((tq, n), lambda i: (i, 0))
    prev = lambda n: pl.BlockSpec((SWA_WINDOW, n), lambda i: (jnp.maximum(i * per_tile - 1, 0), 0))
    return pl.pallas_call(
        functools.partial(_swa_kernel, layer=layer, tiles_per_seq=seq // tq),
        grid=(t // tq,),
        in_specs=[pl.BlockSpec(memory_space=pltpu.SMEM), cur(512), cur(256), cur(512), prev(256), prev(512)],
        out_specs=cur(512),
        out_shape=jax.ShapeDtypeStruct((t, 512), BF16),
        compiler_params=_params(("parallel",)),
        name="swa_attention",
    )(sinks, qs, ks, vs, ks, vs)


def _post_kernel(x_ref, oa_ref, ob_ref, oc_ref, gate_ref, woa_ref, wob_ref, woc_ref, wout_ref,
                 gpost_ref, gpre_ref, wup_ref, wdown_ref, gmlp_ref, out_ref):
    d = D_MODEL
    tm = x_ref.shape[0]
    halves = [slice(0, tm // 2), slice(tm // 2, tm)]
    merged = []
    for r in halves:
        mixed = (gate_ref[r, 0:d] * _dot(oa_ref[r, :], woa_ref[...])
                 + gate_ref[r, d:2 * d] * _dot(ob_ref[r, :], wob_ref[...])
                 + gate_ref[r, 2 * d:3 * d] * _dot(oc_ref[r, :], woc_ref[...]))
        merged.append(_dot(mixed.astype(BF16), wout_ref[...]))
    x1 = [x_ref[r, :] + _rms(y, gpost_ref[...]) for r, y in zip(halves, merged)]
    hidden = [_rms(v, gpre_ref[...]).astype(BF16) for v in x1]
    up = [jnp.square(jnp.maximum(_dot(h, wup_ref[...]), 0.0)).astype(BF16) for h in hidden]
    down = [_dot(u, wdown_ref[...]) for u in up]
    for r, v, y in zip(halves, x1, down):
        out_ref[r, :] = v + _rms(y, gmlp_ref[...])


def _post(x, oa, ob, oc, gates, layer, woa, wob, woc, wout, g_post, g_pre, wup, wdown, g_mlp):
    t = x.shape[0]
    tm = POST_TM
    row = lambda n: pl.BlockSpec((tm, n), lambda i: (i, 0))
    consts = [woa, wob, woc, wout, g_post, g_pre, wup, wdown, g_mlp]
    return pl.pallas_call(
        _post_kernel,
        grid=(t // tm,),
        in_specs=[row(D_MODEL), row(512), row(512), row(512), row(N_BRANCHES * D_MODEL)]
                 + [_layer_spec(c, layer) for c in consts],
        out_specs=row(D_MODEL),
        out_shape=jax.ShapeDtypeStruct((t, D_MODEL), F32),
        compiler_params=_params(("parallel",)),
        name="post",
    )(x, oa, ob, oc, gates, *consts)


def _rot_half(w):
    half = w.shape[-1] // 2
    return jnp.concatenate([-w[..., half:], w[..., :half]], axis=-1)


def _pair_pad(a, b):
    z = jnp.zeros_like(a)
    return jnp.concatenate([a, z, z, b], axis=-1).reshape(*a.shape[:-2], -1)


def _layout_weights(w_in, w_uq, w_ukv):
    dep = w_in.shape[0]
    w1 = jnp.swapaxes(w_in, 1, 2).astype(BF16)
    assert w1.shape[1] == W1_COLS
    k_rope = w1[:, OFF_KROPE:OFF_KROPE + MLA_ROPE, :]
    half = MLA_ROPE // 2
    k_rope_rot = jnp.concatenate([-k_rope[:, half:], k_rope[:, :half]], axis=1)
    z32 = jnp.zeros((dep, 32, D_MODEL), BF16)
    wrope = jnp.concatenate([k_rope_rot, z32, k_rope, z32], axis=1)

    uq = w_uq.astype(BF16).reshape(dep, MLA_Q_LORA, MLA_HEADS, MLA_NOPE + MLA_ROPE)
    nope, pe = uq[..., :MLA_NOPE], uq[..., MLA_NOPE:]
    pe_rot = _rot_half(pe)
    zq = jnp.zeros((dep, MLA_Q_LORA, MLA_HEADS, 32), BF16)
    wq_main = jnp.concatenate([nope, pe, zq], axis=-1).reshape(dep, MLA_Q_LORA, -1)
    wq_rot = jnp.concatenate([jnp.zeros_like(nope), pe_rot, zq], axis=-1).reshape(dep, MLA_Q_LORA, -1)
    wq = jnp.concatenate([wq_main, wq_rot], axis=-1)

    ukv = w_ukv.astype(BF16).reshape(dep, MLA_KV_LORA, MLA_HEADS, MLA_NOPE + MLA_V)
    k_nope, v = ukv[..., :MLA_NOPE], ukv[..., MLA_NOPE:]
    wk = jnp.concatenate([k_nope, jnp.zeros_like(k_nope)], axis=-1).reshape(dep, MLA_KV_LORA, -1)
    vp = v.reshape(dep, MLA_KV_LORA, HEAD_PAIRS, 2, MLA_V)
    wkv = jnp.concatenate([wk, _pair_pad(vp[..., 0, :], vp[..., 1, :])], axis=-1)
    return w1, wrope, wq, wkv


def kernel(x, positions, g_mix_pre, w_in, b_gate, g_q_lat, g_kv_lat, w_uq, w_ukv, swa_sinks, w_o_mla, w_o_swa, w_o_sb, w_out, g_mix_post, g_mlp_pre, w_up, w_down, g_mlp_post):
    batch, seq, d = x.shape
    t = batch * seq
    tables = _rope_tables(positions)
    w1, wrope, wq, wkv = _layout_weights(w_in, w_uq, w_ukv)
    woa, wob, woc, wout, wup, wdown = (w.astype(BF16) for w in (w_o_mla, w_o_swa, w_o_sb, w_out, w_up, w_down))
    tri = (lax.broadcasted_iota(jnp.int32, (SB_BLK, SB_BLK), 0)
           > lax.broadcasted_iota(jnp.int32, (SB_BLK, SB_BLK), 1)).astype(BF16)
    rows = lambda g: g.reshape(DEPTH, 1, -1)
    g_pre, bg, g_q, g_kv = rows(g_mix_pre), rows(b_gate), rows(g_q_lat), rows(g_kv_lat)
    g_post, g_mlp_in, g_mlp_out = rows(g_mix_post), rows(g_mlp_pre), rows(g_mlp_post)

    xt = x.reshape(t, d)
    for l in range(DEPTH):
        qm, km, vm, qs, ks, vs, qb, kb, vb, gates = _prep(xt, tables, l, g_pre, w1, wrope, bg, g_q, g_kv,
                                                          wq, wkv)
        oa = _mla_attention(qm, km, vm, batch, seq)
        ob = _swa_attention(swa_sinks, l, qs, ks, vs, seq)
        oc = _sb_attention(qb, kb, vb, tri, batch, seq)
        xt = _post(xt, oa, ob, oc, gates, l, woa, wob, woc, wout, g_post, g_mlp_in, wup, wdown, g_mlp_out)
    return xt.reshape(batch, seq, d)
```

```python
import functools

import jax
import jax.numpy as jnp
from jax import lax
from jax.experimental import pallas as pl
from jax.experimental.pallas import tpu as pltpu

F32 = jnp.float32
BF16 = jnp.bfloat16

D_MODEL = 1024
DEPTH = 4
MLA_HEADS = 8
MLA_Q_LORA = 256
MLA_KV_LORA = 128
MLA_NOPE = 64
MLA_ROPE = 32
MLA_V = 64
SWA_HEADS = 8
SWA_KV_HEADS = 2
SWA_HEAD_DIM = 64
SWA_WINDOW = 128
SB_HEADS = 8
SB_HEAD_DIM = 64
D_FF = 4 * D_MODEL
ROPE_THETA = 10000.0
EPS = 1e-6
N_BRANCHES = 3

LANES = 128
HEAD_PAIRS = 4
MLA_HEAD_PAD = 128
NEG_BIG = -1e30
LOG2E = 1.4426950408889634
SB_SKIP = 136.0
SP_CLAMP = 126.0
VMEM_LIMIT = 56 * 1024 * 1024

OFF_CQ = 0
OFF_CKV = OFF_CQ + MLA_Q_LORA
OFF_KROPE = OFF_CKV + MLA_KV_LORA
OFF_QS = OFF_KROPE + MLA_ROPE
OFF_KS = OFF_QS + 512
OFF_VS = OFF_KS + 128
OFF_QB = OFF_VS + 128
OFF_KB = OFF_QB + 512
OFF_VB = OFF_KB + 512
OFF_GATE = OFF_VB + 512
W1_COLS = OFF_GATE + N_BRANCHES * D_MODEL

PREP_TM = 256
POST_TM = 256
MLA_TQ = 1024
MLA_TK = 512
MLA_PAIRS = 2
SB_TQ = 256
SB_TK = 256
SB_BLK = 256
SB_PAIRS = 4
SB_LOOKAHEAD = 16
SWA_TQ = 512
ROPE_TM = 2048


def _rms(x, g):
    return x * lax.rsqrt(jnp.mean(x * x, axis=-1, keepdims=True) + EPS) * g


def _dot(a, b):
    return jnp.dot(a, b, preferred_element_type=F32)


def _dot_nt(a, b):
    return lax.dot_general(a, b, (((1,), (1,)), ((), ())), preferred_element_type=F32)


def _const_spec(shape):
    return pl.BlockSpec(shape, lambda *_: (0,) * len(shape), pipeline_mode=pl.Buffered(1))


def _layer_spec(arr, layer):
    return pl.BlockSpec((None,) + arr.shape[1:], lambda *_: (layer, 0, 0), pipeline_mode=pl.Buffered(1))


def _params(sem):
    return pltpu.CompilerParams(dimension_semantics=sem, vmem_limit_bytes=VMEM_LIMIT)


def _pair_ones(width):
    lane = lax.broadcasted_iota(jnp.int32, (1, width), 1) % (2 * LANES)
    return ((lane >= 64) & (lane < 2 * LANES - 64)).astype(F32)


def _rope_table_kernel(pos_ref, inv_ref, ca_ref, sa_ref, cb_ref, sb_ref):
    ang = pos_ref[...].astype(F32) * inv_ref[...]
    cos, sin = jnp.cos(ang), jnp.sin(ang)
    lane = lax.broadcasted_iota(jnp.int32, (1, LANES), 1)
    low = lane < SWA_HEAD_DIM
    rope_lanes = (lane >= MLA_NOPE) & (lane < MLA_NOPE + MLA_ROPE)
    ca_ref[...] = jnp.where(rope_lanes, cos, jnp.where(low, 1.0, 0.0))
    sa_ref[...] = jnp.where(rope_lanes, sin, 0.0)
    cb_ref[...] = jnp.where(low, cos, pltpu.roll(cos, SWA_HEAD_DIM, axis=1))
    sb_ref[...] = jnp.where(low, sin, pltpu.roll(sin, SWA_HEAD_DIM, axis=1))


def _rope_tables(positions):
    t = positions.size
    pos = positions.reshape(t, 1)
    inv_a16 = 1.0 / (ROPE_THETA ** (jnp.arange(0, MLA_ROPE, 2, dtype=F32) / MLA_ROPE))
    inv_b32 = 1.0 / (ROPE_THETA ** (jnp.arange(0, SWA_HEAD_DIM, 2, dtype=F32) / SWA_HEAD_DIM))
    inv = jnp.concatenate([inv_b32, inv_b32, inv_a16, inv_a16, jnp.zeros((32,), F32)]).reshape(1, LANES)
    row = pl.BlockSpec((ROPE_TM, LANES), lambda i: (i, 0))
    out = jax.ShapeDtypeStruct((t, LANES), F32)
    return pl.pallas_call(
        _rope_table_kernel,
        grid=(t // ROPE_TM,),
        in_specs=[pl.BlockSpec((ROPE_TM, 1), lambda i: (i, 0)), pl.BlockSpec((1, LANES), lambda i: (0, 0))],
        out_specs=[row, row, row, row],
        out_shape=[out, out, out, out],
        compiler_params=_params(("parallel",)),
        name="rope_tables",
    )(pos, inv)


def _prep_kernel(x_ref, g_ref, ca_ref, sa_ref, cb_ref, sb_ref, w1_ref, wrope_ref, bg_ref, gq_ref, gkv_ref,
                 wq_ref, wkv_ref,
                 qm_ref, km_ref, vm_ref, qs_ref, ks_ref, vs_ref, qb_ref, kb_ref, vb_ref, gate_ref):
    h = _rms(x_ref[...], g_ref[...]).astype(BF16)

    def mm(lo, n):
        return _dot_nt(h, w1_ref[lo:lo + n, :])

    ca, sa, cb, sb = ca_ref[...], sa_ref[...], cb_ref[...], sb_ref[...]

    lane = lax.broadcasted_iota(jnp.int32, (1, LANES), 1)
    low_head = lane < SWA_HEAD_DIM
    first_half = lane % SWA_HEAD_DIM < SWA_HEAD_DIM // 2

    def roll(v, shift):
        return pltpu.roll(v, shift, axis=1)

    def rope64(v, cos, sin):
        half = SWA_HEAD_DIM // 2
        rot = jnp.where(first_half, -roll(v, LANES - half), roll(v, half))
        return v * cos + rot * sin

    cqn = _rms(mm(OFF_CQ, MLA_Q_LORA), gq_ref[...]).astype(BF16)
    ckvn = _rms(mm(OFF_CKV, MLA_KV_LORA), gkv_ref[...]).astype(BF16)
    kpe_blk = _dot_nt(h, wrope_ref[...])

    q_scale = SWA_HEAD_DIM ** -0.5 * LOG2E
    qs = mm(OFF_QS, 512)
    for p in range(HEAD_PAIRS):
        blk = slice(p * LANES, (p + 1) * LANES)
        qs_ref[:, blk] = rope64(qs[:, blk], cb * q_scale, sb * q_scale).astype(BF16)
    ks = rope64(mm(OFF_KS, LANES), cb, sb)
    ks_swap = roll(ks, SWA_HEAD_DIM)
    ks_ref[:, :LANES] = jnp.where(low_head, ks, ks_swap).astype(BF16)
    ks_ref[:, LANES:] = jnp.where(low_head, ks_swap, ks).astype(BF16)
    vs = mm(OFF_VS, LANES)
    vs_swap = roll(vs, SWA_HEAD_DIM)
    for n, blk_val in enumerate([jnp.where(low_head, vs, 1.0), jnp.where(low_head, 1.0, vs_swap),
                                 jnp.where(low_head, vs_swap, 1.0), jnp.where(low_head, 1.0, vs)]):
        vs_ref[:, n * LANES:(n + 1) * LANES] = blk_val.astype(BF16)

    qb_ref[...] = (mm(OFF_QB, 512) * (SB_HEAD_DIM ** -0.5 * LOG2E)).astype(BF16)
    kb_ref[...] = mm(OFF_KB, 512).astype(BF16)
    vb_ref[...] = mm(OFF_VB, 512).astype(BF16)

    scale = (MLA_NOPE + MLA_ROPE) ** -0.5 * LOG2E
    ca8 = jnp.concatenate([ca * scale] * MLA_HEADS, axis=1)
    sa8 = jnp.concatenate([sa * scale] * MLA_HEADS, axis=1)
    nq = MLA_HEADS * MLA_HEAD_PAD
    qm_ref[...] = (_dot(cqn, wq_ref[:, :nq]) * ca8 + _dot(cqn, wq_ref[:, nq:]) * sa8).astype(BF16)
    rope_lanes = (lane >= MLA_NOPE) & (lane < MLA_NOPE + MLA_ROPE)
    kpe = jnp.where(rope_lanes, kpe_blk * ca + roll(kpe_blk, MLA_NOPE) * sa, 0.0)
    kpe8 = jnp.concatenate([kpe] * MLA_HEADS, axis=1)
    km_ref[...] = (_dot(ckvn, wkv_ref[:, :nq]) + kpe8).astype(BF16)
    vm_ref[...] = (_dot(ckvn, wkv_ref[:, nq:]) + _pair_ones(nq)).astype(BF16)

    gate_ref[...] = jax.nn.sigmoid(mm(OFF_GATE, N_BRANCHES * D_MODEL) + bg_ref[...])


def _prep(x, tables, layer, g_pre, w1, wrope, b_gate, g_q, g_kv, wq, wkv):
    t = x.shape[0]
    tm = PREP_TM
    ca, sa, cb, sb = tables

    def row(n):
        return pl.BlockSpec((tm, n), lambda i: (i, 0))

    def out(n, dt=BF16):
        return jax.ShapeDtypeStruct((t, n), dt)

    nq = MLA_HEADS * MLA_HEAD_PAD
    consts = [w1, wrope, b_gate, g_q, g_kv, wq, wkv]
    return pl.pallas_call(
        _prep_kernel,
        grid=(t // tm,),
        in_specs=[row(D_MODEL), _layer_spec(g_pre, layer), row(LANES), row(LANES), row(LANES), row(LANES)]
                 + [_layer_spec(c, layer) for c in consts],
        out_specs=[row(nq), row(nq), row(nq), row(512), row(256), row(512), row(512), row(512), row(512),
                   row(N_BRANCHES * D_MODEL)],
        out_shape=[out(nq), out(nq), out(nq), out(512), out(256), out(512), out(512), out(512), out(512),
                   out(N_BRANCHES * D_MODEL, F32)],
        compiler_params=_params(("parallel",)),
        name="prep",
    )(x, g_pre, ca, sa, cb, sb, *consts)


def _mla_kernel(q_ref, k_ref, v_ref, o_ref, m_ref, acc_ref):
    qi = pl.program_id(2)
    tq, tk = MLA_TQ, MLA_TK
    n_heads = 2 * MLA_PAIRS
    m_ref[...] = jnp.full(m_ref.shape, NEG_BIG, F32)
    acc_ref[...] = jnp.zeros(acc_ref.shape, F32)

    def head_lanes(h):
        return slice(h * MLA_HEAD_PAD, (h + 1) * MLA_HEAD_PAD)

    def process(chunks):
        keys = [k_ref[pl.ds(pl.multiple_of(j * tk, tk), tk), :] for j, _ in chunks]

        def scores(h):
            return [_dot_nt(q_ref[:, head_lanes(h)], k[:, head_lanes(h)]) for k in keys]

        width = len(chunks) * tk
        start = pl.multiple_of(chunks[0][0] * tk, tk)
        v = v_ref[pl.ds(start, width), :]
        ahead = scores(0)
        for h in range(n_heads):
            s_head, ahead = ahead, (scores(h + 1) if h + 1 < n_heads else None)
            parts = []
            for (_, diag), s in zip(chunks, s_head):
                if diag is not None:
                    row = lax.broadcasted_iota(jnp.int32, (tq, tk), 0)
                    col = lax.broadcasted_iota(jnp.int32, (tq, tk), 1) + diag * tk
                    s = jnp.where(col <= row, s, NEG_BIG)
                parts.append(s)
            s = parts[0] if len(parts) == 1 else jnp.concatenate(parts, axis=1)
            m_old = m_ref[h]
            m_new = jnp.maximum(m_old, jnp.max(s, axis=-1, keepdims=True))
            alpha = jnp.exp2(m_old - m_new)
            p = jnp.exp2(s - jnp.concatenate([m_new] * (width // LANES), axis=1))
            acc_ref[h] = alpha * acc_ref[h] + _dot(p.astype(BF16), v[:, head_lanes(h)])
            m_ref[h] = m_new

    assert tq == 2 * tk

    def body(i, carry):
        process([(2 * i, None), (2 * i + 1, None)])
        return carry

    lax.fori_loop(0, qi, body, 0)
    process([(2 * qi, 0), (2 * qi + 1, 1)])

    first_half = lax.broadcasted_iota(jnp.int32, (tq, LANES), 1) < MLA_V
    for p in range(MLA_PAIRS):
        a0, a1 = acc_ref[2 * p], acc_ref[2 * p + 1]
        num = jnp.where(first_half, a0, a1)
        den = jnp.where(first_half, pltpu.roll(a0, MLA_V, axis=1), pltpu.roll(a1, MLA_V, axis=1))
        o_ref[:, p * LANES:(p + 1) * LANES] = (num / den).astype(o_ref.dtype)


def _mla_attention(qm, km, vm, batch, seq):
    t = qm.shape[0]
    tq = MLA_TQ
    nq = seq // tq
    wide = MLA_PAIRS * 2 * MLA_HEAD_PAD
    return pl.pallas_call(
        _mla_kernel,
        grid=(batch, HEAD_PAIRS // MLA_PAIRS, nq),
        in_specs=[pl.BlockSpec((tq, wide), lambda b, p, i: (b * nq + i, p)),
                  pl.BlockSpec((seq, wide), lambda b, p, i: (b, p)),
                  pl.BlockSpec((seq, wide), lambda b, p, i: (b, p))],
        out_specs=pl.BlockSpec((tq, MLA_PAIRS * LANES), lambda b, p, i: (b * nq + i, p)),
        out_shape=jax.ShapeDtypeStruct((t, HEAD_PAIRS * LANES), BF16),
        scratch_shapes=[pltpu.VMEM((2 * MLA_PAIRS, tq, LANES), F32),
                        pltpu.VMEM((2 * MLA_PAIRS, tq, LANES), F32)],
        compiler_params=_params(("parallel", "parallel", "parallel")),
        name="mla_attention",
    )(qm, km, vm)


def _sb_kernel(q_ref, k_ref, v_ref, tri_ref, o_ref, qh_ref, carry_ref, acc_ref):
    qi = pl.program_id(2)
    tq, tk, blk = SB_TQ, SB_TK, SB_BLK
    carry_ref[...] = jnp.zeros(carry_ref.shape, F32)
    acc_ref[...] = jnp.zeros(acc_ref.shape, F32)
    n_heads = 2 * SB_PAIRS
    first_half = lax.broadcasted_iota(jnp.int32, (tq, LANES), 1) < SB_HEAD_DIM
    for p in range(SB_PAIRS):
        q2 = q_ref[:, p * LANES:(p + 1) * LANES]
        zero = jnp.zeros_like(q2)
        qh_ref[2 * p] = jnp.where(first_half, q2, zero)
        qh_ref[2 * p + 1] = jnp.where(first_half, zero, q2)

    def pair_lanes(h):
        return slice((h // 2) * LANES, (h // 2 + 1) * LANES)

    def weights_pv(h, z, v, masked):
        sp = jnp.maximum(z, jnp.log2(1.0 + jnp.exp2(jnp.minimum(z, SP_CLAMP))))
        own = z - sp
        if masked:
            row = lax.broadcasted_iota(jnp.int32, (tq, tk), 0)
            col = lax.broadcasted_iota(jnp.int32, (tq, tk), 1)
            valid = col < row
            sp = jnp.where(valid, sp, 0.0)
        sp16 = sp.astype(BF16)
        c = carry_ref[h]
        expo = [None] * (tk // blk)
        for b in reversed(range(tk // blk)):
            cols = slice(b * blk, (b + 1) * blk)
            later = _dot(sp16[:, cols], tri_ref[...])
            expo[b] = own[:, cols] - later - jnp.concatenate([c] * (blk // LANES), axis=1)
            c = c + jnp.sum(sp[:, cols], axis=-1, keepdims=True)
        carry_ref[h] = c
        a = jnp.exp2(jnp.concatenate(expo, axis=1))
        if masked:
            a = jnp.where(valid, a, 0.0)
        acc_ref[h] += _dot(a.astype(BF16), v)

    def process(chunks):
        starts = [pl.multiple_of(j * tk, tk) for j, _ in chunks]
        keys = [k_ref[pl.ds(st, tk), :] for st in starts]
        vals = [v_ref[pl.ds(st, tk), :] for st in starts]
        units = [(c, h) for c in range(len(chunks)) for h in range(n_heads)]

        def logits(unit):
            c, h = unit
            return _dot_nt(qh_ref[h], keys[c][:, pair_lanes(h)])

        pending = [logits(u) for u in units[:SB_LOOKAHEAD]]
        for i, (c, h) in enumerate(units):
            z = pending.pop(0)
            if i + SB_LOOKAHEAD < len(units):
                pending.append(logits(units[i + SB_LOOKAHEAD]))
            weights_pv(h, z, vals[c][:, pair_lanes(h)], chunks[c][1])

    def min_carry():
        c = carry_ref[0]
        for h in range(1, n_heads):
            c = jnp.minimum(c, carry_ref[h])
        return jnp.min(c)

    def more(state):
        j, cmin = state
        return jnp.logical_and(j >= 0, cmin < SB_SKIP)

    def step(state):
        j, _ = state
        process([(j, False)])
        return j - 1, min_carry()

    @pl.when(qi == 0)
    def _():
        process([(qi, True)])

    @pl.when(qi > 0)
    def _():
        process([(qi, True), (qi - 1, False)])

    lax.while_loop(more, step, (qi - 2, min_carry()))

    for p in range(SB_PAIRS):
        o_ref[:, p * LANES:(p + 1) * LANES] = jnp.where(
            first_half, acc_ref[2 * p], acc_ref[2 * p + 1]).astype(o_ref.dtype)


def _sb_attention(qb, kb, vb, tri, batch, seq):
    t = qb.shape[0]
    tq = SB_TQ
    nq = seq // tq
    wide = SB_PAIRS * LANES
    return pl.pallas_call(
        _sb_kernel,
        grid=(batch, HEAD_PAIRS // SB_PAIRS, nq),
        in_specs=[pl.BlockSpec((tq, wide), lambda b, p, i: (b * nq + i, p)),
                  pl.BlockSpec((seq, wide), lambda b, p, i: (b, p)),
                  pl.BlockSpec((seq, wide), lambda b, p, i: (b, p)),
                  _const_spec(tri.shape)],
        out_specs=pl.BlockSpec((tq, wide), lambda b, p, i: (b * nq + i, p)),
        out_shape=jax.ShapeDtypeStruct((t, HEAD_PAIRS * LANES), BF16),
        scratch_shapes=[pltpu.VMEM((2 * SB_PAIRS, tq, LANES), BF16), pltpu.VMEM((2 * SB_PAIRS, tq, LANES), F32),
                        pltpu.VMEM((2 * SB_PAIRS, tq, LANES), F32)],
        compiler_params=_params(("parallel", "parallel", "parallel")),
        name="sb_attention",
    )(qb, kb, vb, tri)


def _swa_kernel(sink_ref, q_ref, k_ref, v_ref, kp_ref, vp_ref, o_ref, *, layer, tiles_per_seq):
    i = pl.program_id(0)
    w, tq = SWA_WINDOW, SWA_TQ
    has_prev = (i % tiles_per_seq) != 0
    kcat = jnp.concatenate([kp_ref[...], k_ref[...]], axis=0)
    vcat = jnp.concatenate([vp_ref[...], v_ref[...]], axis=0)
    row = lax.broadcasted_iota(jnp.int32, (w, 2 * w), 0)
    col = lax.broadcasted_iota(jnp.int32, (w, 2 * w), 1)
    band = (col > row) & (col <= row + w)
    band_first = band & ((col >= w) | has_prev)
    first_half = lax.broadcasted_iota(jnp.int32, (w, LANES), 1) < SWA_HEAD_DIM
    for p in range(HEAD_PAIRS):
        g = p // (HEAD_PAIRS // SWA_KV_HEADS)
        q2 = q_ref[:, p * LANES:(p + 1) * LANES]
        kg = kcat[:, g * LANES:(g + 1) * LANES]
        half = lax.broadcasted_iota(jnp.int32, q2.shape, 1) < SWA_HEAD_DIM
        zero = jnp.zeros_like(q2)
        qh = [jnp.where(half, q2, zero), jnp.where(half, zero, q2)]
        s_full = [_dot_nt(qh[hh], kg) for hh in range(2)]
        for r in range(tq // w):
            keys = slice(r * w, (r + 2) * w)
            acc, esink = [], []
            for hh in range(2):
                sink = sink_ref[layer, 2 * p + hh] * LOG2E
                s = jnp.where(band_first if r == 0 else band, s_full[hh][r * w:(r + 1) * w, keys], NEG_BIG)
                m = jnp.maximum(jnp.broadcast_to(jnp.max(s, axis=-1, keepdims=True), (w, LANES)), sink)
                prob = jnp.exp2(s - jnp.concatenate([m, m], axis=1))
                vh = vcat[keys, (2 * g + hh) * LANES:(2 * g + hh + 1) * LANES]
                acc.append(_dot(prob.astype(BF16), vh))
                esink.append(jnp.exp2(sink - m))
            num = jnp.where(first_half, acc[0], acc[1])
            den = (jnp.where(first_half, pltpu.roll(acc[0], SWA_HEAD_DIM, axis=1),
                             pltpu.roll(acc[1], SWA_HEAD_DIM, axis=1))
                   + jnp.where(first_half, esink[0], esink[1]))
            o_ref[r * w:(r + 1) * w, p * LANES:(p + 1) * LANES] = (num / den).astype(o_ref.dtype)


def _swa_attention(sinks, layer, qs, ks, vs, seq):
    t = qs.shape[0]
    tq = SWA_TQ
    per_tile = tq // SWA_WINDOW
    cur = lambda n: pl.BlockSpec((tq, n), lambda i: (i, 0))
    prev = lambda n: pl.BlockSpec((SWA_WINDOW, n), lambda i: (jnp.maximum(i * per_tile - 1, 0), 0))
    return pl.pallas_call(
        functools.partial(_swa_kernel, layer=layer, tiles_per_seq=seq // tq),
        grid=(t // tq,),
        in_specs=[pl.BlockSpec(memory_space=pltpu.SMEM), cur(512), cur(256), cur(512), prev(256), prev(512)],
        out_specs=cur(512),
        out_shape=jax.ShapeDtypeStruct((t, 512), BF16),
        compiler_params=_params(("parallel",)),
        name="swa_attention",
    )(sinks, qs, ks, vs, ks, vs)


def _post_kernel(x_ref, oa_ref, ob_ref, oc_ref, gate_ref, woa_ref, wob_ref, woc_ref, wout_ref,
                 gpost_ref, gpre_ref, wup_ref, wdown_ref, gmlp_ref, out_ref):
    d = D_MODEL
    tm = x_ref.shape[0]
    halves = [slice(0, tm // 2), slice(tm // 2, tm)]
    merged = []
    for r in halves:
        mixed = (gate_ref[r, 0:d] * _dot(oa_ref[r, :], woa_ref[...])
                 + gate_ref[r, d:2 * d] * _dot(ob_ref[r, :], wob_ref[...])
                 + gate_ref[r, 2 * d:3 * d] * _dot(oc_ref[r, :], woc_ref[...]))
        merged.append(_dot(mixed.astype(BF16), wout_ref[...]))
    x1 = [x_ref[r, :] + _rms(y, gpost_ref[...]) for r, y in zip(halves, merged)]
    hidden = [_rms(v, gpre_ref[...]).astype(BF16) for v in x1]
    up = [jnp.square(jnp.maximum(_dot(h, wup_ref[...]), 0.0)).astype(BF16) for h in hidden]
    down = [_dot(u, wdown_ref[...]) for u in up]
    for r, v, y in zip(halves, x1, down):
        out_ref[r, :] = v + _rms(y, gmlp_ref[...])


def _post(x, oa, ob, oc, gates, layer, woa, wob, woc, wout, g_post, g_pre, wup, wdown, g_mlp):
    t = x.shape[0]
    tm = POST_TM
    row = lambda n: pl.BlockSpec((tm, n), lambda i: (i, 0))
    consts = [woa, wob, woc, wout, g_post, g_pre, wup, wdown, g_mlp]
    return pl.pallas_call(
        _post_kernel,
        grid=(t // tm,),
        in_specs=[row(D_MODEL), row(512), row(512), row(512), row(N_BRANCHES * D_MODEL)]
                 + [_layer_spec(c, layer) for c in consts],
        out_specs=row(D_MODEL),
        out_shape=jax.ShapeDtypeStruct((t, D_MODEL), F32),
        compiler_params=_params(("parallel",)),
        name="post",
    )(x, oa, ob, oc, gates, *consts)


def _rot_half(w):
    half = w.shape[-1] // 2
    return jnp.concatenate([-w[..., half:], w[..., :half]], axis=-1)


def _pair_pad(a, b):
    z = jnp.zeros_like(a)
    return jnp.concatenate([a, z, z, b], axis=-1).reshape(*a.shape[:-2], -1)


def _layout_weights(w_in, w_uq, w_ukv):
    dep = w_in.shape[0]
    w1 = jnp.swapaxes(w_in, 1, 2).astype(BF16)
    assert w1.shape[1] == W1_COLS
    k_rope = w1[:, OFF_KROPE:OFF_KROPE + MLA_ROPE, :]
    half = MLA_ROPE // 2
    k_rope_rot = jnp.concatenate([-k_rope[:, half:], k_rope[:, :half]], axis=1)
    z32 = jnp.zeros((dep, 32, D_MODEL), BF16)
    wrope = jnp.concatenate([k_rope_rot, z32, k_rope, z32], axis=1)

    uq = w_uq.astype(BF16).reshape(dep, MLA_Q_LORA, MLA_HEADS, MLA_NOPE + MLA_ROPE)
    nope, pe = uq[..., :MLA_NOPE], uq[..., MLA_NOPE:]
    pe_rot = _rot_half(pe)
    zq = jnp.zeros((dep, MLA_Q_LORA, MLA_HEADS, 32), BF16)
    wq_main = jnp.concatenate([nope, pe, zq], axis=-1).reshape(dep, MLA_Q_LORA, -1)
    wq_rot = jnp.concatenate([jnp.zeros_like(nope), pe_rot, zq], axis=-1).reshape(dep, MLA_Q_LORA, -1)
    wq = jnp.concatenate([wq_main, wq_rot], axis=-1)

    ukv = w_ukv.astype(BF16).reshape(dep, MLA_KV_LORA, MLA_HEADS, MLA_NOPE + MLA_V)
    k_nope, v = ukv[..., :MLA_NOPE], ukv[..., MLA_NOPE:]
    wk = jnp.concatenate([k_nope, jnp.zeros_like(k_nope)], axis=-1).reshape(dep, MLA_KV_LORA, -1)
    vp = v.reshape(dep, MLA_KV_LORA, HEAD_PAIRS, 2, MLA_V)
    wkv = jnp.concatenate([wk, _pair_pad(vp[..., 0, :], vp[..., 1, :])], axis=-1)
    return w1, wrope, wq, wkv


def kernel(x, positions, g_mix_pre, w_in, b_gate, g_q_lat, g_kv_lat, w_uq, w_ukv, swa_sinks, w_o_mla, w_o_swa, w_o_sb, w_out, g_mix_post, g_mlp_pre, w_up, w_down, g_mlp_post):
    batch, seq, d = x.shape
    t = batch * seq
    tables = _rope_tables(positions)
    w1, wrope, wq, wkv = _layout_weights(w_in, w_uq, w_ukv)
    woa, wob, woc, wout, wup, wdown = (w.astype(BF16) for w in (w_o_mla, w_o_swa, w_o_sb, w_out, w_up, w_down))
    tri = (lax.broadcasted_iota(jnp.int32, (SB_BLK, SB_BLK), 0)
           > lax.broadcasted_iota(jnp.int32, (SB_BLK, SB_BLK), 1)).astype(BF16)
    rows = lambda g: g.reshape(DEPTH, 1, -1)
    g_pre, bg, g_q, g_kv = rows(g_mix_pre), rows(b_gate), rows(g_q_lat), rows(g_kv_lat)
    g_post, g_mlp_in, g_mlp_out = rows(g_mix_post), rows(g_mlp_pre), rows(g_mlp_post)

    xt = x.reshape(t, d)
    for l in range(DEPTH):
        qm, km, vm, qs, ks, vs, qb, kb, vb, gates = _prep(xt, tables, l, g_pre, w1, wrope, bg, g_q, g_kv,
                                                          wq, wkv)
        oa = _mla_attention(qm, km, vm, batch, seq)
        ob = _swa_attention(swa_sinks, l, qs, ks, vs, seq)
        oc = _sb_attention(qb, kb, vb, tri, batch, seq)
        xt = _post(xt, oa, ob, oc, gates, l, woa, wob, woc, wout, g_post, g_mlp_in, wup, wdown, g_mlp_out)
    return xt.reshape(batch, seq, d)
```

```python
import functools

import jax
import jax.numpy as jnp
from jax import lax
from jax.experimental import pallas as pl
from jax.experimental.pallas import tpu as pltpu

F32 = jnp.float32
BF16 = jnp.bfloat16

D_MODEL = 1024
DEPTH = 4
MLA_HEADS = 8
MLA_Q_LORA = 256
MLA_KV_LORA = 128
MLA_NOPE = 64
MLA_ROPE = 32
MLA_V = 64
SWA_HEADS = 8
SWA_KV_HEADS = 2
SWA_HEAD_DIM = 64
SWA_WINDOW = 128
SB_HEADS = 8
SB_HEAD_DIM = 64
D_FF = 4 * D_MODEL
ROPE_THETA = 10000.0
EPS = 1e-6
N_BRANCHES = 3

LANES = 128
HEAD_PAIRS = 4
MLA_HEAD_PAD = 128
NEG_BIG = -1e30
LOG2E = 1.4426950408889634
SB_SKIP = 136.0
SP_CLAMP = 126.0
VMEM_LIMIT = 56 * 1024 * 1024

OFF_CQ = 0
OFF_CKV = OFF_CQ + MLA_Q_LORA
OFF_KROPE = OFF_CKV + MLA_KV_LORA
OFF_QS = OFF_KROPE + MLA_ROPE
OFF_KS = OFF_QS + 512
OFF_VS = OFF_KS + 128
OFF_QB = OFF_VS + 128
OFF_KB = OFF_QB + 512
OFF_VB = OFF_KB + 512
OFF_GATE = OFF_VB + 512
W1_COLS = OFF_GATE + N_BRANCHES * D_MODEL

PREP_TM = 256
POST_TM = 256
MLA_TQ = 1024
MLA_TK = 512
MLA_PAIRS = 2
SB_TQ = 256
SB_TK = 256
SB_BLK = 256
SB_PAIRS = 4
SB_LOOKAHEAD = 16
SWA_TQ = 512
ROPE_TM = 2048


def _rms(x, g):
    return x * lax.rsqrt(jnp.mean(x * x, axis=-1, keepdims=True) + EPS) * g


def _dot(a, b):
    return jnp.dot(a, b, preferred_element_type=F32)


def _dot_nt(a, b):
    return lax.dot_general(a, b, (((1,), (1,)), ((), ())), preferred_element_type=F32)


def _const_spec(shape):
    return pl.BlockSpec(shape, lambda *_: (0,) * len(shape), pipeline_mode=pl.Buffered(1))


def _layer_spec(arr, layer):
    return pl.BlockSpec((None,) + arr.shape[1:], lambda *_: (layer, 0, 0), pipeline_mode=pl.Buffered(1))


def _params(sem):
    return pltpu.CompilerParams(dimension_semantics=sem, vmem_limit_bytes=VMEM_LIMIT)


def _pair_ones(width):
    lane = lax.broadcasted_iota(jnp.int32, (1, width), 1) % (2 * LANES)
    return ((lane >= 64) & (lane < 2 * LANES - 64)).astype(F32)


def _rope_table_kernel(pos_ref, inv_ref, ca_ref, sa_ref, cb_ref, sb_ref):
    ang = pos_ref[...].astype(F32) * inv_ref[...]
    cos, sin = jnp.cos(ang), jnp.sin(ang)
    lane = lax.broadcasted_iota(jnp.int32, (1, LANES), 1)
    low = lane < SWA_HEAD_DIM
    rope_lanes = (lane >= MLA_NOPE) & (lane < MLA_NOPE + MLA_ROPE)
    ca_ref[...] = jnp.where(rope_lanes, cos, jnp.where(low, 1.0, 0.0))
    sa_ref[...] = jnp.where(rope_lanes, sin, 0.0)
    cb_ref[...] = jnp.where(low, cos, pltpu.roll(cos, SWA_HEAD_DIM, axis=1))
    sb_ref[...] = jnp.where(low, sin, pltpu.roll(sin, SWA_HEAD_DIM, axis=1))


def _rope_tables(positions):
    t = positions.size
    pos = positions.reshape(t, 1)
    inv_a16 = 1.0 / (ROPE_THETA ** (jnp.arange(0, MLA_ROPE, 2, dtype=F32) / MLA_ROPE))
    inv_b32 = 1.0 / (ROPE_THETA ** (jnp.arange(0, SWA_HEAD_DIM, 2, dtype=F32) / SWA_HEAD_DIM))
    inv = jnp.concatenate([inv_b32, inv_b32, inv_a16, inv_a16, jnp.zeros((32,), F32)]).reshape(1, LANES)
    row = pl.BlockSpec((ROPE_TM, LANES), lambda i: (i, 0))
    out = jax.ShapeDtypeStruct((t, LANES), F32)
    return pl.pallas_call(
        _rope_table_kernel,
        grid=(t // ROPE_TM,),
        in_specs=[pl.BlockSpec((ROPE_TM, 1), lambda i: (i, 0)), pl.BlockSpec((1, LANES), lambda i: (0, 0))],
        out_specs=[row, row, row, row],
        out_shape=[out, out, out, out],
        compiler_params=_params(("parallel",)),
        name="rope_tables",
    )(pos, inv)


def _prep_kernel(x_ref, g_ref, ca_ref, sa_ref, cb_ref, sb_ref, w1_ref, wrope_ref, bg_ref, gq_ref, gkv_ref,
                 wq_ref, wkv_ref,
                 qm_ref, km_ref, vm_ref, qs_ref, ks_ref, vs_ref, qb_ref, kb_ref, vb_ref, gate_ref):
    h = _rms(x_ref[...], g_ref[...]).astype(BF16)

    def mm(lo, n):
        return _dot_nt(h, w1_ref[lo:lo + n, :])

    ca, sa, cb, sb = ca_ref[...], sa_ref[...], cb_ref[...], sb_ref[...]

    lane = lax.broadcasted_iota(jnp.int32, (1, LANES), 1)
    low_head = lane < SWA_HEAD_DIM
    first_half = lane % SWA_HEAD_DIM < SWA_HEAD_DIM // 2

    def roll(v, shift):
        return pltpu.roll(v, shift, axis=1)

    def rope64(v, cos, sin):
        half = SWA_HEAD_DIM // 2
        rot = jnp.where(first_half, -roll(v, LANES - half), roll(v, half))
        return v * cos + rot * sin

    cqn = _rms(mm(OFF_CQ, MLA_Q_LORA), gq_ref[...]).astype(BF16)
    ckvn = _rms(mm(OFF_CKV, MLA_KV_LORA), gkv_ref[...]).astype(BF16)
    kpe_blk = _dot_nt(h, wrope_ref[...])

    q_scale = SWA_HEAD_DIM ** -0.5 * LOG2E
    qs = mm(OFF_QS, 512)
    for p in range(HEAD_PAIRS):
        blk = slice(p * LANES, (p + 1) * LANES)
        qs_ref[:, blk] = rope64(qs[:, blk], cb * q_scale, sb * q_scale).astype(BF16)
    ks = rope64(mm(OFF_KS, LANES), cb, sb)
    ks_swap = roll(ks, SWA_HEAD_DIM)
    ks_ref[:, :LANES] = jnp.where(low_head, ks, ks_swap).astype(BF16)
    ks_ref[:, LANES:] = jnp.where(low_head, ks_swap, ks).astype(BF16)
    vs = mm(OFF_VS, LANES)
    vs_swap = roll(vs, SWA_HEAD_DIM)
    for n, blk_val in enumerate([jnp.where(low_head, vs, 1.0), jnp.where(low_head, 1.0, vs_swap),
                                 jnp.where(low_head, vs_swap, 1.0), jnp.where(low_head, 1.0, vs)]):
        vs_ref[:, n * LANES:(n + 1) * LANES] = blk_val.astype(BF16)

    qb_ref[...] = (mm(OFF_QB, 512) * (SB_HEAD_DIM ** -0.5 * LOG2E)).astype(BF16)
    kb_ref[...] = mm(OFF_KB, 512).astype(BF16)
    vb_ref[...] = mm(OFF_VB, 512).astype(BF16)

    scale = (MLA_NOPE + MLA_ROPE) ** -0.5 * LOG2E
    ca8 = jnp.concatenate([ca * scale] * MLA_HEADS, axis=1)
    sa8 = jnp.concatenate([sa * scale] * MLA_HEADS, axis=1)
    nq = MLA_HEADS * MLA_HEAD_PAD
    qm_ref[...] = (_dot(cqn, wq_ref[:, :nq]) * ca8 + _dot(cqn, wq_ref[:, nq:]) * sa8).astype(BF16)
    rope_lanes = (lane >= MLA_NOPE) & (lane < MLA_NOPE + MLA_ROPE)
    kpe = jnp.where(rope_lanes, kpe_blk * ca + roll(kpe_blk, MLA_NOPE) * sa, 0.0)
    kpe8 = jnp.concatenate([kpe] * MLA_HEADS, axis=1)
    km_ref[...] = (_dot(ckvn, wkv_ref[:, :nq]) + kpe8).astype(BF16)
    vm_ref[...] = (_dot(ckvn, wkv_ref[:, nq:]) + _pair_ones(nq)).astype(BF16)

    gate_ref[...] = jax.nn.sigmoid(mm(OFF_GATE, N_BRANCHES * D_MODEL) + bg_ref[...])


def _prep(x, tables, layer, g_pre, w1, wrope, b_gate, g_q, g_kv, wq, wkv):
    t = x.shape[0]
    tm = PREP_TM
    ca, sa, cb, sb = tables

    def row(n):
        return pl.BlockSpec((tm, n), lambda i: (i, 0))

    def out(n, dt=BF16):
        return jax.ShapeDtypeStruct((t, n), dt)

    nq = MLA_HEADS * MLA_HEAD_PAD
    consts = [w1, wrope, b_gate, g_q, g_kv, wq, wkv]
    return pl.pallas_call(
        _prep_kernel,
        grid=(t // tm,),
        in_specs=[row(D_MODEL), _layer_spec(g_pre, layer), row(LANES), row(LANES), row(LANES), row(LANES)]
                 + [_layer_spec(c, layer) for c in consts],
        out_specs=[row(nq), row(nq), row(nq), row(512), row(256), row(512), row(512), row(512), row(512),
                   row(N_BRANCHES * D_MODEL)],
        out_shape=[out(nq), out(nq), out(nq), out(512), out(256), out(512), out(512), out(512), out(512),
                   out(N_BRANCHES * D_MODEL, F32)],
        compiler_params=_params(("parallel",)),
        name="prep",
    )(x, g_pre, ca, sa, cb, sb, *consts)


def _mla_kernel(q_ref, k_ref, v_ref, o_ref, m_ref, acc_ref):
    qi = pl.program_id(2)
    tq, tk = MLA_TQ, MLA_TK
    n_heads = 2 * MLA_PAIRS
    m_ref[...] = jnp.full(m_ref.shape, NEG_BIG, F32)
    acc_ref[...] = jnp.zeros(acc_ref.shape, F32)

    def head_lanes(h):
        return slice(h * MLA_HEAD_PAD, (h + 1) * MLA_HEAD_PAD)

    def process(row_sets):
        units = [(rows, chunks, h) for rows, chunks in row_sets for h in range(n_heads)]

        def scores(unit):
            rows, chunks, h = unit
            return [_dot_nt(q_ref[rows, head_lanes(h)],
                            k_ref[pl.ds(pl.multiple_of(j * tk, tk), tk), head_lanes(h)]) for j, _ in chunks]

        ahead = scores(units[0])
        for n, (rows, chunks, h) in enumerate(units):
            s_unit, ahead = ahead, (scores(units[n + 1]) if n + 1 < len(units) else None)
            n_rows = rows.stop - rows.start
            parts = []
            for (_, masked), s in zip(chunks, s_unit):
                if masked:
                    row = lax.broadcasted_iota(jnp.int32, (n_rows, tk), 0)
                    col = lax.broadcasted_iota(jnp.int32, (n_rows, tk), 1)
                    s = jnp.where(col <= row, s, NEG_BIG)
                parts.append(s)
            s = parts[0] if len(parts) == 1 else jnp.concatenate(parts, axis=1)
            width = len(chunks) * tk
            start = pl.multiple_of(chunks[0][0] * tk, tk)
            v = v_ref[pl.ds(start, width), head_lanes(h)]
            m_old = m_ref[h, rows]
            m_new = jnp.maximum(m_old, jnp.max(s, axis=-1, keepdims=True))
            alpha = jnp.exp2(m_old - m_new)
            p = jnp.exp2(s - jnp.concatenate([m_new] * (width // LANES), axis=1))
            acc_ref[h, rows] = alpha * acc_ref[h, rows] + _dot(p.astype(BF16), v)
            m_ref[h, rows] = m_new

    assert tq == 2 * tk
    all_rows, upper, lower = slice(0, tq), slice(0, tk), slice(tk, tq)

    def body(i, carry):
        process([(all_rows, [(2 * i, False), (2 * i + 1, False)])])
        return carry

    lax.fori_loop(0, qi, body, 0)
    process([(upper, [(2 * qi, True)]), (lower, [(2 * qi, False), (2 * qi + 1, True)])])

    first_half = lax.broadcasted_iota(jnp.int32, (tq, LANES), 1) < MLA_V
    for p in range(MLA_PAIRS):
        a0, a1 = acc_ref[2 * p], acc_ref[2 * p + 1]
        num = jnp.where(first_half, a0, a1)
        den = jnp.where(first_half, pltpu.roll(a0, MLA_V, axis=1), pltpu.roll(a1, MLA_V, axis=1))
        o_ref[:, p * LANES:(p + 1) * LANES] = (num / den).astype(o_ref.dtype)


def _mla_attention(qm, km, vm, batch, seq):
    t = qm.shape[0]
    tq = MLA_TQ
    nq = seq // tq
    wide = MLA_PAIRS * 2 * MLA_HEAD_PAD
    return pl.pallas_call(
        _mla_kernel,
        grid=(batch, HEAD_PAIRS // MLA_PAIRS, nq),
        in_specs=[pl.BlockSpec((tq, wide), lambda b, p, i: (b * nq + i, p)),
                  pl.BlockSpec((seq, wide), lambda b, p, i: (b, p)),
                  pl.BlockSpec((seq, wide), lambda b, p, i: (b, p))],
        out_specs=pl.BlockSpec((tq, MLA_PAIRS * LANES), lambda b, p, i: (b * nq + i, p)),
        out_shape=jax.ShapeDtypeStruct((t, HEAD_PAIRS * LANES), BF16),
        scratch_shapes=[pltpu.VMEM((2 * MLA_PAIRS, tq, LANES), F32),
                        pltpu.VMEM((2 * MLA_PAIRS, tq, LANES), F32)],
        compiler_params=_params(("parallel", "parallel", "parallel")),
        name="mla_attention",
    )(qm, km, vm)


def _sb_kernel(q_ref, k_ref, v_ref, tri_ref, o_ref, qh_ref, carry_ref, acc_ref):
    qi = pl.program_id(2)
    tq, tk, blk = SB_TQ, SB_TK, SB_BLK
    carry_ref[...] = jnp.zeros(carry_ref.shape, F32)
    acc_ref[...] = jnp.zeros(acc_ref.shape, F32)
    n_heads = 2 * SB_PAIRS
    first_half = lax.broadcasted_iota(jnp.int32, (tq, LANES), 1) < SB_HEAD_DIM
    for p in range(SB_PAIRS):
        q2 = q_ref[:, p * LANES:(p + 1) * LANES]
        zero = jnp.zeros_like(q2)
        qh_ref[2 * p] = jnp.where(first_half, q2, zero)
        qh_ref[2 * p + 1] = jnp.where(first_half, zero, q2)

    def pair_lanes(h):
        return slice((h // 2) * LANES, (h // 2 + 1) * LANES)

    def weights_pv(h, z, v, masked):
        sp = jnp.maximum(z, jnp.log2(1.0 + jnp.exp2(jnp.minimum(z, SP_CLAMP))))
        own = z - sp
        if masked:
            row = lax.broadcasted_iota(jnp.int32, (tq, tk), 0)
            col = lax.broadcasted_iota(jnp.int32, (tq, tk), 1)
            valid = col < row
            sp = jnp.where(valid, sp, 0.0)
        sp16 = sp.astype(BF16)
        c = carry_ref[h]
        expo = [None] * (tk // blk)
        for b in reversed(range(tk // blk)):
            cols = slice(b * blk, (b + 1) * blk)
            later = _dot(sp16[:, cols], tri_ref[...])
            expo[b] = own[:, cols] - later - jnp.concatenate([c] * (blk // LANES), axis=1)
            c = c + jnp.sum(sp[:, cols], axis=-1, keepdims=True)
        carry_ref[h] = c
        a = jnp.exp2(jnp.concatenate(expo, axis=1))
        if masked:
            a = jnp.where(valid, a, 0.0)
        acc_ref[h] += _dot(a.astype(BF16), v)

    def process(chunks):
        starts = [pl.multiple_of(j * tk, tk) for j, _ in chunks]
        keys = [k_ref[pl.ds(st, tk), :] for st in starts]
        vals = [v_ref[pl.ds(st, tk), :] for st in starts]
        units = [(c, h) for c in range(len(chunks)) for h in range(n_heads)]

        def logits(unit):
            c, h = unit
            return _dot_nt(qh_ref[h], keys[c][:, pair_lanes(h)])

        pending = [logits(u) for u in units[:SB_LOOKAHEAD]]
        for i, (c, h) in enumerate(units):
            z = pending.pop(0)
            if i + SB_LOOKAHEAD < len(units):
                pending.append(logits(units[i + SB_LOOKAHEAD]))
            weights_pv(h, z, vals[c][:, pair_lanes(h)], chunks[c][1])

    def min_carry():
        c = carry_ref[0]
        for h in range(1, n_heads):
            c = jnp.minimum(c, carry_ref[h])
        return jnp.min(c)

    def more(state):
        j, cmin = state
        return jnp.logical_and(j >= 0, cmin < SB_SKIP)

    def step(state):
        j, _ = state
        process([(j, False)])
        return j - 1, min_carry()

    @pl.when(qi == 0)
    def _():
        process([(qi, True)])

    @pl.when(qi > 0)
    def _():
        process([(qi, True), (qi - 1, False)])

    lax.while_loop(more, step, (qi - 2, min_carry()))

    for p in range(SB_PAIRS):
        o_ref[:, p * LANES:(p + 1) * LANES] = jnp.where(
            first_half, acc_ref[2 * p], acc_ref[2 * p + 1]).astype(o_ref.dtype)


def _sb_attention(qb, kb, vb, tri, batch, seq):
    t = qb.shape[0]
    tq = SB_TQ
    nq = seq // tq
    wide = SB_PAIRS * LANES
    return pl.pallas_call(
        _sb_kernel,
        grid=(batch, HEAD_PAIRS // SB_PAIRS, nq),
        in_specs=[pl.BlockSpec((tq, wide), lambda b, p, i: (b * nq + i, p)),
                  pl.BlockSpec((seq, wide), lambda b, p, i: (b, p)),
                  pl.BlockSpec((seq, wide), lambda b, p, i: (b, p)),
                  _const_spec(tri.shape)],
        out_specs=pl.BlockSpec((tq, wide), lambda b, p, i: (b * nq + i, p)),
        out_shape=jax.ShapeDtypeStruct((t, HEAD_PAIRS * LANES), BF16),
        scratch_shapes=[pltpu.VMEM((2 * SB_PAIRS, tq, LANES), BF16), pltpu.VMEM((2 * SB_PAIRS, tq, LANES), F32),
                        pltpu.VMEM((2 * SB_PAIRS, tq, LANES), F32)],
        compiler_params=_params(("parallel", "parallel", "parallel")),
        name="sb_attention",
    )(qb, kb, vb, tri)


def _swa_kernel(sink_ref, q_ref, k_ref, v_ref, kp_ref, vp_ref, o_ref, *, layer, tiles_per_seq):
    i = pl.program_id(0)
    w, tq = SWA_WINDOW, SWA_TQ
    has_prev = (i % tiles_per_seq) != 0
    kcat = jnp.concatenate([kp_ref[...], k_ref[...]], axis=0)
    vcat = jnp.concatenate([vp_ref[...], v_ref[...]], axis=0)
    row = lax.broadcasted_iota(jnp.int32, (w, 2 * w), 0)
    col = lax.broadcasted_iota(jnp.int32, (w, 2 * w), 1)
    band = (col > row) & (col <= row + w)
    band_first = band & ((col >= w) | has_prev)
    first_half = lax.broadcasted_iota(jnp.int32, (w, LANES), 1) < SWA_HEAD_DIM
    for p in range(HEAD_PAIRS):
        g = p // (HEAD_PAIRS // SWA_KV_HEADS)
        q2 = q_ref[:, p * LANES:(p + 1) * LANES]
        kg = kcat[:, g * LANES:(g + 1) * LANES]
        half = lax.broadcasted_iota(jnp.int32, q2.shape, 1) < SWA_HEAD_DIM
        zero = jnp.zeros_like(q2)
        qh = [jnp.where(half, q2, zero), jnp.where(half, zero, q2)]
        s_full = [_dot_nt(qh[hh], kg) for hh in range(2)]
        for r in range(tq // w):
            keys = slice(r * w, (r + 2) * w)
            acc, esink = [], []
            for hh in range(2):
                sink = sink_ref[layer, 2 * p + hh] * LOG2E
                s = jnp.where(band_first if r == 0 else band, s_full[hh][r * w:(r + 1) * w, keys], NEG_BIG)
                m = jnp.maximum(jnp.broadcast_to(jnp.max(s, axis=-1, keepdims=True), (w, LANES)), sink)
                prob = jnp.exp2(s - jnp.concatenate([m, m], axis=1))
                vh = vcat[keys, (2 * g + hh) * LANES:(2 * g + hh + 1) * LANES]
                acc.append(_dot(prob.astype(BF16), vh))
                esink.append(jnp.exp2(sink - m))
            num = jnp.where(first_half, acc[0], acc[1])
            den = (jnp.where(first_half, pltpu.roll(acc[0], SWA_HEAD_DIM, axis=1),
                             pltpu.roll(acc[1], SWA_HEAD_DIM, axis=1))
                   + jnp.where(first_half, esink[0], esink[1]))
            o_ref[r * w:(r + 1) * w, p * LANES:(p + 1) * LANES] = (num / den).astype(o_ref.dtype)


def _swa_attention(sinks, layer, qs, ks, vs, seq):
    t = qs.shape[0]
    tq = SWA_TQ
    per_tile = tq // SWA_WINDOW
    cur = lambda n: pl.BlockSpec((tq, n), lambda i: (i, 0))
    prev = lambda n: pl.BlockSpec((SWA_WINDOW, n), lambda i: (jnp.maximum(i * per_tile - 1, 0), 0))
    return pl.pallas_call(
        functools.partial(_swa_kernel, layer=layer, tiles_per_seq=seq // tq),
        grid=(t // tq,),
        in_specs=[pl.BlockSpec(memory_space=pltpu.SMEM), cur(512), cur(256), cur(512), prev(256), prev(512)],
        out_specs=cur(512),
        out_shape=jax.ShapeDtypeStruct((t, 512), BF16),
        compiler_params=_params(("parallel",)),
        name="swa_attention",
    )(sinks, qs, ks, vs, ks, vs)


def _post_kernel(x_ref, oa_ref, ob_ref, oc_ref, gate_ref, woa_ref, wob_ref, woc_ref, wout_ref,
                 gpost_ref, gpre_ref, wup_ref, wdown_ref, gmlp_ref, out_ref):
    d = D_MODEL
    tm = x_ref.shape[0]
    halves = [slice(0, tm // 2), slice(tm // 2, tm)]
    merged = []
    for r in halves:
        mixed = (gate_ref[r, 0:d] * _dot(oa_ref[r, :], woa_ref[...])
                 + gate_ref[r, d:2 * d] * _dot(ob_ref[r, :], wob_ref[...])
                 + gate_ref[r, 2 * d:3 * d] * _dot(oc_ref[r, :], woc_ref[...]))
        merged.append(_dot(mixed.astype(BF16), wout_ref[...]))
    x1 = [x_ref[r, :] + _rms(y, gpost_ref[...]) for r, y in zip(halves, merged)]
    hidden = [_rms(v, gpre_ref[...]).astype(BF16) for v in x1]
    up = [jnp.square(jnp.maximum(_dot(h, wup_ref[...]), 0.0)).astype(BF16) for h in hidden]
    down = [_dot(u, wdown_ref[...]) for u in up]
    for r, v, y in zip(halves, x1, down):
        out_ref[r, :] = v + _rms(y, gmlp_ref[...])


def _post(x, oa, ob, oc, gates, layer, woa, wob, woc, wout, g_post, g_pre, wup, wdown, g_mlp):
    t = x.shape[0]
    tm = POST_TM
    row = lambda n: pl.BlockSpec((tm, n), lambda i: (i, 0))
    consts = [woa, wob, woc, wout, g_post, g_pre, wup, wdown, g_mlp]
    return pl.pallas_call(
        _post_kernel,
        grid=(t // tm,),
        in_specs=[row(D_MODEL), row(512), row(512), row(512), row(N_BRANCHES * D_MODEL)]
                 + [_layer_spec(c, layer) for c in consts],
        out_specs=row(D_MODEL),
        out_shape=jax.ShapeDtypeStruct((t, D_MODEL), F32),
        compiler_params=_params(("parallel",)),
        name="post",
    )(x, oa, ob, oc, gates, *consts)


def _rot_half(w):
    half = w.shape[-1] // 2
    return jnp.concatenate([-w[..., half:], w[..., :half]], axis=-1)


def _pair_pad(a, b):
    z = jnp.zeros_like(a)
    return jnp.concatenate([a, z, z, b], axis=-1).reshape(*a.shape[:-2], -1)


def _layout_weights(w_in, w_uq, w_ukv):
    dep = w_in.shape[0]
    w1 = jnp.swapaxes(w_in, 1, 2).astype(BF16)
    assert w1.shape[1] == W1_COLS
    k_rope = w1[:, OFF_KROPE:OFF_KROPE + MLA_ROPE, :]
    half = MLA_ROPE // 2
    k_rope_rot = jnp.concatenate([-k_rope[:, half:], k_rope[:, :half]], axis=1)
    z32 = jnp.zeros((dep, 32, D_MODEL), BF16)
    wrope = jnp.concatenate([k_rope_rot, z32, k_rope, z32], axis=1)

    uq = w_uq.astype(BF16).reshape(dep, MLA_Q_LORA, MLA_HEADS, MLA_NOPE + MLA_ROPE)
    nope, pe = uq[..., :MLA_NOPE], uq[..., MLA_NOPE:]
    pe_rot = _rot_half(pe)
    zq = jnp.zeros((dep, MLA_Q_LORA, MLA_HEADS, 32), BF16)
    wq_main = jnp.concatenate([nope, pe, zq], axis=-1).reshape(dep, MLA_Q_LORA, -1)
    wq_rot = jnp.concatenate([jnp.zeros_like(nope), pe_rot, zq], axis=-1).reshape(dep, MLA_Q_LORA, -1)
    wq = jnp.concatenate([wq_main, wq_rot], axis=-1)

    ukv = w_ukv.astype(BF16).reshape(dep, MLA_KV_LORA, MLA_HEADS, MLA_NOPE + MLA_V)
    k_nope, v = ukv[..., :MLA_NOPE], ukv[..., MLA_NOPE:]
    wk = jnp.concatenate([k_nope, jnp.zeros_like(k_nope)], axis=-1).reshape(dep, MLA_KV_LORA, -1)
    vp = v.reshape(dep, MLA_KV_LORA, HEAD_PAIRS, 2, MLA_V)
    wkv = jnp.concatenate([wk, _pair_pad(vp[..., 0, :], vp[..., 1, :])], axis=-1)
    return w1, wrope, wq, wkv


def kernel(x, positions, g_mix_pre, w_in, b_gate, g_q_lat, g_kv_lat, w_uq, w_ukv, swa_sinks, w_o_mla, w_o_swa, w_o_sb, w_out, g_mix_post, g_mlp_pre, w_up, w_down, g_mlp_post):
    batch, seq, d = x.shape
    t = batch * seq
    tables = _rope_tables(positions)
    w1, wrope, wq, wkv = _layout_weights(w_in, w_uq, w_ukv)
    woa, wob, woc, wout, wup, wdown = (w.astype(BF16) for w in (w_o_mla, w_o_swa, w_o_sb, w_out, w_up, w_down))
    tri = (lax.broadcasted_iota(jnp.int32, (SB_BLK, SB_BLK), 0)
           > lax.broadcasted_iota(jnp.int32, (SB_BLK, SB_BLK), 1)).astype(BF16)
    rows = lambda g: g.reshape(DEPTH, 1, -1)
    g_pre, bg, g_q, g_kv = rows(g_mix_pre), rows(b_gate), rows(g_q_lat), rows(g_kv_lat)
    g_post, g_mlp_in, g_mlp_out = rows(g_mix_post), rows(g_mlp_pre), rows(g_mlp_post)

    xt = x.reshape(t, d)
    for l in range(DEPTH):
        qm, km, vm, qs, ks, vs, qb, kb, vb, gates = _prep(xt, tables, l, g_pre, w1, wrope, bg, g_q, g_kv,
                                                          wq, wkv)
        oa = _mla_attention(qm, km, vm, batch, seq)
        ob = _swa_attention(swa_sinks, l, qs, ks, vs, seq)
        oc = _sb_attention(qb, kb, vb, tri, batch, seq)
        xt = _post(xt, oa, ob, oc, gates, l, woa, wob, woc, wout, g_post, g_mlp_in, wup, wdown, g_mlp_out)
    return xt.reshape(batch, seq, d)
```

```python
import functools

import jax
import jax.numpy as jnp
from jax import lax
from jax.experimental import pallas as pl
from jax.experimental.pallas import tpu as pltpu

F32 = jnp.float32
BF16 = jnp.bfloat16

D_MODEL = 1024
DEPTH = 4
MLA_HEADS = 8
MLA_Q_LORA = 256
MLA_KV_LORA = 128
MLA_NOPE = 64
MLA_ROPE = 32
MLA_V = 64
SWA_HEADS = 8
SWA_KV_HEADS = 2
SWA_HEAD_DIM = 64
SWA_WINDOW = 128
SB_HEADS = 8
SB_HEAD_DIM = 64
D_FF = 4 * D_MODEL
ROPE_THETA = 10000.0
EPS = 1e-6
N_BRANCHES = 3

LANES = 128
HEAD_PAIRS = 4
MLA_HEAD_PAD = 128
NEG_BIG = -1e30
LOG2E = 1.4426950408889634
SB_SKIP = 136.0
SP_CLAMP = 126.0
VMEM_LIMIT = 56 * 1024 * 1024

OFF_CQ = 0
OFF_CKV = OFF_CQ + MLA_Q_LORA
OFF_KROPE = OFF_CKV + MLA_KV_LORA
OFF_QS = OFF_KROPE + MLA_ROPE
OFF_KS = OFF_QS + 512
OFF_VS = OFF_KS + 128
OFF_QB = OFF_VS + 128
OFF_KB = OFF_QB + 512
OFF_VB = OFF_KB + 512
OFF_GATE = OFF_VB + 512
W1_COLS = OFF_GATE + N_BRANCHES * D_MODEL

PREP_TM = 256
POST_TM = 256
MLA_TQ = 1024
MLA_TK = 512
MLA_PAIRS = 2
SB_TQ = 256
SB_TK = 256
SB_BLK = 256
SB_PAIRS = 4
SB_LOOKAHEAD = 16
SWA_TQ = 512
ROPE_TM = 2048


def _rms(x, g):
    return x * lax.rsqrt(jnp.mean(x * x, axis=-1, keepdims=True) + EPS) * g


def _dot(a, b):
    return jnp.dot(a, b, preferred_element_type=F32)


def _dot_nt(a, b):
    return lax.dot_general(a, b, (((1,), (1,)), ((), ())), preferred_element_type=F32)


def _const_spec(shape):
    return pl.BlockSpec(shape, lambda *_: (0,) * len(shape), pipeline_mode=pl.Buffered(1))


def _layer_spec(arr, layer):
    return pl.BlockSpec((None,) + arr.shape[1:], lambda *_: (layer, 0, 0), pipeline_mode=pl.Buffered(1))


def _params(sem):
    return pltpu.CompilerParams(dimension_semantics=sem, vmem_limit_bytes=VMEM_LIMIT)


def _pair_ones(width):
    lane = lax.broadcasted_iota(jnp.int32, (1, width), 1) % (2 * LANES)
    return ((lane >= 64) & (lane < 2 * LANES - 64)).astype(F32)


def _rope_table_kernel(pos_ref, inv_ref, ca_ref, sa_ref, cb_ref, sb_ref):
    ang = pos_ref[...].astype(F32) * inv_ref[...]
    cos, sin = jnp.cos(ang), jnp.sin(ang)
    lane = lax.broadcasted_iota(jnp.int32, (1, LANES), 1)
    low = lane < SWA_HEAD_DIM
    rope_lanes = (lane >= MLA_NOPE) & (lane < MLA_NOPE + MLA_ROPE)
    ca_ref[...] = jnp.where(rope_lanes, cos, jnp.where(low, 1.0, 0.0))
    sa_ref[...] = jnp.where(rope_lanes, sin, 0.0)
    cb_ref[...] = jnp.where(low, cos, pltpu.roll(cos, SWA_HEAD_DIM, axis=1))
    sb_ref[...] = jnp.where(low, sin, pltpu.roll(sin, SWA_HEAD_DIM, axis=1))


def _rope_tables(positions):
    t = positions.size
    pos = positions.reshape(t, 1)
    inv_a16 = 1.0 / (ROPE_THETA ** (jnp.arange(0, MLA_ROPE, 2, dtype=F32) / MLA_ROPE))
    inv_b32 = 1.0 / (ROPE_THETA ** (jnp.arange(0, SWA_HEAD_DIM, 2, dtype=F32) / SWA_HEAD_DIM))
    inv = jnp.concatenate([inv_b32, inv_b32, inv_a16, inv_a16, jnp.zeros((32,), F32)]).reshape(1, LANES)
    row = pl.BlockSpec((ROPE_TM, LANES), lambda i: (i, 0))
    out = jax.ShapeDtypeStruct((t, LANES), F32)
    return pl.pallas_call(
        _rope_table_kernel,
        grid=(t // ROPE_TM,),
        in_specs=[pl.BlockSpec((ROPE_TM, 1), lambda i: (i, 0)), pl.BlockSpec((1, LANES), lambda i: (0, 0))],
        out_specs=[row, row, row, row],
        out_shape=[out, out, out, out],
        compiler_params=_params(("parallel",)),
        name="rope_tables",
    )(pos, inv)


def _prep_kernel(x_ref, g_ref, ca_ref, sa_ref, cb_ref, sb_ref, w1_ref, wrope_ref, bg_ref, gq_ref, gkv_ref,
                 wq_ref, wkv_ref,
                 qm_ref, km_ref, vm_ref, qs_ref, ks_ref, vs_ref, qb_ref, kb_ref, vb_ref, gate_ref):
    h = _rms(x_ref[...], g_ref[...]).astype(BF16)

    def mm(lo, n):
        return _dot_nt(h, w1_ref[lo:lo + n, :])

    ca, sa, cb, sb = ca_ref[...], sa_ref[...], cb_ref[...], sb_ref[...]

    lane = lax.broadcasted_iota(jnp.int32, (1, LANES), 1)
    low_head = lane < SWA_HEAD_DIM
    first_half = lane % SWA_HEAD_DIM < SWA_HEAD_DIM // 2

    def roll(v, shift):
        return pltpu.roll(v, shift, axis=1)

    def rope64(v, cos, sin):
        half = SWA_HEAD_DIM // 2
        rot = jnp.where(first_half, -roll(v, LANES - half), roll(v, half))
        return v * cos + rot * sin

    cqn = _rms(mm(OFF_CQ, MLA_Q_LORA), gq_ref[...]).astype(BF16)
    ckvn = _rms(mm(OFF_CKV, MLA_KV_LORA), gkv_ref[...]).astype(BF16)
    kpe_blk = _dot_nt(h, wrope_ref[...])

    q_scale = SWA_HEAD_DIM ** -0.5 * LOG2E
    qs = mm(OFF_QS, 512)
    for p in range(HEAD_PAIRS):
        blk = slice(p * LANES, (p + 1) * LANES)
        qs_ref[:, blk] = rope64(qs[:, blk], cb * q_scale, sb * q_scale).astype(BF16)
    ks = rope64(mm(OFF_KS, LANES), cb, sb)
    ks_swap = roll(ks, SWA_HEAD_DIM)
    ks_ref[:, :LANES] = jnp.where(low_head, ks, ks_swap).astype(BF16)
    ks_ref[:, LANES:] = jnp.where(low_head, ks_swap, ks).astype(BF16)
    vs = mm(OFF_VS, LANES)
    vs_swap = roll(vs, SWA_HEAD_DIM)
    for n, blk_val in enumerate([jnp.where(low_head, vs, 1.0), jnp.where(low_head, 1.0, vs_swap),
                                 jnp.where(low_head, vs_swap, 1.0), jnp.where(low_head, 1.0, vs)]):
        vs_ref[:, n * LANES:(n + 1) * LANES] = blk_val.astype(BF16)

    qb_ref[...] = (mm(OFF_QB, 512) * (SB_HEAD_DIM ** -0.5 * LOG2E)).astype(BF16)
    kb_ref[...] = mm(OFF_KB, 512).astype(BF16)
    vb_ref[...] = mm(OFF_VB, 512).astype(BF16)

    scale = (MLA_NOPE + MLA_ROPE) ** -0.5 * LOG2E
    ca8 = jnp.concatenate([ca * scale] * MLA_HEADS, axis=1)
    sa8 = jnp.concatenate([sa * scale] * MLA_HEADS, axis=1)
    nq = MLA_HEADS * MLA_HEAD_PAD
    qm_ref[...] = (_dot(cqn, wq_ref[:, :nq]) * ca8 + _dot(cqn, wq_ref[:, nq:]) * sa8).astype(BF16)
    rope_lanes = (lane >= MLA_NOPE) & (lane < MLA_NOPE + MLA_ROPE)
    kpe = jnp.where(rope_lanes, kpe_blk * ca + roll(kpe_blk, MLA_NOPE) * sa, 0.0)
    kpe8 = jnp.concatenate([kpe] * MLA_HEADS, axis=1)
    km_ref[...] = (_dot(ckvn, wkv_ref[:, :nq]) + kpe8).astype(BF16)
    vm_ref[...] = (_dot(ckvn, wkv_ref[:, nq:]) + _pair_ones(nq)).astype(BF16)

    gate_ref[...] = jax.nn.sigmoid(mm(OFF_GATE, N_BRANCHES * D_MODEL) + bg_ref[...])


def _prep(x, tables, layer, g_pre, w1, wrope, b_gate, g_q, g_kv, wq, wkv):
    t = x.shape[0]
    tm = PREP_TM
    ca, sa, cb, sb = tables

    def row(n):
        return pl.BlockSpec((tm, n), lambda i: (i, 0))

    def out(n, dt=BF16):
        return jax.ShapeDtypeStruct((t, n), dt)

    nq = MLA_HEADS * MLA_HEAD_PAD
    consts = [w1, wrope, b_gate, g_q, g_kv, wq, wkv]
    return pl.pallas_call(
        _prep_kernel,
        grid=(t // tm,),
        in_specs=[row(D_MODEL), _layer_spec(g_pre, layer), row(LANES), row(LANES), row(LANES), row(LANES)]
                 + [_layer_spec(c, layer) for c in consts],
        out_specs=[row(nq), row(nq), row(nq), row(512), row(256), row(512), row(512), row(512), row(512),
                   row(N_BRANCHES * D_MODEL)],
        out_shape=[out(nq), out(nq), out(nq), out(512), out(256), out(512), out(512), out(512), out(512),
                   out(N_BRANCHES * D_MODEL, F32)],
        compiler_params=_params(("parallel",)),
        name="prep",
    )(x, g_pre, ca, sa, cb, sb, *consts)


def _mla_kernel(q_ref, k_ref, v_ref, o_ref, m_ref, acc_ref):
    qi = pl.program_id(2)
    tq, tk = MLA_TQ, MLA_TK
    n_heads = 2 * MLA_PAIRS
    m_ref[...] = jnp.full(m_ref.shape, NEG_BIG, F32)
    acc_ref[...] = jnp.zeros(acc_ref.shape, F32)

    def head_lanes(h):
        return slice(h * MLA_HEAD_PAD, (h + 1) * MLA_HEAD_PAD)

    def process(row_sets):
        units = [(rows, chunks, h) for rows, chunks in row_sets for h in range(n_heads)]

        def scores(unit):
            rows, chunks, h = unit
            return [_dot_nt(q_ref[rows, head_lanes(h)],
                            k_ref[pl.ds(pl.multiple_of(j * tk, tk), tk), head_lanes(h)]) for j, _ in chunks]

        ahead = scores(units[0])
        for n, (rows, chunks, h) in enumerate(units):
            s_unit, ahead = ahead, (scores(units[n + 1]) if n + 1 < len(units) else None)
            n_rows = rows.stop - rows.start
            parts = []
            for (_, masked), s in zip(chunks, s_unit):
                if masked:
                    row = lax.broadcasted_iota(jnp.int32, (n_rows, tk), 0)
                    col = lax.broadcasted_iota(jnp.int32, (n_rows, tk), 1)
                    s = jnp.where(col <= row, s, NEG_BIG)
                parts.append(s)
            s = parts[0] if len(parts) == 1 else jnp.concatenate(parts, axis=1)
            width = len(chunks) * tk
            start = pl.multiple_of(chunks[0][0] * tk, tk)
            v = v_ref[pl.ds(start, width), head_lanes(h)]
            m_old = m_ref[h, rows]
            m_new = jnp.maximum(m_old, jnp.max(s, axis=-1, keepdims=True))
            alpha = jnp.exp2(m_old - m_new)
            p = jnp.exp2(s - jnp.concatenate([m_new] * (width // LANES), axis=1))
            acc_ref[h, rows] = alpha * acc_ref[h, rows] + _dot(p.astype(BF16), v)
            m_ref[h, rows] = m_new

    assert tq == 2 * tk
    all_rows, upper, lower = slice(0, tq), slice(0, tk), slice(tk, tq)

    def body(i, carry):
        process([(all_rows, [(2 * i, False), (2 * i + 1, False)])])
        return carry

    lax.fori_loop(0, qi, body, 0)
    process([(upper, [(2 * qi, True)]), (lower, [(2 * qi, False), (2 * qi + 1, True)])])

    first_half = lax.broadcasted_iota(jnp.int32, (tq, LANES), 1) < MLA_V
    for p in range(MLA_PAIRS):
        a0, a1 = acc_ref[2 * p], acc_ref[2 * p + 1]
        num = jnp.where(first_half, a0, a1)
        den = jnp.where(first_half, pltpu.roll(a0, MLA_V, axis=1), pltpu.roll(a1, MLA_V, axis=1))
        o_ref[:, p * LANES:(p + 1) * LANES] = (num / den).astype(o_ref.dtype)


def _mla_attention(qm, km, vm, batch, seq):
    t = qm.shape[0]
    tq = MLA_TQ
    nq = seq // tq
    wide = MLA_PAIRS * 2 * MLA_HEAD_PAD
    return pl.pallas_call(
        _mla_kernel,
        grid=(batch, HEAD_PAIRS // MLA_PAIRS, nq),
        in_specs=[pl.BlockSpec((tq, wide), lambda b, p, i: (b * nq + i, p)),
                  pl.BlockSpec((seq, wide), lambda b, p, i: (b, p)),
                  pl.BlockSpec((seq, wide), lambda b, p, i: (b, p))],
        out_specs=pl.BlockSpec((tq, MLA_PAIRS * LANES), lambda b, p, i: (b * nq + i, p)),
        out_shape=jax.ShapeDtypeStruct((t, HEAD_PAIRS * LANES), BF16),
        scratch_shapes=[pltpu.VMEM((2 * MLA_PAIRS, tq, LANES), F32),
                        pltpu.VMEM((2 * MLA_PAIRS, tq, LANES), F32)],
        compiler_params=_params(("parallel", "parallel", "parallel")),
        name="mla_attention",
    )(qm, km, vm)


def _sb_kernel(q_ref, k_ref, v_ref, tri_ref, o_ref, qh_ref, carry_ref, acc_ref):
    qi = pl.program_id(2)
    tq, tk, blk = SB_TQ, SB_TK, SB_BLK
    carry_ref[...] = jnp.zeros(carry_ref.shape, F32)
    acc_ref[...] = jnp.zeros(acc_ref.shape, F32)
    n_heads = 2 * SB_PAIRS
    first_half = lax.broadcasted_iota(jnp.int32, (tq, LANES), 1) < SB_HEAD_DIM
    for p in range(SB_PAIRS):
        q2 = q_ref[:, p * LANES:(p + 1) * LANES]
        zero = jnp.zeros_like(q2)
        qh_ref[2 * p] = jnp.where(first_half, q2, zero)
        qh_ref[2 * p + 1] = jnp.where(first_half, zero, q2)

    def pair_lanes(h):
        return slice((h // 2) * LANES, (h // 2 + 1) * LANES)

    def weights_pv(h, z, v, masked):
        sp = jnp.maximum(z, jnp.log2(1.0 + jnp.exp2(jnp.minimum(z, SP_CLAMP))))
        own = z - sp
        if masked:
            row = lax.broadcasted_iota(jnp.int32, (tq, tk), 0)
            col = lax.broadcasted_iota(jnp.int32, (tq, tk), 1)
            valid = col < row
            sp = jnp.where(valid, sp, 0.0)
        sp16 = sp.astype(BF16)
        c = carry_ref[h]
        expo = [None] * (tk // blk)
        for b in reversed(range(tk // blk)):
            cols = slice(b * blk, (b + 1) * blk)
            later = _dot(sp16[:, cols], tri_ref[...])
            expo[b] = own[:, cols] - later - jnp.concatenate([c] * (blk // LANES), axis=1)
            c = c + jnp.sum(sp[:, cols], axis=-1, keepdims=True)
        carry_ref[h] = c
        a = jnp.exp2(jnp.concatenate(expo, axis=1))
        if masked:
            a = jnp.where(valid, a, 0.0)
        acc_ref[h] += _dot(a.astype(BF16), v)

    def process(chunks):
        starts = [pl.multiple_of(j * tk, tk) for j, _ in chunks]
        keys = [k_ref[pl.ds(st, tk), :] for st in starts]
        vals = [v_ref[pl.ds(st, tk), :] for st in starts]
        units = [(c, h) for c in range(len(chunks)) for h in range(n_heads)]

        def logits(unit):
            c, h = unit
            return _dot_nt(qh_ref[h], keys[c][:, pair_lanes(h)])

        pending = [logits(u) for u in units[:SB_LOOKAHEAD]]
        for i, (c, h) in enumerate(units):
            z = pending.pop(0)
            if i + SB_LOOKAHEAD < len(units):
                pending.append(logits(units[i + SB_LOOKAHEAD]))
            weights_pv(h, z, vals[c][:, pair_lanes(h)], chunks[c][1])

    def min_carry():
        c = carry_ref[0]
        for h in range(1, n_heads):
            c = jnp.minimum(c, carry_ref[h])
        return jnp.min(c)

    def more(state):
        j, cmin = state
        return jnp.logical_and(j >= 0, cmin < SB_SKIP)

    def step(state):
        j, _ = state
        process([(j, False)])
        return j - 1, min_carry()

    @pl.when(qi == 0)
    def _():
        process([(qi, True)])

    @pl.when(qi > 0)
    def _():
        process([(qi, True), (qi - 1, False)])

    lax.while_loop(more, step, (qi - 2, min_carry()))

    for p in range(SB_PAIRS):
        o_ref[:, p * LANES:(p + 1) * LANES] = jnp.where(
            first_half, acc_ref[2 * p], acc_ref[2 * p + 1]).astype(o_ref.dtype)


def _sb_attention(qb, kb, vb, tri, batch, seq):
    t = qb.shape[0]
    tq = SB_TQ
    nq = seq // tq
    wide = SB_PAIRS * LANES
    return pl.pallas_call(
        _sb_kernel,
        grid=(batch, HEAD_PAIRS // SB_PAIRS, nq),
        in_specs=[pl.BlockSpec((tq, wide), lambda b, p, i: (b * nq + i, p)),
                  pl.BlockSpec((seq, wide), lambda b, p, i: (b, p)),
                  pl.BlockSpec((seq, wide), lambda b, p, i: (b, p)),
                  _const_spec(tri.shape)],
        out_specs=pl.BlockSpec((tq, wide), lambda b, p, i: (b * nq + i, p)),
        out_shape=jax.ShapeDtypeStruct((t, HEAD_PAIRS * LANES), BF16),
        scratch_shapes=[pltpu.VMEM((2 * SB_PAIRS, tq, LANES), BF16), pltpu.VMEM((2 * SB_PAIRS, tq, LANES), F32),
                        pltpu.VMEM((2 * SB_PAIRS, tq, LANES), F32)],
        compiler_params=_params(("parallel", "parallel", "parallel")),
        name="sb_attention",
    )(qb, kb, vb, tri)


def _swa_kernel(sink_ref, q_ref, k_ref, v_ref, kp_ref, vp_ref, o_ref, *, layer, tiles_per_seq):
    i = pl.program_id(0)
    w, tq = SWA_WINDOW, SWA_TQ
    has_prev = (i % tiles_per_seq) != 0
    kcat = jnp.concatenate([kp_ref[...], k_ref[...]], axis=0)
    vcat = jnp.concatenate([vp_ref[...], v_ref[...]], axis=0)
    row = lax.broadcasted_iota(jnp.int32, (w, 2 * w), 0)
    col = lax.broadcasted_iota(jnp.int32, (w, 2 * w), 1)
    band = (col > row) & (col <= row + w)
    band_first = band & ((col >= w) | has_prev)
    first_half = lax.broadcasted_iota(jnp.int32, (w, LANES), 1) < SWA_HEAD_DIM
    def kv_head(p):
        return p // (HEAD_PAIRS // SWA_KV_HEADS)

    def scores(p):
        q2 = q_ref[:, p * LANES:(p + 1) * LANES]
        kg = kcat[:, kv_head(p) * LANES:(kv_head(p) + 1) * LANES]
        half = lax.broadcasted_iota(jnp.int32, q2.shape, 1) < SWA_HEAD_DIM
        zero = jnp.zeros_like(q2)
        return [_dot_nt(jnp.where(half, q2, zero), kg), _dot_nt(jnp.where(half, zero, q2), kg)]

    ahead = scores(0)
    for p in range(HEAD_PAIRS):
        g = kv_head(p)
        s_full, ahead = ahead, (scores(p + 1) if p + 1 < HEAD_PAIRS else None)
        for r in range(tq // w):
            keys = slice(r * w, (r + 2) * w)
            acc, esink = [], []
            for hh in range(2):
                sink = sink_ref[layer, 2 * p + hh] * LOG2E
                s = jnp.where(band_first if r == 0 else band, s_full[hh][r * w:(r + 1) * w, keys], NEG_BIG)
                m = jnp.maximum(jnp.broadcast_to(jnp.max(s, axis=-1, keepdims=True), (w, LANES)), sink)
                prob = jnp.exp2(s - jnp.concatenate([m, m], axis=1))
                vh = vcat[keys, (2 * g + hh) * LANES:(2 * g + hh + 1) * LANES]
                acc.append(_dot(prob.astype(BF16), vh))
                esink.append(jnp.exp2(sink - m))
            num = jnp.where(first_half, acc[0], acc[1])
            den = (jnp.where(first_half, pltpu.roll(acc[0], SWA_HEAD_DIM, axis=1),
                             pltpu.roll(acc[1], SWA_HEAD_DIM, axis=1))
                   + jnp.where(first_half, esink[0], esink[1]))
            o_ref[r * w:(r + 1) * w, p * LANES:(p + 1) * LANES] = (num / den).astype(o_ref.dtype)


def _swa_attention(sinks, layer, qs, ks, vs, seq):
    t = qs.shape[0]
    tq = SWA_TQ
    per_tile = tq // SWA_WINDOW
    cur = lambda n: pl.BlockSpec((tq, n), lambda i: (i, 0))
    prev = lambda n: pl.BlockSpec((SWA_WINDOW, n), lambda i: (jnp.maximum(i * per_tile - 1, 0), 0))
    return pl.pallas_call(
        functools.partial(_swa_kernel, layer=layer, tiles_per_seq=seq // tq),
        grid=(t // tq,),
        in_specs=[pl.BlockSpec(memory_space=pltpu.SMEM), cur(512), cur(256), cur(512), prev(256), prev(512)],
        out_specs=cur(512),
        out_shape=jax.ShapeDtypeStruct((t, 512), BF16),
        compiler_params=_params(("parallel",)),
        name="swa_attention",
    )(sinks, qs, ks, vs, ks, vs)


def _post_kernel(x_ref, oa_ref, ob_ref, oc_ref, gate_ref, woa_ref, wob_ref, woc_ref, wout_ref,
                 gpost_ref, gpre_ref, wup_ref, wdown_ref, gmlp_ref, out_ref):
    d = D_MODEL
    tm = x_ref.shape[0]
    halves = [slice(0, tm // 2), slice(tm // 2, tm)]
    merged = []
    for r in halves:
        mixed = (gate_ref[r, 0:d] * _dot(oa_ref[r, :], woa_ref[...])
                 + gate_ref[r, d:2 * d] * _dot(ob_ref[r, :], wob_ref[...])
                 + gate_ref[r, 2 * d:3 * d] * _dot(oc_ref[r, :], woc_ref[...]))
        merged.append(_dot(mixed.astype(BF16), wout_ref[...]))
    x1 = [x_ref[r, :] + _rms(y, gpost_ref[...]) for r, y in zip(halves, merged)]
    hidden = [_rms(v, gpre_ref[...]).astype(BF16) for v in x1]
    up = [jnp.square(jnp.maximum(_dot(h, wup_ref[...]), 0.0)).astype(BF16) for h in hidden]
    down = [_dot(u, wdown_ref[...]) for u in up]
    for r, v, y in zip(halves, x1, down):
        out_ref[r, :] = v + _rms(y, gmlp_ref[...])


def _post(x, oa, ob, oc, gates, layer, woa, wob, woc, wout, g_post, g_pre, wup, wdown, g_mlp):
    t = x.shape[0]
    tm = POST_TM
    row = lambda n: pl.BlockSpec((tm, n), lambda i: (i, 0))
    consts = [woa, wob, woc, wout, g_post, g_pre, wup, wdown, g_mlp]
    return pl.pallas_call(
        _post_kernel,
        grid=(t // tm,),
        in_specs=[row(D_MODEL), row(512), row(512), row(512), row(N_BRANCHES * D_MODEL)]
                 + [_layer_spec(c, layer) for c in consts],
        out_specs=row(D_MODEL),
        out_shape=jax.ShapeDtypeStruct((t, D_MODEL), F32),
        compiler_params=_params(("parallel",)),
        name="post",
    )(x, oa, ob, oc, gates, *consts)


def _rot_half(w):
    half = w.shape[-1] // 2
    return jnp.concatenate([-w[..., half:], w[..., :half]], axis=-1)


def _pair_pad(a, b):
    z = jnp.zeros_like(a)
    return jnp.concatenate([a, z, z, b], axis=-1).reshape(*a.shape[:-2], -1)


def _layout_weights(w_in, w_uq, w_ukv):
    dep = w_in.shape[0]
    w1 = jnp.swapaxes(w_in, 1, 2).astype(BF16)
    assert w1.shape[1] == W1_COLS
    k_rope = w1[:, OFF_KROPE:OFF_KROPE + MLA_ROPE, :]
    half = MLA_ROPE // 2
    k_rope_rot = jnp.concatenate([-k_rope[:, half:], k_rope[:, :half]], axis=1)
    z32 = jnp.zeros((dep, 32, D_MODEL), BF16)
    wrope = jnp.concatenate([k_rope_rot, z32, k_rope, z32], axis=1)

    uq = w_uq.astype(BF16).reshape(dep, MLA_Q_LORA, MLA_HEADS, MLA_NOPE + MLA_ROPE)
    nope, pe = uq[..., :MLA_NOPE], uq[..., MLA_NOPE:]
    pe_rot = _rot_half(pe)
    zq = jnp.zeros((dep, MLA_Q_LORA, MLA_HEADS, 32), BF16)
    wq_main = jnp.concatenate([nope, pe, zq], axis=-1).reshape(dep, MLA_Q_LORA, -1)
    wq_rot = jnp.concatenate([jnp.zeros_like(nope), pe_rot, zq], axis=-1).reshape(dep, MLA_Q_LORA, -1)
    wq = jnp.concatenate([wq_main, wq_rot], axis=-1)

    ukv = w_ukv.astype(BF16).reshape(dep, MLA_KV_LORA, MLA_HEADS, MLA_NOPE + MLA_V)
    k_nope, v = ukv[..., :MLA_NOPE], ukv[..., MLA_NOPE:]
    wk = jnp.concatenate([k_nope, jnp.zeros_like(k_nope)], axis=-1).reshape(dep, MLA_KV_LORA, -1)
    vp = v.reshape(dep, MLA_KV_LORA, HEAD_PAIRS, 2, MLA_V)
    wkv = jnp.concatenate([wk, _pair_pad(vp[..., 0, :], vp[..., 1, :])], axis=-1)
    return w1, wrope, wq, wkv


def kernel(x, positions, g_mix_pre, w_in, b_gate, g_q_lat, g_kv_lat, w_uq, w_ukv, swa_sinks, w_o_mla, w_o_swa, w_o_sb, w_out, g_mix_post, g_mlp_pre, w_up, w_down, g_mlp_post):
    batch, seq, d = x.shape
    t = batch * seq
    tables = _rope_tables(positions)
    w1, wrope, wq, wkv = _layout_weights(w_in, w_uq, w_ukv)
    woa, wob, woc, wout, wup, wdown = (w.astype(BF16) for w in (w_o_mla, w_o_swa, w_o_sb, w_out, w_up, w_down))
    tri = (lax.broadcasted_iota(jnp.int32, (SB_BLK, SB_BLK), 0)
           > lax.broadcasted_iota(jnp.int32, (SB_BLK, SB_BLK), 1)).astype(BF16)
    rows = lambda g: g.reshape(DEPTH, 1, -1)
    g_pre, bg, g_q, g_kv = rows(g_mix_pre), rows(b_gate), rows(g_q_lat), rows(g_kv_lat)
    g_post, g_mlp_in, g_mlp_out = rows(g_mix_post), rows(g_mlp_pre), rows(g_mlp_post)

    xt = x.reshape(t, d)
    for l in range(DEPTH):
        qm, km, vm, qs, ks, vs, qb, kb, vb, gates = _prep(xt, tables, l, g_pre, w1, wrope, bg, g_q, g_kv,
                                                          wq, wkv)
        oa = _mla_attention(qm, km, vm, batch, seq)
        ob = _swa_attention(swa_sinks, l, qs, ks, vs, seq)
        oc = _sb_attention(qb, kb, vb, tri, batch, seq)
        xt = _post(xt, oa, ob, oc, gates, l, woa, wob, woc, wout, g_post, g_mlp_in, wup, wdown, g_mlp_out)
    return xt.reshape(batch, seq, d)
```

```python
import functools

import jax
import jax.numpy as jnp
from jax import lax
from jax.experimental import pallas as pl
from jax.experimental.pallas import tpu as pltpu

F32 = jnp.float32
BF16 = jnp.bfloat16

D_MODEL = 1024
DEPTH = 4
MLA_HEADS = 8
MLA_Q_LORA = 256
MLA_KV_LORA = 128
MLA_NOPE = 64
MLA_ROPE = 32
MLA_V = 64
SWA_HEADS = 8
SWA_KV_HEADS = 2
SWA_HEAD_DIM = 64
SWA_WINDOW = 128
SB_HEADS = 8
SB_HEAD_DIM = 64
D_FF = 4 * D_MODEL
ROPE_THETA = 10000.0
EPS = 1e-6
N_BRANCHES = 3

LANES = 128
HEAD_PAIRS = 4
MLA_HEAD_PAD = 128
NEG_BIG = -1e30
LOG2E = 1.4426950408889634
SB_SKIP = 136.0
SP_CLAMP = 126.0
VMEM_LIMIT = 56 * 1024 * 1024

OFF_CQ = 0
OFF_CKV = OFF_CQ + MLA_Q_LORA
OFF_KROPE = OFF_CKV + MLA_KV_LORA
OFF_QS = OFF_KROPE + MLA_ROPE
OFF_KS = OFF_QS + 512
OFF_VS = OFF_KS + 128
OFF_QB = OFF_VS + 128
OFF_KB = OFF_QB + 512
OFF_VB = OFF_KB + 512
OFF_GATE = OFF_VB + 512
W1_COLS = OFF_GATE + N_BRANCHES * D_MODEL

PREP_TM = 256
POST_TM = 256
MLA_TQ = 1024
MLA_TK = 512
MLA_PAIRS = 2
MLA_DIAG = 256
SB_TQ = 256
SB_TK = 256
SB_BLK = 256
SB_PAIRS = 4
SWA_TQ = 512
ROPE_TM = 2048


def _rms(x, g):
    return x * lax.rsqrt(jnp.mean(x * x, axis=-1, keepdims=True) + EPS) * g


def _dot(a, b):
    return jnp.dot(a, b, preferred_element_type=F32)


def _dot_nt(a, b):
    return lax.dot_general(a, b, (((1,), (1,)), ((), ())), preferred_element_type=F32)


def _const_spec(shape):
    return pl.BlockSpec(shape, lambda *_: (0,) * len(shape), pipeline_mode=pl.Buffered(1))


def _layer_spec(arr, layer):
    return pl.BlockSpec((None,) + arr.shape[1:], lambda *_: (layer, 0, 0), pipeline_mode=pl.Buffered(1))


def _params(sem):
    return pltpu.CompilerParams(dimension_semantics=sem, vmem_limit_bytes=VMEM_LIMIT)


def _pair_ones(width):
    lane = lax.broadcasted_iota(jnp.int32, (1, width), 1) % (2 * LANES)
    return ((lane >= 64) & (lane < 2 * LANES - 64)).astype(F32)


def _rope_table_kernel(pos_ref, inv_ref, ca_ref, sa_ref, cb_ref, sb_ref):
    ang = pos_ref[...].astype(F32) * inv_ref[...]
    cos, sin = jnp.cos(ang), jnp.sin(ang)
    lane = lax.broadcasted_iota(jnp.int32, (1, LANES), 1)
    low = lane < SWA_HEAD_DIM
    rope_lanes = (lane >= MLA_NOPE) & (lane < MLA_NOPE + MLA_ROPE)
    ca_ref[...] = jnp.where(rope_lanes, cos, jnp.where(low, 1.0, 0.0))
    sa_ref[...] = jnp.where(rope_lanes, sin, 0.0)
    cb_ref[...] = jnp.where(low, cos, pltpu.roll(cos, SWA_HEAD_DIM, axis=1))
    sb_ref[...] = jnp.where(low, sin, pltpu.roll(sin, SWA_HEAD_DIM, axis=1))


def _rope_tables(positions):
    t = positions.size
    pos = positions.reshape(t, 1)
    inv_a16 = 1.0 / (ROPE_THETA ** (jnp.arange(0, MLA_ROPE, 2, dtype=F32) / MLA_ROPE))
    inv_b32 = 1.0 / (ROPE_THETA ** (jnp.arange(0, SWA_HEAD_DIM, 2, dtype=F32) / SWA_HEAD_DIM))
    inv = jnp.concatenate([inv_b32, inv_b32, inv_a16, inv_a16, jnp.zeros((32,), F32)]).reshape(1, LANES)
    row = pl.BlockSpec((ROPE_TM, LANES), lambda i: (i, 0))
    out = jax.ShapeDtypeStruct((t, LANES), F32)
    return pl.pallas_call(
        _rope_table_kernel,
        grid=(t // ROPE_TM,),
        in_specs=[pl.BlockSpec((ROPE_TM, 1), lambda i: (i, 0)), pl.BlockSpec((1, LANES), lambda i: (0, 0))],
        out_specs=[row, row, row, row],
        out_shape=[out, out, out, out],
        compiler_params=_params(("parallel",)),
        name="rope_tables",
    )(pos, inv)


def _prep_kernel(x_ref, g_ref, ca_ref, sa_ref, cb_ref, sb_ref, w1_ref, wrope_ref, bg_ref, gq_ref, gkv_ref,
                 wq_ref, wkv_ref,
                 qm_ref, km_ref, vm_ref, qs_ref, ks_ref, vs_ref, qb_ref, kb_ref, vb_ref, gate_ref):
    h = _rms(x_ref[...], g_ref[...]).astype(BF16)

    def mm(lo, n):
        return _dot_nt(h, w1_ref[lo:lo + n, :])

    ca, sa, cb, sb = ca_ref[...], sa_ref[...], cb_ref[...], sb_ref[...]

    lane = lax.broadcasted_iota(jnp.int32, (1, LANES), 1)
    low_head = lane < SWA_HEAD_DIM
    first_half = lane % SWA_HEAD_DIM < SWA_HEAD_DIM // 2

    def roll(v, shift):
        return pltpu.roll(v, shift, axis=1)

    def rope64(v, cos, sin):
        half = SWA_HEAD_DIM // 2
        rot = jnp.where(first_half, -roll(v, LANES - half), roll(v, half))
        return v * cos + rot * sin

    cqn = _rms(mm(OFF_CQ, MLA_Q_LORA), gq_ref[...]).astype(BF16)
    ckvn = _rms(mm(OFF_CKV, MLA_KV_LORA), gkv_ref[...]).astype(BF16)
    kpe_blk = _dot_nt(h, wrope_ref[...])

    q_scale = SWA_HEAD_DIM ** -0.5 * LOG2E
    qs = mm(OFF_QS, 512)
    for p in range(HEAD_PAIRS):
        blk = slice(p * LANES, (p + 1) * LANES)
        qs_ref[:, blk] = rope64(qs[:, blk], cb * q_scale, sb * q_scale).astype(BF16)
    ks = rope64(mm(OFF_KS, LANES), cb, sb)
    ks_swap = roll(ks, SWA_HEAD_DIM)
    ks_ref[:, :LANES] = jnp.where(low_head, ks, ks_swap).astype(BF16)
    ks_ref[:, LANES:] = jnp.where(low_head, ks_swap, ks).astype(BF16)
    vs = mm(OFF_VS, LANES)
    vs_swap = roll(vs, SWA_HEAD_DIM)
    for n, blk_val in enumerate([jnp.where(low_head, vs, 1.0), jnp.where(low_head, 1.0, vs_swap),
                                 jnp.where(low_head, vs_swap, 1.0), jnp.where(low_head, 1.0, vs)]):
        vs_ref[:, n * LANES:(n + 1) * LANES] = blk_val.astype(BF16)

    qb_ref[...] = (mm(OFF_QB, 512) * (SB_HEAD_DIM ** -0.5 * LOG2E)).astype(BF16)
    kb_ref[...] = mm(OFF_KB, 512).astype(BF16)
    vb_ref[...] = mm(OFF_VB, 512).astype(BF16)

    scale = (MLA_NOPE + MLA_ROPE) ** -0.5 * LOG2E
    ca8 = jnp.concatenate([ca * scale] * MLA_HEADS, axis=1)
    sa8 = jnp.concatenate([sa * scale] * MLA_HEADS, axis=1)
    nq = MLA_HEADS * MLA_HEAD_PAD
    qm_ref[...] = (_dot(cqn, wq_ref[:, :nq]) * ca8 + _dot(cqn, wq_ref[:, nq:]) * sa8).astype(BF16)
    rope_lanes = (lane >= MLA_NOPE) & (lane < MLA_NOPE + MLA_ROPE)
    kpe = jnp.where(rope_lanes, kpe_blk * ca + roll(kpe_blk, MLA_NOPE) * sa, 0.0)
    kpe8 = jnp.concatenate([kpe] * MLA_HEADS, axis=1)
    km_ref[...] = (_dot(ckvn, wkv_ref[:, :nq]) + kpe8).astype(BF16)
    vm_ref[...] = (_dot(ckvn, wkv_ref[:, nq:]) + _pair_ones(nq)).astype(BF16)

    gate_ref[...] = jax.nn.sigmoid(mm(OFF_GATE, N_BRANCHES * D_MODEL) + bg_ref[...])


def _prep(x, tables, layer, g_pre, w1, wrope, b_gate, g_q, g_kv, wq, wkv):
    t = x.shape[0]
    tm = PREP_TM
    ca, sa, cb, sb = tables

    def row(n):
        return pl.BlockSpec((tm, n), lambda i: (i, 0))

    def out(n, dt=BF16):
        return jax.ShapeDtypeStruct((t, n), dt)

    nq = MLA_HEADS * MLA_HEAD_PAD
    consts = [w1, wrope, b_gate, g_q, g_kv, wq, wkv]
    return pl.pallas_call(
        _prep_kernel,
        grid=(t // tm,),
        in_specs=[row(D_MODEL), _layer_spec(g_pre, layer), row(LANES), row(LANES), row(LANES), row(LANES)]
                 + [_layer_spec(c, layer) for c in consts],
        out_specs=[row(nq), row(nq), row(nq), row(512), row(256), row(512), row(512), row(512), row(512),
                   row(N_BRANCHES * D_MODEL)],
        out_shape=[out(nq), out(nq), out(nq), out(512), out(256), out(512), out(512), out(512), out(512),
                   out(N_BRANCHES * D_MODEL, F32)],
        compiler_params=_params(("parallel",)),
        name="prep",
    )(x, g_pre, ca, sa, cb, sb, *consts)


def _mla_kernel(q_ref, k_ref, v_ref, o_ref, m_ref, acc_ref):
    qi = pl.program_id(2)
    tq, tk = MLA_TQ, MLA_TK
    n_heads = 2 * MLA_PAIRS
    m_ref[...] = jnp.full(m_ref.shape, NEG_BIG, F32)
    acc_ref[...] = jnp.zeros(acc_ref.shape, F32)

    def head_lanes(h):
        return slice(h * MLA_HEAD_PAD, (h + 1) * MLA_HEAD_PAD)

    def process(row_sets):
        units = [(rows, chunks, h) for rows, chunks in row_sets for h in range(n_heads)]

        def key_rows(first, keys):
            return pl.ds(pl.multiple_of(first, MLA_DIAG), keys)

        def scores(unit):
            rows, chunks, h = unit
            return [_dot_nt(q_ref[rows, head_lanes(h)], k_ref[key_rows(first, keys), head_lanes(h)])
                    for first, keys, _ in chunks]

        ahead = scores(units[0])
        for n, (rows, chunks, h) in enumerate(units):
            s_unit, ahead = ahead, (scores(units[n + 1]) if n + 1 < len(units) else None)
            n_rows = rows.stop - rows.start
            parts = []
            for (_, keys, masked), s in zip(chunks, s_unit):
                if masked:
                    assert keys == n_rows
                    row = lax.broadcasted_iota(jnp.int32, (n_rows, keys), 0)
                    col = lax.broadcasted_iota(jnp.int32, (n_rows, keys), 1)
                    s = jnp.where(col <= row, s, NEG_BIG)
                parts.append(s)
            s = parts[0] if len(parts) == 1 else jnp.concatenate(parts, axis=1)
            width = sum(keys for _, keys, _ in chunks)
            v = v_ref[key_rows(chunks[0][0], width), head_lanes(h)]
            m_old = m_ref[h, rows]
            m_new = jnp.maximum(m_old, jnp.max(s, axis=-1, keepdims=True))
            alpha = jnp.exp2(m_old - m_new)
            p = jnp.exp2(s - jnp.concatenate([m_new] * (width // LANES), axis=1))
            acc_ref[h, rows] = alpha * acc_ref[h, rows] + _dot(p.astype(BF16), v)
            m_ref[h, rows] = m_new

    assert tq == 2 * tk

    def body(i, carry):
        process([(slice(0, tq), [(2 * i * tk, tk, False), ((2 * i + 1) * tk, tk, False)])])
        return carry

    lax.fori_loop(0, qi, body, 0)

    base = qi * tq
    diagonal = []
    for r in range(tq // MLA_DIAG):
        before = [(base, r * MLA_DIAG, False)] if r else []
        diagonal.append((slice(r * MLA_DIAG, (r + 1) * MLA_DIAG),
                         before + [(base + r * MLA_DIAG, MLA_DIAG, True)]))
    process(diagonal)

    first_half = lax.broadcasted_iota(jnp.int32, (tq, LANES), 1) < MLA_V
    for p in range(MLA_PAIRS):
        a0, a1 = acc_ref[2 * p], acc_ref[2 * p + 1]
        num = jnp.where(first_half, a0, a1)
        den = jnp.where(first_half, pltpu.roll(a0, MLA_V, axis=1), pltpu.roll(a1, MLA_V, axis=1))
        o_ref[:, p * LANES:(p + 1) * LANES] = (num / den).astype(o_ref.dtype)


def _mla_attention(qm, km, vm, batch, seq):
    t = qm.shape[0]
    tq = MLA_TQ
    nq = seq // tq
    wide = MLA_PAIRS * 2 * MLA_HEAD_PAD
    return pl.pallas_call(
        _mla_kernel,
        grid=(batch, HEAD_PAIRS // MLA_PAIRS, nq),
        in_specs=[pl.BlockSpec((tq, wide), lambda b, p, i: (b * nq + i, p)),
                  pl.BlockSpec((seq, wide), lambda b, p, i: (b, p)),
                  pl.BlockSpec((seq, wide), lambda b, p, i: (b, p))],
        out_specs=pl.BlockSpec((tq, MLA_PAIRS * LANES), lambda b, p, i: (b * nq + i, p)),
        out_shape=jax.ShapeDtypeStruct((t, HEAD_PAIRS * LANES), BF16),
        scratch_shapes=[pltpu.VMEM((2 * MLA_PAIRS, tq, LANES), F32),
                        pltpu.VMEM((2 * MLA_PAIRS, tq, LANES), F32)],
        compiler_params=_params(("parallel", "parallel", "parallel")),
        name="mla_attention",
    )(qm, km, vm)


def _sb_kernel(q_ref, k_ref, v_ref, tri_ref, o_ref, qh_ref, carry_ref, acc_ref):
    qi = pl.program_id(2)
    tq, tk, blk = SB_TQ, SB_TK, SB_BLK
    carry_ref[...] = jnp.zeros(carry_ref.shape, F32)
    acc_ref[...] = jnp.zeros(acc_ref.shape, F32)
    n_heads = 2 * SB_PAIRS
    first_half = lax.broadcasted_iota(jnp.int32, (tq, LANES), 1) < SB_HEAD_DIM
    for p in range(SB_PAIRS):
        q2 = q_ref[:, p * LANES:(p + 1) * LANES]
        zero = jnp.zeros_like(q2)
        qh_ref[2 * p] = jnp.where(first_half, q2, zero)
        qh_ref[2 * p + 1] = jnp.where(first_half, zero, q2)

    def pair_lanes(h):
        return slice((h // 2) * LANES, (h // 2 + 1) * LANES)

    def weights_pv(h, z, v, masked):
        sp = jnp.maximum(z, jnp.log2(1.0 + jnp.exp2(jnp.minimum(z, SP_CLAMP))))
        own = z - sp
        if masked:
            row = lax.broadcasted_iota(jnp.int32, (tq, tk), 0)
            col = lax.broadcasted_iota(jnp.int32, (tq, tk), 1)
            valid = col < row
            sp = jnp.where(valid, sp, 0.0)
        sp16 = sp.astype(BF16)
        c = carry_ref[h]
        expo = [None] * (tk // blk)
        for b in reversed(range(tk // blk)):
            cols = slice(b * blk, (b + 1) * blk)
            later = _dot(sp16[:, cols], tri_ref[...])
            expo[b] = own[:, cols] - later - jnp.concatenate([c] * (blk // LANES), axis=1)
            c = c + jnp.sum(sp[:, cols], axis=-1, keepdims=True)
        carry_ref[h] = c
        a = jnp.exp2(jnp.concatenate(expo, axis=1))
        if masked:
            a = jnp.where(valid, a, 0.0)
        acc_ref[h] += _dot(a.astype(BF16), v)

    def process(chunks):
        starts = [pl.multiple_of(j * tk, tk) for j, _ in chunks]
        keys = [k_ref[pl.ds(st, tk), :] for st in starts]
        vals = [v_ref[pl.ds(st, tk), :] for st in starts]
        logits = [[_dot_nt(qh_ref[h], k[:, pair_lanes(h)]) for h in range(n_heads)] for k in keys]
        for (_, masked), z_heads, v in zip(chunks, logits, vals):
            for h, z in enumerate(z_heads):
                weights_pv(h, z, v[:, pair_lanes(h)], masked)

    def min_carry():
        c = carry_ref[0]
        for h in range(1, n_heads):
            c = jnp.minimum(c, carry_ref[h])
        return jnp.min(c)

    def more(state):
        j, cmin = state
        return jnp.logical_and(j >= 0, cmin < SB_SKIP)

    def step(state):
        j, _ = state
        process([(j, False)])
        return j - 1, min_carry()

    @pl.when(qi == 0)
    def _():
        process([(qi, True)])

    @pl.when(qi > 0)
    def _():
        process([(qi, True), (qi - 1, False)])

    lax.while_loop(more, step, (qi - 2, min_carry()))

    for p in range(SB_PAIRS):
        o_ref[:, p * LANES:(p + 1) * LANES] = jnp.where(
            first_half, acc_ref[2 * p], acc_ref[2 * p + 1]).astype(o_ref.dtype)


def _sb_attention(qb, kb, vb, tri, batch, seq):
    t = qb.shape[0]
    tq = SB_TQ
    nq = seq // tq
    wide = SB_PAIRS * LANES
    return pl.pallas_call(
        _sb_kernel,
        grid=(batch, HEAD_PAIRS // SB_PAIRS, nq),
        in_specs=[pl.BlockSpec((tq, wide), lambda b, p, i: (b * nq + i, p)),
                  pl.BlockSpec((seq, wide), lambda b, p, i: (b, p)),
                  pl.BlockSpec((seq, wide), lambda b, p, i: (b, p)),
                  _const_spec(tri.shape)],
        out_specs=pl.BlockSpec((tq, wide), lambda b, p, i: (b * nq + i, p)),
        out_shape=jax.ShapeDtypeStruct((t, HEAD_PAIRS * LANES), BF16),
        scratch_shapes=[pltpu.VMEM((2 * SB_PAIRS, tq, LANES), BF16), pltpu.VMEM((2 * SB_PAIRS, tq, LANES), F32),
                        pltpu.VMEM((2 * SB_PAIRS, tq, LANES), F32)],
        compiler_params=_params(("parallel", "parallel", "parallel")),
        name="sb_attention",
    )(qb, kb, vb, tri)


def _swa_kernel(sink_ref, q_ref, k_ref, v_ref, kp_ref, vp_ref, o_ref, *, layer, tiles_per_seq):
    i = pl.program_id(0)
    w, tq = SWA_WINDOW, SWA_TQ
    has_prev = (i % tiles_per_seq) != 0
    kcat = jnp.concatenate([kp_ref[...], k_ref[...]], axis=0)
    vcat = jnp.concatenate([vp_ref[...], v_ref[...]], axis=0)
    row = lax.broadcasted_iota(jnp.int32, (w, 2 * w), 0)
    col = lax.broadcasted_iota(jnp.int32, (w, 2 * w), 1)
    band = (col > row) & (col <= row + w)
    band_first = band & ((col >= w) | has_prev)
    first_half = lax.broadcasted_iota(jnp.int32, (w, LANES), 1) < SWA_HEAD_DIM
    def kv_head(p):
        return p // (HEAD_PAIRS // SWA_KV_HEADS)

    def scores(p):
        q2 = q_ref[:, p * LANES:(p + 1) * LANES]
        kg = kcat[:, kv_head(p) * LANES:(kv_head(p) + 1) * LANES]
        half = lax.broadcasted_iota(jnp.int32, q2.shape, 1) < SWA_HEAD_DIM
        zero = jnp.zeros_like(q2)
        return [_dot_nt(jnp.where(half, q2, zero), kg), _dot_nt(jnp.where(half, zero, q2), kg)]

    ahead = scores(0)
    for p in range(HEAD_PAIRS):
        g = kv_head(p)
        s_full, ahead = ahead, (scores(p + 1) if p + 1 < HEAD_PAIRS else None)
        for r in range(tq // w):
            keys = slice(r * w, (r + 2) * w)
            acc, esink = [], []
            for hh in range(2):
                sink = sink_ref[layer, 2 * p + hh] * LOG2E
                s = jnp.where(band_first if r == 0 else band, s_full[hh][r * w:(r + 1) * w, keys], NEG_BIG)
                m = jnp.maximum(jnp.broadcast_to(jnp.max(s, axis=-1, keepdims=True), (w, LANES)), sink)
                prob = jnp.exp2(s - jnp.concatenate([m, m], axis=1))
                vh = vcat[keys, (2 * g + hh) * LANES:(2 * g + hh + 1) * LANES]
                acc.append(_dot(prob.astype(BF16), vh))
                esink.append(jnp.exp2(sink - m))
            num = jnp.where(first_half, acc[0], acc[1])
            den = (jnp.where(first_half, pltpu.roll(acc[0], SWA_HEAD_DIM, axis=1),
                             pltpu.roll(acc[1], SWA_HEAD_DIM, axis=1))
                   + jnp.where(first_half, esink[0], esink[1]))
            o_ref[r * w:(r + 1) * w, p * LANES:(p + 1) * LANES] = (num / den).astype(o_ref.dtype)


def _swa_attention(sinks, layer, qs, ks, vs, seq):
    t = qs.shape[0]
    tq = SWA_TQ
    per_tile = tq // SWA_WINDOW
    cur = lambda n: pl.BlockSpec((tq, n), lambda i: (i, 0))
    prev = lambda n: pl.BlockSpec((SWA_WINDOW, n), lambda i: (jnp.maximum(i * per_tile - 1, 0), 0))
    return pl.pallas_call(
        functools.partial(_swa_kernel, layer=layer, tiles_per_seq=seq // tq),
        grid=(t // tq,),
        in_specs=[pl.BlockSpec(memory_space=pltpu.SMEM), cur(512), cur(256), cur(512), prev(256), prev(512)],
        out_specs=cur(512),
        out_shape=jax.ShapeDtypeStruct((t, 512), BF16),
        compiler_params=_params(("parallel",)),
        name="swa_attention",
    )(sinks, qs, ks, vs, ks, vs)


def _post_kernel(x_ref, oa_ref, ob_ref, oc_ref, gate_ref, woa_ref, wob_ref, woc_ref, wout_ref,
                 gpost_ref, gpre_ref, wup_ref, wdown_ref, gmlp_ref, out_ref):
    d = D_MODEL
    tm = x_ref.shape[0]
    halves = [slice(0, tm // 2), slice(tm // 2, tm)]
    merged = []
    for r in halves:
        mixed = (gate_ref[r, 0:d] * _dot(oa_ref[r, :], woa_ref[...])
                 + gate_ref[r, d:2 * d] * _dot(ob_ref[r, :], wob_ref[...])
                 + gate_ref[r, 2 * d:3 * d] * _dot(oc_ref[r, :], woc_ref[...]))
        merged.append(_dot(mixed.astype(BF16), wout_ref[...]))
    x1 = [x_ref[r, :] + _rms(y, gpost_ref[...]) for r, y in zip(halves, merged)]
    hidden = [_rms(v, gpre_ref[...]).astype(BF16) for v in x1]
    up = [jnp.square(jnp.maximum(_dot(h, wup_ref[...]), 0.0)).astype(BF16) for h in hidden]
    down = [_dot(u, wdown_ref[...]) for u in up]
    for r, v, y in zip(halves, x1, down):
        out_ref[r, :] = v + _rms(y, gmlp_ref[...])


def _post(x, oa, ob, oc, gates, layer, woa, wob, woc, wout, g_post, g_pre, wup, wdown, g_mlp):
    t = x.shape[0]
    tm = POST_TM
    row = lambda n: pl.BlockSpec((tm, n), lambda i: (i, 0))
    consts = [woa, wob, woc, wout, g_post, g_pre, wup, wdown, g_mlp]
    return pl.pallas_call(
        _post_kernel,
        grid=(t // tm,),
        in_specs=[row(D_MODEL), row(512), row(512), row(512), row(N_BRANCHES * D_MODEL)]
                 + [_layer_spec(c, layer) for c in consts],
        out_specs=row(D_MODEL),
        out_shape=jax.ShapeDtypeStruct((t, D_MODEL), F32),
        compiler_params=_params(("parallel",)),
        name="post",
    )(x, oa, ob, oc, gates, *consts)


def _rot_half(w):
    half = w.shape[-1] // 2
    return jnp.concatenate([-w[..., half:], w[..., :half]], axis=-1)


def _pair_pad(a, b):
    z = jnp.zeros_like(a)
    return jnp.concatenate([a, z, z, b], axis=-1).reshape(*a.shape[:-2], -1)


def _layout_weights(w_in, w_uq, w_ukv):
    dep = w_in.shape[0]
    w1 = jnp.swapaxes(w_in, 1, 2).astype(BF16)
    assert w1.shape[1] == W1_COLS
    k_rope = w1[:, OFF_KROPE:OFF_KROPE + MLA_ROPE, :]
    half = MLA_ROPE // 2
    k_rope_rot = jnp.concatenate([-k_rope[:, half:], k_rope[:, :half]], axis=1)
    z32 = jnp.zeros((dep, 32, D_MODEL), BF16)
    wrope = jnp.concatenate([k_rope_rot, z32, k_rope, z32], axis=1)

    uq = w_uq.astype(BF16).reshape(dep, MLA_Q_LORA, MLA_HEADS, MLA_NOPE + MLA_ROPE)
    nope, pe = uq[..., :MLA_NOPE], uq[..., MLA_NOPE:]
    pe_rot = _rot_half(pe)
    zq = jnp.zeros((dep, MLA_Q_LORA, MLA_HEADS, 32), BF16)
    wq_main = jnp.concatenate([nope, pe, zq], axis=-1).reshape(dep, MLA_Q_LORA, -1)
    wq_rot = jnp.concatenate([jnp.zeros_like(nope), pe_rot, zq], axis=-1).reshape(dep, MLA_Q_LORA, -1)
    wq = jnp.concatenate([wq_main, wq_rot], axis=-1)

    ukv = w_ukv.astype(BF16).reshape(dep, MLA_KV_LORA, MLA_HEADS, MLA_NOPE + MLA_V)
    k_nope, v = ukv[..., :MLA_NOPE], ukv[..., MLA_NOPE:]
    wk = jnp.concatenate([k_nope, jnp.zeros_like(k_nope)], axis=-1).reshape(dep, MLA_KV_LORA, -1)
    vp = v.reshape(dep, MLA_KV_LORA, HEAD_PAIRS, 2, MLA_V)
    wkv = jnp.concatenate([wk, _pair_pad(vp[..., 0, :], vp[..., 1, :])], axis=-1)
    return w1, wrope, wq, wkv


def kernel(x, positions, g_mix_pre, w_in, b_gate, g_q_lat, g_kv_lat, w_uq, w_ukv, swa_sinks, w_o_mla, w_o_swa, w_o_sb, w_out, g_mix_post, g_mlp_pre, w_up, w_down, g_mlp_post):
    batch, seq, d = x.shape
    t = batch * seq
    tables = _rope_tables(positions)
    w1, wrope, wq, wkv = _layout_weights(w_in, w_uq, w_ukv)
    woa, wob, woc, wout, wup, wdown = (w.astype(BF16) for w in (w_o_mla, w_o_swa, w_o_sb, w_out, w_up, w_down))
    tri = (lax.broadcasted_iota(jnp.int32, (SB_BLK, SB_BLK), 0)
           > lax.broadcasted_iota(jnp.int32, (SB_BLK, SB_BLK), 1)).astype(BF16)
    rows = lambda g: g.reshape(DEPTH, 1, -1)
    g_pre, bg, g_q, g_kv = rows(g_mix_pre), rows(b_gate), rows(g_q_lat), rows(g_kv_lat)
    g_post, g_mlp_in, g_mlp_out = rows(g_mix_post), rows(g_mlp_pre), rows(g_mlp_post)

    xt = x.reshape(t, d)
    for l in range(DEPTH):
        qm, km, vm, qs, ks, vs, qb, kb, vb, gates = _prep(xt, tables, l, g_pre, w1, wrope, bg, g_q, g_kv,
                                                          wq, wkv)
        oa = _mla_attention(qm, km, vm, batch, seq)
        ob = _swa_attention(swa_sinks, l, qs, ks, vs, seq)
        oc = _sb_attention(qb, kb, vb, tri, batch, seq)
        xt = _post(xt, oa, ob, oc, gates, l, woa, wob, woc, wout, g_post, g_mlp_in, wup, wdown, g_mlp_out)
    return xt.reshape(batch, seq, d)
```

```python
import functools

import jax
import jax.numpy as jnp
from jax import lax
from jax.experimental import pallas as pl
from jax.experimental.pallas import tpu as pltpu

F32 = jnp.float32
BF16 = jnp.bfloat16

D_MODEL = 1024
DEPTH = 4
MLA_HEADS = 8
MLA_Q_LORA = 256
MLA_KV_LORA = 128
MLA_NOPE = 64
MLA_ROPE = 32
MLA_V = 64
SWA_HEADS = 8
SWA_KV_HEADS = 2
SWA_HEAD_DIM = 64
SWA_WINDOW = 128
SB_HEADS = 8
SB_HEAD_DIM = 64
D_FF = 4 * D_MODEL
ROPE_THETA = 10000.0
EPS = 1e-6
N_BRANCHES = 3

LANES = 128
HEAD_PAIRS = 4
MLA_HEAD_PAD = 128
NEG_BIG = -1e30
LOG2E = 1.4426950408889634
SB_SKIP = 136.0
SP_CLAMP = 126.0
VMEM_LIMIT = 56 * 1024 * 1024

OFF_CQ = 0
OFF_CKV = OFF_CQ + MLA_Q_LORA
OFF_KROPE = OFF_CKV + MLA_KV_LORA
OFF_QS = OFF_KROPE + MLA_ROPE
OFF_KS = OFF_QS + 512
OFF_VS = OFF_KS + 128
OFF_QB = OFF_VS + 128
OFF_KB = OFF_QB + 512
OFF_VB = OFF_KB + 512
OFF_GATE = OFF_VB + 512
W1_COLS = OFF_GATE + N_BRANCHES * D_MODEL

PREP_TM = 256
POST_TM = 256
MLA_TQ = 1024
MLA_TK = 512
MLA_PAIRS = 2
MLA_DIAG = 256
SB_TQ = 512
SB_TK = 256
SB_BLK = 256
SB_PAIRS = 4
SWA_TQ = 512
ROPE_TM = 2048


def _rms(x, g):
    return x * lax.rsqrt(jnp.mean(x * x, axis=-1, keepdims=True) + EPS) * g


def _dot(a, b):
    return jnp.dot(a, b, preferred_element_type=F32)


def _dot_nt(a, b):
    return lax.dot_general(a, b, (((1,), (1,)), ((), ())), preferred_element_type=F32)


def _const_spec(shape):
    return pl.BlockSpec(shape, lambda *_: (0,) * len(shape), pipeline_mode=pl.Buffered(1))


def _layer_spec(arr, layer):
    return pl.BlockSpec((None,) + arr.shape[1:], lambda *_: (layer, 0, 0), pipeline_mode=pl.Buffered(1))


def _params(sem):
    return pltpu.CompilerParams(dimension_semantics=sem, vmem_limit_bytes=VMEM_LIMIT)


def _pair_ones(width):
    lane = lax.broadcasted_iota(jnp.int32, (1, width), 1) % (2 * LANES)
    return ((lane >= 64) & (lane < 2 * LANES - 64)).astype(F32)


def _rope_table_kernel(pos_ref, inv_ref, ca_ref, sa_ref, cb_ref, sb_ref):
    ang = pos_ref[...].astype(F32) * inv_ref[...]
    cos, sin = jnp.cos(ang), jnp.sin(ang)
    lane = lax.broadcasted_iota(jnp.int32, (1, LANES), 1)
    low = lane < SWA_HEAD_DIM
    rope_lanes = (lane >= MLA_NOPE) & (lane < MLA_NOPE + MLA_ROPE)
    ca_ref[...] = jnp.where(rope_lanes, cos, jnp.where(low, 1.0, 0.0))
    sa_ref[...] = jnp.where(rope_lanes, sin, 0.0)
    cb_ref[...] = jnp.where(low, cos, pltpu.roll(cos, SWA_HEAD_DIM, axis=1))
    sb_ref[...] = jnp.where(low, sin, pltpu.roll(sin, SWA_HEAD_DIM, axis=1))


def _rope_tables(positions):
    t = positions.size
    pos = positions.reshape(t, 1)
    inv_a16 = 1.0 / (ROPE_THETA ** (jnp.arange(0, MLA_ROPE, 2, dtype=F32) / MLA_ROPE))
    inv_b32 = 1.0 / (ROPE_THETA ** (jnp.arange(0, SWA_HEAD_DIM, 2, dtype=F32) / SWA_HEAD_DIM))
    inv = jnp.concatenate([inv_b32, inv_b32, inv_a16, inv_a16, jnp.zeros((32,), F32)]).reshape(1, LANES)
    row = pl.BlockSpec((ROPE_TM, LANES), lambda i: (i, 0))
    out = jax.ShapeDtypeStruct((t, LANES), F32)
    return pl.pallas_call(
        _rope_table_kernel,
        grid=(t // ROPE_TM,),
        in_specs=[pl.BlockSpec((ROPE_TM, 1), lambda i: (i, 0)), pl.BlockSpec((1, LANES), lambda i: (0, 0))],
        out_specs=[row, row, row, row],
        out_shape=[out, out, out, out],
        compiler_params=_params(("parallel",)),
        name="rope_tables",
    )(pos, inv)


def _prep_kernel(x_ref, g_ref, ca_ref, sa_ref, cb_ref, sb_ref, w1_ref, wrope_ref, bg_ref, gq_ref, gkv_ref,
                 wq_ref, wkv_ref,
                 qm_ref, km_ref, vm_ref, qs_ref, ks_ref, vs_ref, qb_ref, kb_ref, vb_ref, gate_ref):
    h = _rms(x_ref[...], g_ref[...]).astype(BF16)

    def mm(lo, n):
        return _dot_nt(h, w1_ref[lo:lo + n, :])

    ca, sa, cb, sb = ca_ref[...], sa_ref[...], cb_ref[...], sb_ref[...]

    lane = lax.broadcasted_iota(jnp.int32, (1, LANES), 1)
    low_head = lane < SWA_HEAD_DIM
    first_half = lane % SWA_HEAD_DIM < SWA_HEAD_DIM // 2

    def roll(v, shift):
        return pltpu.roll(v, shift, axis=1)

    def rope64(v, cos, sin):
        half = SWA_HEAD_DIM // 2
        rot = jnp.where(first_half, -roll(v, LANES - half), roll(v, half))
        return v * cos + rot * sin

    cqn = _rms(mm(OFF_CQ, MLA_Q_LORA), gq_ref[...]).astype(BF16)
    ckvn = _rms(mm(OFF_CKV, MLA_KV_LORA), gkv_ref[...]).astype(BF16)
    kpe_blk = _dot_nt(h, wrope_ref[...])

    q_scale = SWA_HEAD_DIM ** -0.5 * LOG2E
    qs = mm(OFF_QS, 512)
    for p in range(HEAD_PAIRS):
        blk = slice(p * LANES, (p + 1) * LANES)
        qs_ref[:, blk] = rope64(qs[:, blk], cb * q_scale, sb * q_scale).astype(BF16)
    ks = rope64(mm(OFF_KS, LANES), cb, sb)
    ks_swap = roll(ks, SWA_HEAD_DIM)
    ks_ref[:, :LANES] = jnp.where(low_head, ks, ks_swap).astype(BF16)
    ks_ref[:, LANES:] = jnp.where(low_head, ks_swap, ks).astype(BF16)
    vs = mm(OFF_VS, LANES)
    vs_swap = roll(vs, SWA_HEAD_DIM)
    for n, blk_val in enumerate([jnp.where(low_head, vs, 1.0), jnp.where(low_head, 1.0, vs_swap),
                                 jnp.where(low_head, vs_swap, 1.0), jnp.where(low_head, 1.0, vs)]):
        vs_ref[:, n * LANES:(n + 1) * LANES] = blk_val.astype(BF16)

    qb_ref[...] = (mm(OFF_QB, 512) * (SB_HEAD_DIM ** -0.5 * LOG2E)).astype(BF16)
    kb_ref[...] = mm(OFF_KB, 512).astype(BF16)
    vb_ref[...] = mm(OFF_VB, 512).astype(BF16)

    scale = (MLA_NOPE + MLA_ROPE) ** -0.5 * LOG2E
    ca8 = jnp.concatenate([ca * scale] * MLA_HEADS, axis=1)
    sa8 = jnp.concatenate([sa * scale] * MLA_HEADS, axis=1)
    nq = MLA_HEADS * MLA_HEAD_PAD
    qm_ref[...] = (_dot(cqn, wq_ref[:, :nq]) * ca8 + _dot(cqn, wq_ref[:, nq:]) * sa8).astype(BF16)
    rope_lanes = (lane >= MLA_NOPE) & (lane < MLA_NOPE + MLA_ROPE)
    kpe = jnp.where(rope_lanes, kpe_blk * ca + roll(kpe_blk, MLA_NOPE) * sa, 0.0)
    kpe8 = jnp.concatenate([kpe] * MLA_HEADS, axis=1)
    km_ref[...] = (_dot(ckvn, wkv_ref[:, :nq]) + kpe8).astype(BF16)
    vm_ref[...] = (_dot(ckvn, wkv_ref[:, nq:]) + _pair_ones(nq)).astype(BF16)

    gate_ref[...] = jax.nn.sigmoid(mm(OFF_GATE, N_BRANCHES * D_MODEL) + bg_ref[...])


def _prep(x, tables, layer, g_pre, w1, wrope, b_gate, g_q, g_kv, wq, wkv):
    t = x.shape[0]
    tm = PREP_TM
    ca, sa, cb, sb = tables

    def row(n):
        return pl.BlockSpec((tm, n), lambda i: (i, 0))

    def out(n, dt=BF16):
        return jax.ShapeDtypeStruct((t, n), dt)

    nq = MLA_HEADS * MLA_HEAD_PAD
    consts = [w1, wrope, b_gate, g_q, g_kv, wq, wkv]
    return pl.pallas_call(
        _prep_kernel,
        grid=(t // tm,),
        in_specs=[row(D_MODEL), _layer_spec(g_pre, layer), row(LANES), row(LANES), row(LANES), row(LANES)]
                 + [_layer_spec(c, layer) for c in consts],
        out_specs=[row(nq), row(nq), row(nq), row(512), row(256), row(512), row(512), row(512), row(512),
                   row(N_BRANCHES * D_MODEL)],
        out_shape=[out(nq), out(nq), out(nq), out(512), out(256), out(512), out(512), out(512), out(512),
                   out(N_BRANCHES * D_MODEL, F32)],
        compiler_params=_params(("parallel",)),
        name="prep",
    )(x, g_pre, ca, sa, cb, sb, *consts)


def _mla_kernel(q_ref, k_ref, v_ref, o_ref, m_ref, acc_ref):
    qi = pl.program_id(2)
    tq, tk = MLA_TQ, MLA_TK
    n_heads = 2 * MLA_PAIRS
    m_ref[...] = jnp.full(m_ref.shape, NEG_BIG, F32)
    acc_ref[...] = jnp.zeros(acc_ref.shape, F32)

    def head_lanes(h):
        return slice(h * MLA_HEAD_PAD, (h + 1) * MLA_HEAD_PAD)

    def process(row_sets):
        units = [(rows, chunks, h) for rows, chunks in row_sets for h in range(n_heads)]

        def key_rows(first, keys):
            return pl.ds(pl.multiple_of(first, MLA_DIAG), keys)

        def scores(unit):
            rows, chunks, h = unit
            return [_dot_nt(q_ref[rows, head_lanes(h)], k_ref[key_rows(first, keys), head_lanes(h)])
                    for first, keys, _ in chunks]

        ahead = scores(units[0])
        for n, (rows, chunks, h) in enumerate(units):
            s_unit, ahead = ahead, (scores(units[n + 1]) if n + 1 < len(units) else None)
            n_rows = rows.stop - rows.start
            parts = []
            for (_, keys, masked), s in zip(chunks, s_unit):
                if masked:
                    assert keys == n_rows
                    row = lax.broadcasted_iota(jnp.int32, (n_rows, keys), 0)
                    col = lax.broadcasted_iota(jnp.int32, (n_rows, keys), 1)
                    s = jnp.where(col <= row, s, NEG_BIG)
                parts.append(s)
            s = parts[0] if len(parts) == 1 else jnp.concatenate(parts, axis=1)
            width = sum(keys for _, keys, _ in chunks)
            v = v_ref[key_rows(chunks[0][0], width), head_lanes(h)]
            m_old = m_ref[h, rows]
            m_new = jnp.maximum(m_old, jnp.max(s, axis=-1, keepdims=True))
            alpha = jnp.exp2(m_old - m_new)
            p = jnp.exp2(s - jnp.concatenate([m_new] * (width // LANES), axis=1))
            acc_ref[h, rows] = alpha * acc_ref[h, rows] + _dot(p.astype(BF16), v)
            m_ref[h, rows] = m_new

    assert tq == 2 * tk

    def body(i, carry):
        process([(slice(0, tq), [(2 * i * tk, tk, False), ((2 * i + 1) * tk, tk, False)])])
        return carry

    lax.fori_loop(0, qi, body, 0)

    base = qi * tq
    diagonal = []
    for r in range(tq // MLA_DIAG):
        before = [(base, r * MLA_DIAG, False)] if r else []
        diagonal.append((slice(r * MLA_DIAG, (r + 1) * MLA_DIAG),
                         before + [(base + r * MLA_DIAG, MLA_DIAG, True)]))
    process(diagonal)

    first_half = lax.broadcasted_iota(jnp.int32, (tq, LANES), 1) < MLA_V
    for p in range(MLA_PAIRS):
        a0, a1 = acc_ref[2 * p], acc_ref[2 * p + 1]
        num = jnp.where(first_half, a0, a1)
        den = jnp.where(first_half, pltpu.roll(a0, MLA_V, axis=1), pltpu.roll(a1, MLA_V, axis=1))
        o_ref[:, p * LANES:(p + 1) * LANES] = (num / den).astype(o_ref.dtype)


def _mla_attention(qm, km, vm, batch, seq):
    t = qm.shape[0]
    tq = MLA_TQ
    nq = seq // tq
    wide = MLA_PAIRS * 2 * MLA_HEAD_PAD
    return pl.pallas_call(
        _mla_kernel,
        grid=(batch, HEAD_PAIRS // MLA_PAIRS, nq),
        in_specs=[pl.BlockSpec((tq, wide), lambda b, p, i: (b * nq + i, p)),
                  pl.BlockSpec((seq, wide), lambda b, p, i: (b, p)),
                  pl.BlockSpec((seq, wide), lambda b, p, i: (b, p))],
        out_specs=pl.BlockSpec((tq, MLA_PAIRS * LANES), lambda b, p, i: (b * nq + i, p)),
        out_shape=jax.ShapeDtypeStruct((t, HEAD_PAIRS * LANES), BF16),
        scratch_shapes=[pltpu.VMEM((2 * MLA_PAIRS, tq, LANES), F32),
                        pltpu.VMEM((2 * MLA_PAIRS, tq, LANES), F32)],
        compiler_params=_params(("parallel", "parallel", "parallel")),
        name="mla_attention",
    )(qm, km, vm)


def _sb_kernel(q_ref, k_ref, v_ref, tri_ref, o_ref, qh_ref, carry_ref, acc_ref):
    qi = pl.program_id(2)
    tq, tk, blk = SB_TQ, SB_TK, SB_BLK
    carry_ref[...] = jnp.zeros(carry_ref.shape, F32)
    acc_ref[...] = jnp.zeros(acc_ref.shape, F32)
    n_heads = 2 * SB_PAIRS
    first_half = lax.broadcasted_iota(jnp.int32, (tq, LANES), 1) < SB_HEAD_DIM
    for p in range(SB_PAIRS):
        q2 = q_ref[:, p * LANES:(p + 1) * LANES]
        zero = jnp.zeros_like(q2)
        qh_ref[2 * p] = jnp.where(first_half, q2, zero)
        qh_ref[2 * p + 1] = jnp.where(first_half, zero, q2)

    def pair_lanes(h):
        return slice((h // 2) * LANES, (h // 2 + 1) * LANES)

    def weights_pv(h, rows, z, v, masked):
        sp = jnp.maximum(z, jnp.log2(1.0 + jnp.exp2(jnp.minimum(z, SP_CLAMP))))
        own = z - sp
        if masked:
            row = lax.broadcasted_iota(jnp.int32, (tk, tk), 0)
            col = lax.broadcasted_iota(jnp.int32, (tk, tk), 1)
            valid = col < row
            sp = jnp.where(valid, sp, 0.0)
        sp16 = sp.astype(BF16)
        c = carry_ref[h, rows]
        expo = [None] * (tk // blk)
        for b in reversed(range(tk // blk)):
            cols = slice(b * blk, (b + 1) * blk)
            later = _dot(sp16[:, cols], tri_ref[...])
            expo[b] = own[:, cols] - later - jnp.concatenate([c] * (blk // LANES), axis=1)
            c = c + jnp.sum(sp[:, cols], axis=-1, keepdims=True)
        carry_ref[h, rows] = c
        a = jnp.exp2(jnp.concatenate(expo, axis=1))
        if masked:
            a = jnp.where(valid, a, 0.0)
        acc_ref[h, rows] += _dot(a.astype(BF16), v)

    def process(items):
        def chunk(j):
            return pl.ds(pl.multiple_of(j * tk, tk), tk)

        logits = [[_dot_nt(qh_ref[h, rows], k_ref[chunk(j), pair_lanes(h)]) for h in range(n_heads)]
                  for rows, j, _ in items]
        for (rows, j, masked), z_heads in zip(items, logits):
            for h, z in enumerate(z_heads):
                weights_pv(h, rows, z, v_ref[chunk(j), pair_lanes(h)], masked)

    def min_carry(rows):
        c = carry_ref[0, rows]
        for h in range(1, n_heads):
            c = jnp.minimum(c, carry_ref[h, rows])
        return jnp.min(c)

    n_blocks = tq // tk
    blocks = [slice(r * tk, (r + 1) * tk) for r in range(n_blocks)]
    diag_chunk = [qi * n_blocks + r for r in range(n_blocks)]
    diagonal = [(blocks[r], diag_chunk[r], True) for r in range(n_blocks)]

    @pl.when(qi == 0)
    def _():
        process(diagonal + [(blocks[r], diag_chunk[r] - 1, False) for r in range(1, n_blocks)])

    @pl.when(qi > 0)
    def _():
        process(diagonal + [(blocks[r], diag_chunk[r] - 1, False) for r in range(n_blocks)])

    def more(state):
        j, cmin = state
        return jnp.logical_and(j >= 0, cmin < SB_SKIP)

    for r in range(n_blocks):
        def step(state, rows=blocks[r]):
            j, _ = state
            process([(rows, j, False)])
            return j - 1, min_carry(rows)

        lax.while_loop(more, step, (diag_chunk[r] - 2, min_carry(blocks[r])))

    for p in range(SB_PAIRS):
        o_ref[:, p * LANES:(p + 1) * LANES] = jnp.where(
            first_half, acc_ref[2 * p], acc_ref[2 * p + 1]).astype(o_ref.dtype)


def _sb_attention(qb, kb, vb, tri, batch, seq):
    t = qb.shape[0]
    tq = SB_TQ
    nq = seq // tq
    wide = SB_PAIRS * LANES
    return pl.pallas_call(
        _sb_kernel,
        grid=(batch, HEAD_PAIRS // SB_PAIRS, nq),
        in_specs=[pl.BlockSpec((tq, wide), lambda b, p, i: (b * nq + i, p)),
                  pl.BlockSpec((seq, wide), lambda b, p, i: (b, p), pipeline_mode=pl.Buffered(1)),
                  pl.BlockSpec((seq, wide), lambda b, p, i: (b, p), pipeline_mode=pl.Buffered(1)),
                  _const_spec(tri.shape)],
        out_specs=pl.BlockSpec((tq, wide), lambda b, p, i: (b * nq + i, p)),
        out_shape=jax.ShapeDtypeStruct((t, HEAD_PAIRS * LANES), BF16),
        scratch_shapes=[pltpu.VMEM((2 * SB_PAIRS, tq, LANES), BF16), pltpu.VMEM((2 * SB_PAIRS, tq, LANES), F32),
                        pltpu.VMEM((2 * SB_PAIRS, tq, LANES), F32)],
        compiler_params=_params(("parallel", "parallel", "parallel")),
        name="sb_attention",
    )(qb, kb, vb, tri)


def _swa_kernel(sink_ref, q_ref, k_ref, v_ref, kp_ref, vp_ref, o_ref, *, layer, tiles_per_seq):
    i = pl.program_id(0)
    w, tq = SWA_WINDOW, SWA_TQ
    has_prev = (i % tiles_per_seq) != 0
    kcat = jnp.concatenate([kp_ref[...], k_ref[...]], axis=0)
    vcat = jnp.concatenate([vp_ref[...], v_ref[...]], axis=0)
    row = lax.broadcasted_iota(jnp.int32, (w, 2 * w), 0)
    col = lax.broadcasted_iota(jnp.int32, (w, 2 * w), 1)
    band = (col > row) & (col <= row + w)
    band_first = band & ((col >= w) | has_prev)
    first_half = lax.broadcasted_iota(jnp.int32, (w, LANES), 1) < SWA_HEAD_DIM
    def kv_head(p):
        return p // (HEAD_PAIRS // SWA_KV_HEADS)

    def scores(p):
        q2 = q_ref[:, p * LANES:(p + 1) * LANES]
        kg = kcat[:, kv_head(p) * LANES:(kv_head(p) + 1) * LANES]
        half = lax.broadcasted_iota(jnp.int32, q2.shape, 1) < SWA_HEAD_DIM
        zero = jnp.zeros_like(q2)
        return [_dot_nt(jnp.where(half, q2, zero), kg), _dot_nt(jnp.where(half, zero, q2), kg)]

    ahead = scores(0)
    for p in range(HEAD_PAIRS):
        g = kv_head(p)
        s_full, ahead = ahead, (scores(p + 1) if p + 1 < HEAD_PAIRS else None)
        for r in range(tq // w):
            keys = slice(r * w, (r + 2) * w)
            acc, esink = [], []
            for hh in range(2):
                sink = sink_ref[layer, 2 * p + hh] * LOG2E
                s = jnp.where(band_first if r == 0 else band, s_full[hh][r * w:(r + 1) * w, keys], NEG_BIG)
                m = jnp.maximum(jnp.broadcast_to(jnp.max(s, axis=-1, keepdims=True), (w, LANES)), sink)
                prob = jnp.exp2(s - jnp.concatenate([m, m], axis=1))
                vh = vcat[keys, (2 * g + hh) * LANES:(2 * g + hh + 1) * LANES]
                acc.append(_dot(prob.astype(BF16), vh))
                esink.append(jnp.exp2(sink - m))
            num = jnp.where(first_half, acc[0], acc[1])
            den = (jnp.where(first_half, pltpu.roll(acc[0], SWA_HEAD_DIM, axis=1),
                             pltpu.roll(acc[1], SWA_HEAD_DIM, axis=1))
                   + jnp.where(first_half, esink[0], esink[1]))
            o_ref[r * w:(r + 1) * w, p * LANES:(p + 1) * LANES] = (num / den).astype(o_ref.dtype)


def _swa_attention(sinks, layer, qs, ks, vs, seq):
    t = qs.shape[0]
    tq = SWA_TQ
    per_tile = tq // SWA_WINDOW
    cur = lambda n: pl.BlockSpec((tq, n), lambda i: (i, 0))
    prev = lambda n: pl.BlockSpec((SWA_WINDOW, n), lambda i: (jnp.maximum(i * per_tile - 1, 0), 0))
    return pl.pallas_call(
        functools.partial(_swa_kernel, layer=layer, tiles_per_seq=seq // tq),
        grid=(t // tq,),
        in_specs=[pl.BlockSpec(memory_space=pltpu.SMEM), cur(512), cur(256), cur(512), prev(256), prev(512)],
        out_specs=cur(512),
        out_shape=jax.ShapeDtypeStruct((t, 512), BF16),
        compiler_params=_params(("parallel",)),
        name="swa_attention",
    )(sinks, qs, ks, vs, ks, vs)


def _post_kernel(x_ref, oa_ref, ob_ref, oc_ref, gate_ref, woa_ref, wob_ref, woc_ref, wout_ref,
                 gpost_ref, gpre_ref, wup_ref, wdown_ref, gmlp_ref, out_ref):
    d = D_MODEL
    tm = x_ref.shape[0]
    halves = [slice(0, tm // 2), slice(tm // 2, tm)]
    merged = []
    for r in halves:
        mixed = (gate_ref[r, 0:d] * _dot(oa_ref[r, :], woa_ref[...])
                 + gate_ref[r, d:2 * d] * _dot(ob_ref[r, :], wob_ref[...])
                 + gate_ref[r, 2 * d:3 * d] * _dot(oc_ref[r, :], woc_ref[...]))
        merged.append(_dot(mixed.astype(BF16), wout_ref[...]))
    x1 = [x_ref[r, :] + _rms(y, gpost_ref[...]) for r, y in zip(halves, merged)]
    hidden = [_rms(v, gpre_ref[...]).astype(BF16) for v in x1]
    up = [jnp.square(jnp.maximum(_dot(h, wup_ref[...]), 0.0)).astype(BF16) for h in hidden]
    down = [_dot(u, wdown_ref[...]) for u in up]
    for r, v, y in zip(halves, x1, down):
        out_ref[r, :] = v + _rms(y, gmlp_ref[...])


def _post(x, oa, ob, oc, gates, layer, woa, wob, woc, wout, g_post, g_pre, wup, wdown, g_mlp):
    t = x.shape[0]
    tm = POST_TM
    row = lambda n: pl.BlockSpec((tm, n), lambda i: (i, 0))
    consts = [woa, wob, woc, wout, g_post, g_pre, wup, wdown, g_mlp]
    return pl.pallas_call(
        _post_kernel,
        grid=(t // tm,),
        in_specs=[row(D_MODEL), row(512), row(512), row(512), row(N_BRANCHES * D_MODEL)]
                 + [_layer_spec(c, layer) for c in consts],
        out_specs=row(D_MODEL),
        out_shape=jax.ShapeDtypeStruct((t, D_MODEL), F32),
        compiler_params=_params(("parallel",)),
        name="post",
    )(x, oa, ob, oc, gates, *consts)


def _rot_half(w):
    half = w.shape[-1] // 2
    return jnp.concatenate([-w[..., half:], w[..., :half]], axis=-1)


def _pair_pad(a, b):
    z = jnp.zeros_like(a)
    return jnp.concatenate([a, z, z, b], axis=-1).reshape(*a.shape[:-2], -1)


def _layout_weights(w_in, w_uq, w_ukv):
    dep = w_in.shape[0]
    w1 = jnp.swapaxes(w_in, 1, 2).astype(BF16)
    assert w1.shape[1] == W1_COLS
    k_rope = w1[:, OFF_KROPE:OFF_KROPE + MLA_ROPE, :]
    half = MLA_ROPE // 2
    k_rope_rot = jnp.concatenate([-k_rope[:, half:], k_rope[:, :half]], axis=1)
    z32 = jnp.zeros((dep, 32, D_MODEL), BF16)
    wrope = jnp.concatenate([k_rope_rot, z32, k_rope, z32], axis=1)

    uq = w_uq.astype(BF16).reshape(dep, MLA_Q_LORA, MLA_HEADS, MLA_NOPE + MLA_ROPE)
    nope, pe = uq[..., :MLA_NOPE], uq[..., MLA_NOPE:]
    pe_rot = _rot_half(pe)
    zq = jnp.zeros((dep, MLA_Q_LORA, MLA_HEADS, 32), BF16)
    wq_main = jnp.concatenate([nope, pe, zq], axis=-1).reshape(dep, MLA_Q_LORA, -1)
    wq_rot = jnp.concatenate([jnp.zeros_like(nope), pe_rot, zq], axis=-1).reshape(dep, MLA_Q_LORA, -1)
    wq = jnp.concatenate([wq_main, wq_rot], axis=-1)

    ukv = w_ukv.astype(BF16).reshape(dep, MLA_KV_LORA, MLA_HEADS, MLA_NOPE + MLA_V)
    k_nope, v = ukv[..., :MLA_NOPE], ukv[..., MLA_NOPE:]
    wk = jnp.concatenate([k_nope, jnp.zeros_like(k_nope)], axis=-1).reshape(dep, MLA_KV_LORA, -1)
    vp = v.reshape(dep, MLA_KV_LORA, HEAD_PAIRS, 2, MLA_V)
    wkv = jnp.concatenate([wk, _pair_pad(vp[..., 0, :], vp[..., 1, :])], axis=-1)
    return w1, wrope, wq, wkv


def kernel(x, positions, g_mix_pre, w_in, b_gate, g_q_lat, g_kv_lat, w_uq, w_ukv, swa_sinks, w_o_mla, w_o_swa, w_o_sb, w_out, g_mix_post, g_mlp_pre, w_up, w_down, g_mlp_post):
    batch, seq, d = x.shape
    t = batch * seq
    tables = _rope_tables(positions)
    w1, wrope, wq, wkv = _layout_weights(w_in, w_uq, w_ukv)
    woa, wob, woc, wout, wup, wdown = (w.astype(BF16) for w in (w_o_mla, w_o_swa, w_o_sb, w_out, w_up, w_down))
    tri = (lax.broadcasted_iota(jnp.int32, (SB_BLK, SB_BLK), 0)
           > lax.broadcasted_iota(jnp.int32, (SB_BLK, SB_BLK), 1)).astype(BF16)
    rows = lambda g: g.reshape(DEPTH, 1, -1)
    g_pre, bg, g_q, g_kv = rows(g_mix_pre), rows(b_gate), rows(g_q_lat), rows(g_kv_lat)
    g_post, g_mlp_in, g_mlp_out = rows(g_mix_post), rows(g_mlp_pre), rows(g_mlp_post)

    xt = x.reshape(t, d)
    for l in range(DEPTH):
        qm, km, vm, qs, ks, vs, qb, kb, vb, gates = _prep(xt, tables, l, g_pre, w1, wrope, bg, g_q, g_kv,
                                                          wq, wkv)
        oa = _mla_attention(qm, km, vm, batch, seq)
        ob = _swa_attention(swa_sinks, l, qs, ks, vs, seq)
        oc = _sb_attention(qb, kb, vb, tri, batch, seq)
        xt = _post(xt, oa, ob, oc, gates, l, woa, wob, woc, wout, g_post, g_mlp_in, wup, wdown, g_mlp_out)
    return xt.reshape(batch, seq, d)
```

```python
import functools

import jax
import jax.numpy as jnp
from jax import lax
from jax.experimental import pallas as pl
from jax.experimental.pallas import tpu as pltpu

F32 = jnp.float32
BF16 = jnp.bfloat16

D_MODEL = 1024
DEPTH = 4
MLA_HEADS = 8
MLA_Q_LORA = 256
MLA_KV_LORA = 128
MLA_NOPE = 64
MLA_ROPE = 32
MLA_V = 64
SWA_HEADS = 8
SWA_KV_HEADS = 2
SWA_HEAD_DIM = 64
SWA_WINDOW = 128
SB_HEADS = 8
SB_HEAD_DIM = 64
D_FF = 4 * D_MODEL
ROPE_THETA = 10000.0
EPS = 1e-6
N_BRANCHES = 3

LANES = 128
HEAD_PAIRS = 4
MLA_HEAD_PAD = 128
NEG_BIG = -1e30
LOG2E = 1.4426950408889634
SB_SKIP = 136.0
SP_CLAMP = 126.0
VMEM_LIMIT = 56 * 1024 * 1024

OFF_CQ = 0
OFF_CKV = OFF_CQ + MLA_Q_LORA
OFF_KROPE = OFF_CKV + MLA_KV_LORA
OFF_QS = OFF_KROPE + MLA_ROPE
OFF_KS = OFF_QS + 512
OFF_VS = OFF_KS + 128
OFF_QB = OFF_VS + 128
OFF_KB = OFF_QB + 512
OFF_VB = OFF_KB + 512
OFF_GATE = OFF_VB + 512
W1_COLS = OFF_GATE + N_BRANCHES * D_MODEL

PREP_TM = 256
POST_TM = 256
MLA_TQ = 1024
MLA_TK = 512
MLA_PAIRS = 2
MLA_DIAG = 256
SB_TQ = 512
SB_TK = 256
SB_BLK = 256
SB_PAIRS = 4
SWA_TQ = 512
ROPE_TM = 2048


def _rms(x, g):
    return x * lax.rsqrt(jnp.mean(x * x, axis=-1, keepdims=True) + EPS) * g


def _dot(a, b):
    return jnp.dot(a, b, preferred_element_type=F32)


def _dot_nt(a, b):
    return lax.dot_general(a, b, (((1,), (1,)), ((), ())), preferred_element_type=F32)


def _const_spec(shape):
    return pl.BlockSpec(shape, lambda *_: (0,) * len(shape), pipeline_mode=pl.Buffered(1))


def _layer_spec(arr, layer):
    return pl.BlockSpec((None,) + arr.shape[1:], lambda *_: (layer, 0, 0), pipeline_mode=pl.Buffered(1))


def _params(sem):
    return pltpu.CompilerParams(dimension_semantics=sem, vmem_limit_bytes=VMEM_LIMIT)


def _pair_ones(width):
    lane = lax.broadcasted_iota(jnp.int32, (1, width), 1) % (2 * LANES)
    return ((lane >= 64) & (lane < 2 * LANES - 64)).astype(F32)


def _rope_table_kernel(pos_ref, inv_ref, ca_ref, sa_ref, cb_ref, sb_ref):
    ang = pos_ref[...].astype(F32) * inv_ref[...]
    cos, sin = jnp.cos(ang), jnp.sin(ang)
    lane = lax.broadcasted_iota(jnp.int32, (1, LANES), 1)
    low = lane < SWA_HEAD_DIM
    rope_lanes = (lane >= MLA_NOPE) & (lane < MLA_NOPE + MLA_ROPE)
    ca_ref[...] = jnp.where(rope_lanes, cos, jnp.where(low, 1.0, 0.0))
    sa_ref[...] = jnp.where(rope_lanes, sin, 0.0)
    cb_ref[...] = jnp.where(low, cos, pltpu.roll(cos, SWA_HEAD_DIM, axis=1))
    sb_ref[...] = jnp.where(low, sin, pltpu.roll(sin, SWA_HEAD_DIM, axis=1))


def _rope_tables(positions):
    t = positions.size
    pos = positions.reshape(t, 1)
    inv_a16 = 1.0 / (ROPE_THETA ** (jnp.arange(0, MLA_ROPE, 2, dtype=F32) / MLA_ROPE))
    inv_b32 = 1.0 / (ROPE_THETA ** (jnp.arange(0, SWA_HEAD_DIM, 2, dtype=F32) / SWA_HEAD_DIM))
    inv = jnp.concatenate([inv_b32, inv_b32, inv_a16, inv_a16, jnp.zeros((32,), F32)]).reshape(1, LANES)
    row = pl.BlockSpec((ROPE_TM, LANES), lambda i: (i, 0))
    out = jax.ShapeDtypeStruct((t, LANES), F32)
    return pl.pallas_call(
        _rope_table_kernel,
        grid=(t // ROPE_TM,),
        in_specs=[pl.BlockSpec((ROPE_TM, 1), lambda i: (i, 0)), pl.BlockSpec((1, LANES), lambda i: (0, 0))],
        out_specs=[row, row, row, row],
        out_shape=[out, out, out, out],
        compiler_params=_params(("parallel",)),
        name="rope_tables",
    )(pos, inv)


def _prep_kernel(x_ref, g_ref, ca_ref, sa_ref, cb_ref, sb_ref, w1_ref, wrope_ref, bg_ref, gq_ref, gkv_ref,
                 wq_ref, wkv_ref,
                 qm_ref, km_ref, vm_ref, qs_ref, ks_ref, vs_ref, qb_ref, kb_ref, vb_ref, gate_ref):
    h = _rms(x_ref[...], g_ref[...]).astype(BF16)

    def mm(lo, n):
        return _dot_nt(h, w1_ref[lo:lo + n, :])

    ca, sa, cb, sb = ca_ref[...], sa_ref[...], cb_ref[...], sb_ref[...]

    lane = lax.broadcasted_iota(jnp.int32, (1, LANES), 1)
    low_head = lane < SWA_HEAD_DIM
    first_half = lane % SWA_HEAD_DIM < SWA_HEAD_DIM // 2

    def roll(v, shift):
        return pltpu.roll(v, shift, axis=1)

    def rope64(v, cos, sin):
        half = SWA_HEAD_DIM // 2
        rot = jnp.where(first_half, -roll(v, LANES - half), roll(v, half))
        return v * cos + rot * sin

    cqn = _rms(mm(OFF_CQ, MLA_Q_LORA), gq_ref[...]).astype(BF16)
    ckvn = _rms(mm(OFF_CKV, MLA_KV_LORA), gkv_ref[...]).astype(BF16)
    kpe_blk = _dot_nt(h, wrope_ref[...])

    q_scale = SWA_HEAD_DIM ** -0.5 * LOG2E
    qs = mm(OFF_QS, 512)
    for p in range(HEAD_PAIRS):
        blk = slice(p * LANES, (p + 1) * LANES)
        qs_ref[:, blk] = rope64(qs[:, blk], cb * q_scale, sb * q_scale).astype(BF16)
    ks = rope64(mm(OFF_KS, LANES), cb, sb)
    ks_swap = roll(ks, SWA_HEAD_DIM)
    ks_ref[:, :LANES] = jnp.where(low_head, ks, ks_swap).astype(BF16)
    ks_ref[:, LANES:] = jnp.where(low_head, ks_swap, ks).astype(BF16)
    vs = mm(OFF_VS, LANES)
    vs_swap = roll(vs, SWA_HEAD_DIM)
    for n, blk_val in enumerate([jnp.where(low_head, vs, 1.0), jnp.where(low_head, 1.0, vs_swap),
                                 jnp.where(low_head, vs_swap, 1.0), jnp.where(low_head, 1.0, vs)]):
        vs_ref[:, n * LANES:(n + 1) * LANES] = blk_val.astype(BF16)

    qb_ref[...] = (mm(OFF_QB, 512) * (SB_HEAD_DIM ** -0.5 * LOG2E)).astype(BF16)
    kb_ref[...] = mm(OFF_KB, 512).astype(BF16)
    vb_ref[...] = mm(OFF_VB, 512).astype(BF16)

    scale = (MLA_NOPE + MLA_ROPE) ** -0.5 * LOG2E
    ca8 = jnp.concatenate([ca * scale] * MLA_HEADS, axis=1)
    sa8 = jnp.concatenate([sa * scale] * MLA_HEADS, axis=1)
    nq = MLA_HEADS * MLA_HEAD_PAD
    qm_ref[...] = (_dot(cqn, wq_ref[:, :nq]) * ca8 + _dot(cqn, wq_ref[:, nq:]) * sa8).astype(BF16)
    rope_lanes = (lane >= MLA_NOPE) & (lane < MLA_NOPE + MLA_ROPE)
    kpe = jnp.where(rope_lanes, kpe_blk * ca + roll(kpe_blk, MLA_NOPE) * sa, 0.0)
    kpe8 = jnp.concatenate([kpe] * MLA_HEADS, axis=1)
    km_ref[...] = (_dot(ckvn, wkv_ref[:, :nq]) + kpe8).astype(BF16)
    vm_ref[...] = (_dot(ckvn, wkv_ref[:, nq:]) + _pair_ones(nq)).astype(BF16)

    gate_ref[...] = jax.nn.sigmoid(mm(OFF_GATE, N_BRANCHES * D_MODEL) + bg_ref[...])


def _prep(x, tables, layer, g_pre, w1, wrope, b_gate, g_q, g_kv, wq, wkv):
    t = x.shape[0]
    tm = PREP_TM
    ca, sa, cb, sb = tables

    def row(n):
        return pl.BlockSpec((tm, n), lambda i: (i, 0))

    def out(n, dt=BF16):
        return jax.ShapeDtypeStruct((t, n), dt)

    nq = MLA_HEADS * MLA_HEAD_PAD
    consts = [w1, wrope, b_gate, g_q, g_kv, wq, wkv]
    return pl.pallas_call(
        _prep_kernel,
        grid=(t // tm,),
        in_specs=[row(D_MODEL), _layer_spec(g_pre, layer), row(LANES), row(LANES), row(LANES), row(LANES)]
                 + [_layer_spec(c, layer) for c in consts],
        out_specs=[row(nq), row(nq), row(nq), row(512), row(256), row(512), row(512), row(512), row(512),
                   row(N_BRANCHES * D_MODEL)],
        out_shape=[out(nq), out(nq), out(nq), out(512), out(256), out(512), out(512), out(512), out(512),
                   out(N_BRANCHES * D_MODEL, F32)],
        compiler_params=_params(("parallel",)),
        name="prep",
    )(x, g_pre, ca, sa, cb, sb, *consts)


def _mla_kernel(q_ref, k_ref, v_ref, o_ref, m_ref, acc_ref):
    qi = pl.program_id(2)
    tq, tk = MLA_TQ, MLA_TK
    n_heads = 2 * MLA_PAIRS
    m_ref[...] = jnp.full(m_ref.shape, NEG_BIG, F32)
    acc_ref[...] = jnp.zeros(acc_ref.shape, F32)

    def head_lanes(h):
        return slice(h * MLA_HEAD_PAD, (h + 1) * MLA_HEAD_PAD)

    def process(row_sets):
        units = [(rows, chunks, h) for rows, chunks in row_sets for h in range(n_heads)]

        def key_rows(first, keys):
            return pl.ds(pl.multiple_of(first, MLA_DIAG), keys)

        def scores(unit):
            rows, chunks, h = unit
            return [_dot_nt(q_ref[rows, head_lanes(h)], k_ref[key_rows(first, keys), head_lanes(h)])
                    for first, keys, _ in chunks]

        ahead = scores(units[0])
        for n, (rows, chunks, h) in enumerate(units):
            s_unit, ahead = ahead, (scores(units[n + 1]) if n + 1 < len(units) else None)
            n_rows = rows.stop - rows.start
            parts = []
            for (_, keys, masked), s in zip(chunks, s_unit):
                if masked:
                    assert keys == n_rows
                    row = lax.broadcasted_iota(jnp.int32, (n_rows, keys), 0)
                    col = lax.broadcasted_iota(jnp.int32, (n_rows, keys), 1)
                    s = jnp.where(col <= row, s, NEG_BIG)
                parts.append(s)
            s = parts[0] if len(parts) == 1 else jnp.concatenate(parts, axis=1)
            width = sum(keys for _, keys, _ in chunks)
            v = v_ref[key_rows(chunks[0][0], width), head_lanes(h)]
            m_old = m_ref[h, rows]
            m_new = jnp.maximum(m_old, jnp.max(s, axis=-1, keepdims=True))
            alpha = jnp.exp2(m_old - m_new)
            p = jnp.exp2(s - jnp.concatenate([m_new] * (width // LANES), axis=1))
            acc_ref[h, rows] = alpha * acc_ref[h, rows] + _dot(p.astype(BF16), v)
            m_ref[h, rows] = m_new

    assert tq == 2 * tk

    def body(i, carry):
        process([(slice(0, tq), [(2 * i * tk, tk, False), ((2 * i + 1) * tk, tk, False)])])
        return carry

    lax.fori_loop(0, qi, body, 0)

    base = qi * tq
    diagonal = []
    for r in range(tq // MLA_DIAG):
        before = [(base, r * MLA_DIAG, False)] if r else []
        diagonal.append((slice(r * MLA_DIAG, (r + 1) * MLA_DIAG),
                         before + [(base + r * MLA_DIAG, MLA_DIAG, True)]))
    process(diagonal)

    first_half = lax.broadcasted_iota(jnp.int32, (tq, LANES), 1) < MLA_V
    for p in range(MLA_PAIRS):
        a0, a1 = acc_ref[2 * p], acc_ref[2 * p + 1]
        num = jnp.where(first_half, a0, a1)
        den = jnp.where(first_half, pltpu.roll(a0, MLA_V, axis=1), pltpu.roll(a1, MLA_V, axis=1))
        o_ref[:, p * LANES:(p + 1) * LANES] = (num / den).astype(o_ref.dtype)


def _mla_attention(qm, km, vm, batch, seq):
    t = qm.shape[0]
    tq = MLA_TQ
    nq = seq // tq
    wide = MLA_PAIRS * 2 * MLA_HEAD_PAD
    return pl.pallas_call(
        _mla_kernel,
        grid=(batch, HEAD_PAIRS // MLA_PAIRS, nq),
        in_specs=[pl.BlockSpec((tq, wide), lambda b, p, i: (b * nq + i, p)),
                  pl.BlockSpec((seq, wide), lambda b, p, i: (b, p)),
                  pl.BlockSpec((seq, wide), lambda b, p, i: (b, p))],
        out_specs=pl.BlockSpec((tq, MLA_PAIRS * LANES), lambda b, p, i: (b * nq + i, p)),
        out_shape=jax.ShapeDtypeStruct((t, HEAD_PAIRS * LANES), BF16),
        scratch_shapes=[pltpu.VMEM((2 * MLA_PAIRS, tq, LANES), F32),
                        pltpu.VMEM((2 * MLA_PAIRS, tq, LANES), F32)],
        compiler_params=_params(("parallel", "parallel", "parallel")),
        name="mla_attention",
    )(qm, km, vm)


def _sb_kernel(q_ref, k_ref, v_ref, tri_ref, o_ref, qh_ref, carry_ref, acc_ref):
    qi = pl.program_id(2)
    tq, tk, blk = SB_TQ, SB_TK, SB_BLK
    carry_ref[...] = jnp.zeros(carry_ref.shape, F32)
    acc_ref[...] = jnp.zeros(acc_ref.shape, F32)
    n_heads = 2 * SB_PAIRS
    first_half = lax.broadcasted_iota(jnp.int32, (tq, LANES), 1) < SB_HEAD_DIM
    for p in range(SB_PAIRS):
        q2 = q_ref[:, p * LANES:(p + 1) * LANES]
        zero = jnp.zeros_like(q2)
        qh_ref[2 * p] = jnp.where(first_half, q2, zero)
        qh_ref[2 * p + 1] = jnp.where(first_half, zero, q2)

    def pair_lanes(h):
        return slice((h // 2) * LANES, (h // 2 + 1) * LANES)

    def weights_pv(h, rows, z, v, masked):
        sp = jnp.maximum(z, jnp.log2(1.0 + jnp.exp2(jnp.minimum(z, SP_CLAMP))))
        own = z - sp
        if masked:
            row = lax.broadcasted_iota(jnp.int32, (tk, tk), 0)
            col = lax.broadcasted_iota(jnp.int32, (tk, tk), 1)
            valid = col < row
            sp = jnp.where(valid, sp, 0.0)
        sp16 = sp.astype(BF16)
        c = carry_ref[h, rows]
        expo = [None] * (tk // blk)
        for b in reversed(range(tk // blk)):
            cols = slice(b * blk, (b + 1) * blk)
            later = _dot(sp16[:, cols], tri_ref[...])
            expo[b] = own[:, cols] - later - jnp.concatenate([c] * (blk // LANES), axis=1)
            c = c + jnp.sum(sp[:, cols], axis=-1, keepdims=True)
        carry_ref[h, rows] = c
        a = jnp.exp2(jnp.concatenate(expo, axis=1))
        if masked:
            a = jnp.where(valid, a, 0.0)
        acc_ref[h, rows] += _dot(a.astype(BF16), v)

    def process(items):
        def chunk(j):
            return pl.ds(pl.multiple_of(j * tk, tk), tk)

        logits = [[_dot_nt(qh_ref[h, rows], k_ref[chunk(j), pair_lanes(h)]) for h in range(n_heads)]
                  for rows, j, _ in items]
        for (rows, j, masked), z_heads in zip(items, logits):
            for h, z in enumerate(z_heads):
                weights_pv(h, rows, z, v_ref[chunk(j), pair_lanes(h)], masked)

    def min_carry(rows):
        c = carry_ref[0, rows]
        for h in range(1, n_heads):
            c = jnp.minimum(c, carry_ref[h, rows])
        return jnp.min(c)

    n_blocks = tq // tk
    blocks = [slice(r * tk, (r + 1) * tk) for r in range(n_blocks)]
    diag_chunk = [qi * n_blocks + r for r in range(n_blocks)]
    diagonal = [(blocks[r], diag_chunk[r], True) for r in range(n_blocks)]

    @pl.when(qi == 0)
    def _():
        process(diagonal + [(blocks[r], diag_chunk[r] - 1, False) for r in range(1, n_blocks)])

    @pl.when(qi > 0)
    def _():
        process(diagonal + [(blocks[r], diag_chunk[r] - 1, False) for r in range(n_blocks)])

    def more(state):
        j, cmin = state
        return jnp.logical_and(j >= 0, cmin < SB_SKIP)

    for r in range(n_blocks):
        def step(state, rows=blocks[r]):
            j, _ = state
            process([(rows, j, False)])
            return j - 1, min_carry(rows)

        lax.while_loop(more, step, (diag_chunk[r] - 2, min_carry(blocks[r])))

    for p in range(SB_PAIRS):
        o_ref[:, p * LANES:(p + 1) * LANES] = jnp.where(
            first_half, acc_ref[2 * p], acc_ref[2 * p + 1]).astype(o_ref.dtype)


def _sb_attention(qb, kb, vb, tri, batch, seq):
    t = qb.shape[0]
    tq = SB_TQ
    nq = seq // tq
    wide = SB_PAIRS * LANES
    return pl.pallas_call(
        _sb_kernel,
        grid=(batch, HEAD_PAIRS // SB_PAIRS, nq),
        in_specs=[pl.BlockSpec((tq, wide), lambda b, p, i: (b * nq + i, p)),
                  pl.BlockSpec((seq, wide), lambda b, p, i: (b, p), pipeline_mode=pl.Buffered(1)),
                  pl.BlockSpec((seq, wide), lambda b, p, i: (b, p), pipeline_mode=pl.Buffered(1)),
                  _const_spec(tri.shape)],
        out_specs=pl.BlockSpec((tq, wide), lambda b, p, i: (b * nq + i, p)),
        out_shape=jax.ShapeDtypeStruct((t, HEAD_PAIRS * LANES), BF16),
        scratch_shapes=[pltpu.VMEM((2 * SB_PAIRS, tq, LANES), BF16), pltpu.VMEM((2 * SB_PAIRS, tq, LANES), F32),
                        pltpu.VMEM((2 * SB_PAIRS, tq, LANES), F32)],
        compiler_params=_params(("parallel", "parallel", "parallel")),
        name="sb_attention",
    )(qb, kb, vb, tri)


def _swa_kernel(sink_ref, q_ref, k_ref, v_ref, kp_ref, vp_ref, o_ref, *, layer, tiles_per_seq):
    i = pl.program_id(0)
    w, tq = SWA_WINDOW, SWA_TQ
    has_prev = (i % tiles_per_seq) != 0
    kcat = jnp.concatenate([kp_ref[...], k_ref[...]], axis=0)
    vcat = jnp.concatenate([vp_ref[...], v_ref[...]], axis=0)
    row = lax.broadcasted_iota(jnp.int32, (w, 2 * w), 0)
    col = lax.broadcasted_iota(jnp.int32, (w, 2 * w), 1)
    band = (col > row) & (col <= row + w)
    band_first = band & ((col >= w) | has_prev)
    first_half = lax.broadcasted_iota(jnp.int32, (w, LANES), 1) < SWA_HEAD_DIM
    def kv_head(p):
        return p // (HEAD_PAIRS // SWA_KV_HEADS)

    def scores(p):
        q2 = q_ref[:, p * LANES:(p + 1) * LANES]
        kg = kcat[:, kv_head(p) * LANES:(kv_head(p) + 1) * LANES]
        half = lax.broadcasted_iota(jnp.int32, q2.shape, 1) < SWA_HEAD_DIM
        zero = jnp.zeros_like(q2)
        return [[_dot_nt(qh[r * w:(r + 1) * w], kg[r * w:(r + 2) * w]) for r in range(tq // w)]
                for qh in (jnp.where(half, q2, zero), jnp.where(half, zero, q2))]

    ahead = scores(0)
    for p in range(HEAD_PAIRS):
        g = kv_head(p)
        s_full, ahead = ahead, (scores(p + 1) if p + 1 < HEAD_PAIRS else None)
        for r in range(tq // w):
            keys = slice(r * w, (r + 2) * w)
            acc, esink = [], []
            for hh in range(2):
                sink = sink_ref[layer, 2 * p + hh] * LOG2E
                s = jnp.where(band_first if r == 0 else band, s_full[hh][r], NEG_BIG)
                m = jnp.maximum(jnp.broadcast_to(jnp.max(s, axis=-1, keepdims=True), (w, LANES)), sink)
                prob = jnp.exp2(s - jnp.concatenate([m, m], axis=1))
                vh = vcat[keys, (2 * g + hh) * LANES:(2 * g + hh + 1) * LANES]
                acc.append(_dot(prob.astype(BF16), vh))
                esink.append(jnp.exp2(sink - m))
            num = jnp.where(first_half, acc[0], acc[1])
            den = (jnp.where(first_half, pltpu.roll(acc[0], SWA_HEAD_DIM, axis=1),
                             pltpu.roll(acc[1], SWA_HEAD_DIM, axis=1))
                   + jnp.where(first_half, esink[0], esink[1]))
            o_ref[r * w:(r + 1) * w, p * LANES:(p + 1) * LANES] = (num / den).astype(o_ref.dtype)


def _swa_attention(sinks, layer, qs, ks, vs, seq):
    t = qs.shape[0]
    tq = SWA_TQ
    per_tile = tq // SWA_WINDOW
    cur = lambda n: pl.BlockSpec((tq, n), lambda i: (i, 0))
    prev = lambda n: pl.BlockSpec((SWA_WINDOW, n), lambda i: (jnp.maximum(i * per_tile - 1, 0), 0))
    return pl.pallas_call(
        functools.partial(_swa_kernel, layer=layer, tiles_per_seq=seq // tq),
        grid=(t // tq,),
        in_specs=[pl.BlockSpec(memory_space=pltpu.SMEM), cur(512), cur(256), cur(512), prev(256), prev(512)],
        out_specs=cur(512),
        out_shape=jax.ShapeDtypeStruct((t, 512), BF16),
        compiler_params=_params(("parallel",)),
        name="swa_attention",
    )(sinks, qs, ks, vs, ks, vs)


def _post_kernel(x_ref, oa_ref, ob_ref, oc_ref, gate_ref, woa_ref, wob_ref, woc_ref, wout_ref,
                 gpost_ref, gpre_ref, wup_ref, wdown_ref, gmlp_ref, out_ref):
    d = D_MODEL
    tm = x_ref.shape[0]
    halves = [slice(0, tm // 2), slice(tm // 2, tm)]
    merged = []
    for r in halves:
        mixed = (gate_ref[r, 0:d] * _dot(oa_ref[r, :], woa_ref[...])
                 + gate_ref[r, d:2 * d] * _dot(ob_ref[r, :], wob_ref[...])
                 + gate_ref[r, 2 * d:3 * d] * _dot(oc_ref[r, :], woc_ref[...]))
        merged.append(_dot(mixed.astype(BF16), wout_ref[...]))
    x1 = [x_ref[r, :] + _rms(y, gpost_ref[...]) for r, y in zip(halves, merged)]
    hidden = [_rms(v, gpre_ref[...]).astype(BF16) for v in x1]
    up = [jnp.square(jnp.maximum(_dot(h, wup_ref[...]), 0.0)).astype(BF16) for h in hidden]
    down = [_dot(u, wdown_ref[...]) for u in up]
    for r, v, y in zip(halves, x1, down):
        out_ref[r, :] = v + _rms(y, gmlp_ref[...])


def _post(x, oa, ob, oc, gates, layer, woa, wob, woc, wout, g_post, g_pre, wup, wdown, g_mlp):
    t = x.shape[0]
    tm = POST_TM
    row = lambda n: pl.BlockSpec((tm, n), lambda i: (i, 0))
    consts = [woa, wob, woc, wout, g_post, g_pre, wup, wdown, g_mlp]
    return pl.pallas_call(
        _post_kernel,
        grid=(t // tm,),
        in_specs=[row(D_MODEL), row(512), row(512), row(512), row(N_BRANCHES * D_MODEL)]
                 + [_layer_spec(c, layer) for c in consts],
        out_specs=row(D_MODEL),
        out_shape=jax.ShapeDtypeStruct((t, D_MODEL), F32),
        compiler_params=_params(("parallel",)),
        name="post",
    )(x, oa, ob, oc, gates, *consts)


def _rot_half(w):
    half = w.shape[-1] // 2
    return jnp.concatenate([-w[..., half:], w[..., :half]], axis=-1)


def _pair_pad(a, b):
    z = jnp.zeros_like(a)
    return jnp.concatenate([a, z, z, b], axis=-1).reshape(*a.shape[:-2], -1)


def _layout_weights(w_in, w_uq, w_ukv):
    dep = w_in.shape[0]
    w1 = jnp.swapaxes(w_in, 1, 2).astype(BF16)
    assert w1.shape[1] == W1_COLS
    k_rope = w1[:, OFF_KROPE:OFF_KROPE + MLA_ROPE, :]
    half = MLA_ROPE // 2
    k_rope_rot = jnp.concatenate([-k_rope[:, half:], k_rope[:, :half]], axis=1)
    z32 = jnp.zeros((dep, 32, D_MODEL), BF16)
    wrope = jnp.concatenate([k_rope_rot, z32, k_rope, z32], axis=1)

    uq = w_uq.astype(BF16).reshape(dep, MLA_Q_LORA, MLA_HEADS, MLA_NOPE + MLA_ROPE)
    nope, pe = uq[..., :MLA_NOPE], uq[..., MLA_NOPE:]
    pe_rot = _rot_half(pe)
    zq = jnp.zeros((dep, MLA_Q_LORA, MLA_HEADS, 32), BF16)
    wq_main = jnp.concatenate([nope, pe, zq], axis=-1).reshape(dep, MLA_Q_LORA, -1)
    wq_rot = jnp.concatenate([jnp.zeros_like(nope), pe_rot, zq], axis=-1).reshape(dep, MLA_Q_LORA, -1)
    wq = jnp.concatenate([wq_main, wq_rot], axis=-1)

    ukv = w_ukv.astype(BF16).reshape(dep, MLA_KV_LORA, MLA_HEADS, MLA_NOPE + MLA_V)
    k_nope, v = ukv[..., :MLA_NOPE], ukv[..., MLA_NOPE:]
    wk = jnp.concatenate([k_nope, jnp.zeros_like(k_nope)], axis=-1).reshape(dep, MLA_KV_LORA, -1)
    vp = v.reshape(dep, MLA_KV_LORA, HEAD_PAIRS, 2, MLA_V)
    wkv = jnp.concatenate([wk, _pair_pad(vp[..., 0, :], vp[..., 1, :])], axis=-1)
    return w1, wrope, wq, wkv


def kernel(x, positions, g_mix_pre, w_in, b_gate, g_q_lat, g_kv_lat, w_uq, w_ukv, swa_sinks, w_o_mla, w_o_swa, w_o_sb, w_out, g_mix_post, g_mlp_pre, w_up, w_down, g_mlp_post):
    batch, seq, d = x.shape
    t = batch * seq
    tables = _rope_tables(positions)
    w1, wrope, wq, wkv = _layout_weights(w_in, w_uq, w_ukv)
    woa, wob, woc, wout, wup, wdown = (w.astype(BF16) for w in (w_o_mla, w_o_swa, w_o_sb, w_out, w_up, w_down))
    tri = (lax.broadcasted_iota(jnp.int32, (SB_BLK, SB_BLK), 0)
           > lax.broadcasted_iota(jnp.int32, (SB_BLK, SB_BLK), 1)).astype(BF16)
    rows = lambda g: g.reshape(DEPTH, 1, -1)
    g_pre, bg, g_q, g_kv = rows(g_mix_pre), rows(b_gate), rows(g_q_lat), rows(g_kv_lat)
    g_post, g_mlp_in, g_mlp_out = rows(g_mix_post), rows(g_mlp_pre), rows(g_mlp_post)

    xt = x.reshape(t, d)
    for l in range(DEPTH):
        qm, km, vm, qs, ks, vs, qb, kb, vb, gates = _prep(xt, tables, l, g_pre, w1, wrope, bg, g_q, g_kv,
                                                          wq, wkv)
        oa = _mla_attention(qm, km, vm, batch, seq)
        ob = _swa_attention(swa_sinks, l, qs, ks, vs, seq)
        oc = _sb_attention(qb, kb, vb, tri, batch, seq)
        xt = _post(xt, oa, ob, oc, gates, l, woa, wob, woc, wout, g_post, g_mlp_in, wup, wdown, g_mlp_out)
    return xt.reshape(batch, seq, d)
```
